```python
import math
import jax, jax.numpy as jnp
from jax import lax
import numpy as np

D_MODEL = 1024
BATCH = 8
SEQ = 2048
DEPTH = 2

N_META = 16
BLOCK = 128
PAD = BLOCK - N_META
WINDOW = 128
ROPE_THETA = 10000.0
HEAD_DIM = 64

A_HEADS = 4
A_KV_HEADS = 2
A_WIDTH = A_HEADS * HEAD_DIM
B_HEADS = 4
B_QK_DIM = HEAD_DIM
B_V_DIM = 2 * HEAD_DIM
B_WIDTH = B_HEADS * B_V_DIM
C_HEADS = 4
C_DK = HEAD_DIM
C_DV = HEAD_DIM
C_WIDTH = C_HEADS * C_DV
C_CHUNK = 16

D_MIX = A_WIDTH + B_WIDTH + C_WIDTH
IN_SPLITS = (A_WIDTH, A_KV_HEADS * HEAD_DIM, A_KV_HEADS * HEAD_DIM, A_WIDTH,
             2 * B_HEADS * B_QK_DIM, 2 * B_HEADS * B_QK_DIM, B_WIDTH, B_WIDTH,
             C_HEADS * C_DK, C_WIDTH, C_HEADS * C_DK, C_HEADS * C_DK, C_WIDTH)
N_IN = sum(IN_SPLITS)

ALPHA = (2 * DEPTH) ** 0.25
BETA = (8 * DEPTH) ** -0.25
LN_EPS = 1e-5
RMS_EPS = 1e-6
NEG = -1e30
F_MIN = 1e-30

kernel_name = "hymba_style_bidir_hybrid_block"


def layer_norm(x, g, b):
    xf = x.astype(jnp.float32)
    mu = jnp.mean(xf, axis=-1, keepdims=True)
    var = jnp.mean(jnp.square(xf - mu), axis=-1, keepdims=True)
    y = (xf - mu) * lax.rsqrt(var + LN_EPS) * g.astype(jnp.float32) + b.astype(jnp.float32)
    return y.astype(x.dtype)


def rms_norm(x, w):
    xf = x.astype(jnp.float32)
    return xf * lax.rsqrt(jnp.mean(jnp.square(xf), axis=-1, keepdims=True) + RMS_EPS) * w.astype(jnp.float32)


def rope_tables(lp):
    pos = (jnp.arange(lp) - PAD).astype(jnp.float32)
    inv = ROPE_THETA ** (-jnp.arange(0, HEAD_DIM, 2, dtype=jnp.float32) / HEAD_DIM)
    ang = pos[:, None] * inv[None, :]
    return jnp.cos(ang)[:, None, :], jnp.sin(ang)[:, None, :]


def rotary(x, cos, sin):
    xf = x.astype(jnp.float32)
    x1, x2 = jnp.split(xf, 2, axis=-1)
    return jnp.concatenate([x1 * cos - x2 * sin, x2 * cos + x1 * sin], axis=-1).astype(x.dtype)


def window_gqa(q, k, v, sink):
    bsz, lp = q.shape[:2]
    nb = lp // BLOCK
    grp = A_HEADS // A_KV_HEADS
    qb = q.reshape(bsz, nb, BLOCK, A_KV_HEADS, grp, HEAD_DIM)

    def band(t):
        tb = t.reshape(bsz, nb, BLOCK, A_KV_HEADS, HEAD_DIM)
        tp = jnp.pad(tb, ((0, 0), (1, 1), (0, 0), (0, 0), (0, 0)))
        return jnp.concatenate([tp[:, :-2], tp[:, 1:-1], tp[:, 2:]], axis=2)

    k_band, v_band = band(k), band(v)
    k_meta, v_meta = k[:, PAD:PAD + N_META], v[:, PAD:PAD + N_META]
    scale = HEAD_DIM ** -0.5
    s_band = jnp.einsum('bnqhgd,bnkhd->bnhgqk', qb, k_band).astype(jnp.float32) * scale
    s_meta = jnp.einsum('bnqhgd,bmhd->bnhgqm', qb, k_meta).astype(jnp.float32) * scale
    blk = jnp.arange(nb)[:, None]
    q_pos = blk * BLOCK + jnp.arange(BLOCK)[None, :]
    k_pos = (blk - 1) * BLOCK + jnp.arange(3 * BLOCK)[None, :]
    kp = k_pos[:, None, :]
    ok = (jnp.abs(q_pos[:, :, None] - kp) <= WINDOW) & (kp >= PAD + N_META) & (kp < lp)
    s_band = jnp.where(ok[None, :, None, None], s_band, NEG)
    sink_logit = jnp.broadcast_to(sink.astype(jnp.float32).reshape(1, 1, A_KV_HEADS, grp, 1, 1),
                                  s_meta.shape[:-1] + (1,))
    p = jax.nn.softmax(jnp.concatenate([sink_logit, s_meta, s_band], axis=-1), axis=-1).astype(v.dtype)
    out = (jnp.einsum('bnhgqm,bmhd->bnqhgd', p[..., 1:1 + N_META], v_meta)
           + jnp.einsum('bnhgqk,bnkhd->bnqhgd', p[..., 1 + N_META:], v_band))
    return out.reshape(bsz, lp, A_HEADS, HEAD_DIM)


def diff_attention(q, k, v, lam, key_ok):
    bsz, lp = q.shape[:2]
    nb = lp // BLOCK
    qb = q.reshape(bsz, nb, BLOCK, B_HEADS, 2, B_QK_DIM).swapaxes(0, 1)
    scale = B_QK_DIM ** -0.5

    def attend(q_blk):
        s = jnp.einsum('bqhcd,bkhcd->bhcqk', q_blk, k).astype(jnp.float32) * scale
        s = jnp.where(key_ok, s, NEG)
        p = jax.nn.softmax(s, axis=-1)
        a = p[:, :, 0] - lam * p[:, :, 1]
        return jnp.einsum('bhqk,bkhv->bqhv', a.astype(v.dtype), v)

    out = lax.map(attend, qb)
    return out.swapaxes(0, 1).reshape(bsz, lp, B_HEADS, B_V_DIM)


def hgrn2_chunked(q, k, v, logf):
    bsz, lp, nh, dk = q.shape
    dv = v.shape[-1]
    n = lp // C_CHUNK
    q, k, v, logf = (t.reshape(bsz, n, C_CHUNK, nh, t.shape[-1]) for t in (q, k, v, logf))
    b = jnp.cumsum(logf, axis=2)
    incl = jnp.tril(jnp.ones((C_CHUNK, C_CHUNK), dtype=bool))
    diff = b[:, :, :, None] - b[:, :, None, :]
    decay = jnp.exp(jnp.where(incl[None, None, :, :, None, None], diff, NEG))
    scores = jnp.einsum('bnthk,bnshk,bntshk->bnhts', q, k, decay)
    intra = jnp.einsum('bnhts,bnshv->bnthv', scores, v)
    b_last = b[:, :, -1]
    u = jnp.einsum('bnshk,bnshv->bnhkv', k * jnp.exp(b_last[:, :, None] - b), v)
    d_chunk = jnp.exp(b_last)

    def step(s, inp):
        dc, uc = inp
        return dc[..., None] * s + uc, s

    s0 = jnp.zeros((bsz, nh, dk, dv), jnp.float32)
    _, s_prev = lax.scan(step, s0, (d_chunk.swapaxes(0, 1), u.swapaxes(0, 1)))
    s_prev = s_prev.swapaxes(0, 1)
    inter = jnp.einsum('bnthk,bnhkv->bnthv', q * jnp.exp(b), s_prev)
    return (intra + inter).reshape(bsz, lp, nh, dv)


def mixer_layer(x, w_in, w_out, sink, lam_params, subln_w, lb, c_norm_w, lam_init, cos, sin, valid):
    bsz, seq_len, _ = x.shape
    lp = seq_len + PAD
    dt = x.dtype
    h = jnp.pad(x @ w_in, ((0, 0), (PAD, 0), (0, 0)))
    idx = [int(i) for i in np.cumsum(IN_SPLITS)[:-1]]
    (qa, ka, va, ga, qb, kb, vb, gb, qc, ic, zf, zb, gc) = jnp.split(h, idx, axis=-1)

    qa = rotary(qa.reshape(bsz, lp, A_HEADS, HEAD_DIM), cos, sin)
    ka = rotary(ka.reshape(bsz, lp, A_KV_HEADS, HEAD_DIM), cos, sin)
    va = va.reshape(bsz, lp, A_KV_HEADS, HEAD_DIM)
    ya = window_gqa(qa, ka, va, sink).reshape(bsz, lp, A_WIDTH) * jax.nn.silu(ga)

    qb = rotary(qb.reshape(bsz, lp, 2 * B_HEADS, B_QK_DIM), cos, sin).reshape(bsz, lp, B_HEADS, 2, B_QK_DIM)
    kb = rotary(kb.reshape(bsz, lp, 2 * B_HEADS, B_QK_DIM), cos, sin).reshape(bsz, lp, B_HEADS, 2, B_QK_DIM)
    vb = vb.reshape(bsz, lp, B_HEADS, B_V_DIM)
    lp32 = lam_params.astype(jnp.float32)
    lam = jnp.exp(jnp.sum(lp32[0] * lp32[1])) - jnp.exp(jnp.sum(lp32[2] * lp32[3])) + lam_init
    ob = diff_attention(qb, kb, vb, lam, valid)
    ob = rms_norm(ob, subln_w.reshape(B_HEADS, B_V_DIM)) * (1.0 - lam_init)
    yb = ob.reshape(bsz, lp, B_WIDTH).astype(dt) * jax.nn.silu(gb)

    qc = jax.nn.silu(qc).reshape(bsz, lp, C_HEADS, C_DK).astype(jnp.float32)
    ic = ic.reshape(bsz, lp, C_HEADS, C_DV).astype(jnp.float32)
    lb_h = lb.reshape(C_HEADS, C_DK)
    vmask = valid[None, :, None, None]

    def gates(z):
        z = z.reshape(bsz, lp, C_HEADS, C_DK).astype(jnp.float32)
        f = lb_h + (1.0 - lb_h) * jax.nn.sigmoid(z)
        logf = jnp.log(jnp.maximum(f, F_MIN))
        kk = (1.0 - lb_h) * jax.nn.sigmoid(-z)
        return jnp.where(vmask, kk, 0.0), jnp.where(vmask, logf, 0.0)

    k_f, logf_f = gates(zf)
    k_b, logf_b = gates(zb)
    y_f = hgrn2_chunked(qc, k_f, ic, logf_f)
    flip = lambda t: jnp.flip(t, axis=1)
    y_b = flip(hgrn2_chunked(flip(qc), flip(k_b), flip(ic), flip(logf_b)))
    oc = rms_norm(y_f + y_b, c_norm_w.reshape(C_HEADS, C_DV))
    yc = oc.reshape(bsz, lp, C_WIDTH).astype(dt) * jax.nn.silu(gc)

    y = jnp.concatenate([ya, yb, yc], axis=-1)[:, PAD:]
    return y @ w_out


def setup_inputs(seed: int = 0) -> dict:
    key = jax.random.key(seed)
    ks = jax.random.split(key, 14)
    nrm = jax.random.normal
    return {
        "x": nrm(ks[0], (BATCH, SEQ, D_MODEL), jnp.float32),
        "meta": nrm(ks[1], (N_META, D_MODEL), jnp.float32),
        "emb_ln_g": 1.0 + 0.02 * nrm(ks[2], (D_MODEL,), jnp.float32),
        "emb_ln_b": 0.02 * nrm(ks[3], (D_MODEL,), jnp.float32),
        "w_in": nrm(ks[4], (DEPTH, D_MODEL, N_IN), jnp.float32) * D_MODEL ** -0.5,
        "w_out": nrm(ks[5], (DEPTH, D_MIX, D_MODEL), jnp.float32) * (D_MIX ** -0.5) * BETA,
        "a_sink": 0.5 * nrm(ks[6], (DEPTH, A_HEADS), jnp.float32),
        "b_lam": 0.1 * nrm(ks[7], (DEPTH, 4, B_QK_DIM), jnp.float32),
        "b_subln_w": 1.0 + 0.02 * nrm(ks[8], (DEPTH, B_WIDTH), jnp.float32),
        "c_lb_logits": 0.5 * nrm(ks[9], (DEPTH, C_HEADS * C_DK), jnp.float32),
        "c_norm_w": 1.0 + 0.02 * nrm(ks[10], (DEPTH, C_WIDTH), jnp.float32),
        "ln_g": 1.0 + 0.02 * nrm(ks[11], (DEPTH, D_MODEL), jnp.float32),
        "ln_b": 0.02 * nrm(ks[12], (DEPTH, D_MODEL), jnp.float32),
    }


def reference(x, meta, emb_ln_g, emb_ln_b, w_in, w_out, a_sink, b_lam, b_subln_w, c_lb_logits, c_norm_w, ln_g, ln_b):
    bsz = x.shape[0]
    h = jnp.concatenate([jnp.broadcast_to(meta[None].astype(x.dtype), (bsz, N_META, D_MODEL)), x], axis=1)
    h = layer_norm(h, emb_ln_g, emb_ln_b)
    lp = h.shape[1] + PAD
    cos, sin = rope_tables(lp)
    valid = jnp.arange(lp) >= PAD
    p_lb = jax.nn.softmax(c_lb_logits.astype(jnp.float32), axis=0)
    lb_all = jnp.cumsum(p_lb, axis=0) - p_lb[0:1]
    for l in range(DEPTH):
        lam_init = 0.8 - 0.6 * math.exp(-0.3 * l)
        y = mixer_layer(h, w_in[l], w_out[l], a_sink[l], b_lam[l], b_subln_w[l], lb_all[l], c_norm_w[l],
                        lam_init, cos, sin, valid)
        h = layer_norm(ALPHA * h + y, ln_g[l], ln_b[l])
    return h[:, N_META:]
```

```python
import functools
import math

import jax
import jax.numpy as jnp
import numpy as np
from jax import lax
from jax.experimental import pallas as pl
from jax.experimental.pallas import tpu as pltpu

D_MODEL = 1024
BATCH = 8
SEQ = 2048
DEPTH = 2
N_META = 16
BLOCK = 128
PAD = BLOCK - N_META
WINDOW = 128
ROPE_THETA = 10000.0
HEAD_DIM = 64
A_HEADS = 4
A_KV_HEADS = 2
A_WIDTH = A_HEADS * HEAD_DIM
B_HEADS = 4
B_V_DIM = 2 * HEAD_DIM
B_WIDTH = B_HEADS * B_V_DIM
C_HEADS = 4
C_WIDTH = C_HEADS * HEAD_DIM
D_MIX = A_WIDTH + B_WIDTH + C_WIDTH
ALPHA = (2 * DEPTH) ** 0.25
LN_EPS = 1e-5
RMS_EPS = 1e-6
NEG = -1e30
F_MIN = 1e-30

LP = SEQ + N_META + PAD
NBLK = LP // BLOCK
M_ROWS = BATCH * LP
TM = LP // 4
LANES = 128
MXU_N = 256
TQ_B = LP // 8
VMEM_LIMIT = 48 * 1024 * 1024

N_ROPE_Q = A_WIDTH + 2 * B_HEADS * HEAD_DIM
N_ROPE_K = A_KV_HEADS * HEAD_DIM + 2 * B_HEADS * HEAD_DIM
N_PLAIN = C_WIDTH + A_KV_HEADS * HEAD_DIM + B_WIDTH
N_SILU = A_WIDTH + B_WIDTH + C_WIDTH + C_WIDTH
N_GATE = 2 * C_WIDTH
A_HEAD_ORDER = (0, 2, 1, 3)

HGRN_LEVELS = (0, 1, 2, 4, 8, 16, 32, 64)


def _cparams(sem):
    return pltpu.CompilerParams(dimension_semantics=sem, vmem_limit_bytes=VMEM_LIMIT)


def _layer_norm(x, g, b):
    mu = jnp.mean(x, axis=-1, keepdims=True)
    xc = x - mu
    var = jnp.mean(xc * xc, axis=-1, keepdims=True)
    return xc * lax.rsqrt(var + LN_EPS) * g + b


def _embed_kernel(x_ref, meta_ref, g_ref, b_ref, h32_ref, h16_ref):
    j = pl.program_id(1)

    @pl.when(j == 0)
    def _():
        m = _layer_norm(meta_ref[...], g_ref[...], b_ref[...])
        h32_ref[0, 0:PAD, :] = jnp.zeros((PAD, D_MODEL), jnp.float32)
        h32_ref[0, PAD:BLOCK, :] = m
        h16_ref[0, 0:PAD, :] = jnp.zeros((PAD, D_MODEL), jnp.bfloat16)
        h16_ref[0, PAD:BLOCK, :] = m.astype(jnp.bfloat16)

    @pl.when(j > 0)
    def _():
        y = _layer_norm(x_ref[0], g_ref[...], b_ref[...])
        h32_ref[0] = y
        h16_ref[0] = y.astype(jnp.bfloat16)


def _embed(x, meta, g, b):
    return pl.pallas_call(
        _embed_kernel,
        grid=(BATCH, NBLK),
        in_specs=[
            pl.BlockSpec((1, BLOCK, D_MODEL), lambda bi, j: (bi, jnp.maximum(j - 1, 0), 0)),
            pl.BlockSpec((N_META, D_MODEL), lambda bi, j: (0, 0)),
            pl.BlockSpec((1, D_MODEL), lambda bi, j: (0, 0)),
            pl.BlockSpec((1, D_MODEL), lambda bi, j: (0, 0)),
        ],
        out_specs=[
            pl.BlockSpec((1, BLOCK, D_MODEL), lambda bi, j: (bi, j, 0)),
            pl.BlockSpec((1, BLOCK, D_MODEL), lambda bi, j: (bi, j, 0)),
        ],
        out_shape=[
            jax.ShapeDtypeStruct((BATCH, LP, D_MODEL), jnp.float32),
            jax.ShapeDtypeStruct((BATCH, LP, D_MODEL), jnp.bfloat16),
        ],
        compiler_params=_cparams(("arbitrary", "arbitrary")),
        name="embed_ln",
    )(x, meta, g.reshape(1, D_MODEL), b.reshape(1, D_MODEL))


def _col_chunks(n):
    return [(c, min(MXU_N, n - c)) for c in range(0, n, MXU_N)]


def _rope_cols(acc, cos, sin_signed, first_half):
    partner = jnp.where(first_half, pltpu.roll(acc, LANES - HEAD_DIM // 2, 1), pltpu.roll(acc, HEAD_DIM // 2, 1))
    return acc * cos + partner * sin_signed


def _inproj_rope_kernel(n_cols, x_ref, w_ref, cos_ref, sin_ref, o_ref):
    x = x_ref[...]
    cos = cos_ref[...]
    sin = sin_ref[...]
    lane = lax.broadcasted_iota(jnp.int32, (TM, LANES), 1)
    first_half = (lane % HEAD_DIM) < (HEAD_DIM // 2)
    for c0, cw in _col_chunks(n_cols):
        acc = jnp.dot(x, w_ref[:, c0:c0 + cw], preferred_element_type=jnp.float32)
        for k in range(cw // LANES):
            t = _rope_cols(acc[:, k * LANES:(k + 1) * LANES], cos, sin, first_half)
            o_ref[:, c0 + k * LANES:c0 + (k + 1) * LANES] = t.astype(o_ref.dtype)


def _inproj_plain_kernel(n_cols, x_ref, w_ref, o_ref):
    x = x_ref[...]
    for c0, cw in _col_chunks(n_cols):
        acc = jnp.dot(x, w_ref[:, c0:c0 + cw], preferred_element_type=jnp.float32)
        o_ref[:, c0:c0 + cw] = acc.astype(o_ref.dtype)


def _inproj_silu_kernel(n_cols, x_ref, w_ref, o_ref):
    x = x_ref[...]
    for c0, cw in _col_chunks(n_cols):
        acc = jnp.dot(x, w_ref[:, c0:c0 + cw], preferred_element_type=jnp.float32)
        o_ref[:, c0:c0 + cw] = (acc * jax.nn.sigmoid(acc)).astype(o_ref.dtype)


def _inproj_gate_kernel(x_ref, w_ref, lb_ref, lf_ref, kk_ref):
    x = x_ref[...]
    lb = lb_ref[...]
    row = (pl.program_id(0) % (LP // TM)) * TM + lax.broadcasted_iota(jnp.int32, (TM, C_WIDTH), 0)
    valid = row >= PAD
    for d in range(2):
        z = jnp.dot(x, w_ref[:, d * C_WIDTH:(d + 1) * C_WIDTH], preferred_element_type=jnp.float32)
        f = lb + (1.0 - lb) * jax.nn.sigmoid(z)
        logf = jnp.log(jnp.maximum(f, F_MIN))
        kk = (1.0 - lb) * jax.nn.sigmoid(-z)
        lf_ref[:, d * C_WIDTH:(d + 1) * C_WIDTH] = jnp.where(valid, logf, 0.0)
        kk_ref[:, d * C_WIDTH:(d + 1) * C_WIDTH] = jnp.where(valid, kk, 0.0)


def _inproj(kind, h16, w, extra=()):
    n = w.shape[1]
    row_spec = pl.BlockSpec((TM, D_MODEL), lambda i: (i, 0))
    w_spec = pl.BlockSpec((D_MODEL, n), lambda i: (0, 0))
    out_spec = pl.BlockSpec((TM, n), lambda i: (i, 0))
    if kind == "rope":
        body = functools.partial(_inproj_rope_kernel, n)
        tab_spec = pl.BlockSpec((TM, LANES), lambda i: (i % (LP // TM), 0))
        in_specs = [row_spec, w_spec, tab_spec, tab_spec]
        out_specs, out_shape = out_spec, jax.ShapeDtypeStruct((M_ROWS, n), jnp.bfloat16)
    elif kind == "plain":
        body = functools.partial(_inproj_plain_kernel, n)
        in_specs = [row_spec, w_spec]
        out_specs, out_shape = out_spec, jax.ShapeDtypeStruct((M_ROWS, n), jnp.bfloat16)
    elif kind == "silu":
        body = functools.partial(_inproj_silu_kernel, n)
        in_specs = [row_spec, w_spec]
        out_specs, out_shape = out_spec, jax.ShapeDtypeStruct((M_ROWS, n), jnp.bfloat16)
    else:
        assert kind == "gate" and n == N_GATE
        body = _inproj_gate_kernel
        in_specs = [row_spec, w_spec, pl.BlockSpec((1, C_WIDTH), lambda i: (0, 0))]
        out_specs = [out_spec, out_spec]
        out_shape = [jax.ShapeDtypeStruct((M_ROWS, n), jnp.float32)] * 2
    return pl.pallas_call(
        body,
        grid=(M_ROWS // TM,),
        in_specs=in_specs,
        out_specs=out_specs,
        out_shape=out_shape,
        compiler_params=_cparams(("arbitrary",)),
        name="inproj_" + kind,
    )(h16, w, *extra)


def _mixer_a_kernel(sink_ref, q_ref, k_ref, v_ref, g_ref, o_ref):
    n = pl.program_id(1)
    start = pl.multiple_of(jnp.clip((n - 1) * BLOCK, 0, LP - 3 * BLOCK), BLOCK)
    kcat = jnp.concatenate([k_ref[0, 0:BLOCK, :], k_ref[0, pl.ds(start, 3 * BLOCK), :]], axis=0)
    vcat = jnp.concatenate([v_ref[0, 0:BLOCK, :], v_ref[0, pl.ds(start, 3 * BLOCK), :]], axis=0)
    col = lax.broadcasted_iota(jnp.int32, (BLOCK, 4 * BLOCK), 1)
    qpos = n * BLOCK + lax.broadcasted_iota(jnp.int32, (BLOCK, 4 * BLOCK), 0)
    kpos = start + col - BLOCK
    ok = ((col < BLOCK) & (col >= PAD)) | ((col >= BLOCK) & (jnp.abs(qpos - kpos) <= WINDOW) & (kpos >= BLOCK))
    lane_hi = lax.broadcasted_iota(jnp.int32, (BLOCK, LANES), 1) >= HEAD_DIM
    for cv in range(A_HEADS // 2):
        qcol = q_ref[0, :, cv * LANES:(cv + 1) * LANES]
        outs = []
        for hf in range(2):
            sink = sink_ref[cv * 2 + hf]
            qm = jnp.where(lane_hi if hf else jnp.logical_not(lane_hi), qcol, jnp.zeros_like(qcol))
            s = lax.dot_general(qm, kcat, (((1,), (1,)), ((), ())), preferred_element_type=jnp.float32)
            s = jnp.where(ok, s, NEG)
            m = jnp.maximum(jnp.max(s, axis=-1, keepdims=True), sink)
            e = jnp.exp(s - m)
            den = jnp.sum(e, axis=-1, keepdims=True) + jnp.exp(sink - m)
            p = (e * (1.0 / den)).astype(jnp.bfloat16)
            outs.append(jnp.dot(p, vcat, preferred_element_type=jnp.float32))
        y = jnp.where(lane_hi, outs[1], outs[0]) * g_ref[0, :, cv * LANES:(cv + 1) * LANES].astype(jnp.float32)
        o_ref[0, :, cv * LANES:(cv + 1) * LANES] = y.astype(o_ref.dtype)


def _mixer_a(sink, q3, k3, v3, g3):
    return pl.pallas_call(
        _mixer_a_kernel,
        grid=(BATCH, NBLK),
        in_specs=[
            pl.BlockSpec(memory_space=pltpu.SMEM),
            pl.BlockSpec((1, BLOCK, A_WIDTH), lambda b, n: (b, n, 0)),
            pl.BlockSpec((1, LP, LANES), lambda b, n: (b, 0, 0)),
            pl.BlockSpec((1, LP, LANES), lambda b, n: (b, 0, C_WIDTH // LANES)),
            pl.BlockSpec((1, BLOCK, A_WIDTH), lambda b, n: (b, n, 0)),
        ],
        out_specs=pl.BlockSpec((1, BLOCK, A_WIDTH), lambda b, n: (b, n, 0)),
        out_shape=jax.ShapeDtypeStruct((BATCH, LP, A_WIDTH), jnp.bfloat16),
        compiler_params=_cparams(("arbitrary", "arbitrary")),
        name="mixer_a_window_gqa",
    )(sink, q3, k3, v3, g3)


def _mixer_b_kernel(lam_init, lam_ref, q_ref, k_ref, v_ref, g_ref, w_ref, o_ref):
    lam = lam_ref[0]
    lane_hi = lax.broadcasted_iota(jnp.int32, (TQ_B, LANES), 1) >= HEAD_DIM
    key_ok = lax.broadcasted_iota(jnp.int32, (TQ_B, BLOCK), 1) >= PAD
    w = w_ref[...] * (1.0 - lam_init)

    def body(qi, carry):
        r0 = pl.multiple_of(qi * TQ_B, 16)
        q = q_ref[0, pl.ds(r0, TQ_B), :]
        parts = []
        for c in range(2):
            qm = jnp.where(lane_hi if c else jnp.logical_not(lane_hi), q, jnp.zeros_like(q))
            s0 = lax.dot_general(qm, k_ref[0, 0:BLOCK, :], (((1,), (1,)), ((), ())),
                                 preferred_element_type=jnp.float32)
            s0 = jnp.where(key_ok, s0, NEG)
            s1 = lax.dot_general(qm, k_ref[0, BLOCK:LP, :], (((1,), (1,)), ((), ())),
                                 preferred_element_type=jnp.float32)
            m = jnp.maximum(jnp.max(s0, axis=-1, keepdims=True), jnp.max(s1, axis=-1, keepdims=True))
            e0 = jnp.exp(s0 - m)
            e1 = jnp.exp(s1 - m)
            den = jnp.sum(e0, axis=-1, keepdims=True) + jnp.sum(e1, axis=-1, keepdims=True)
            parts.append((e0, e1, 1.0 / den))
        (a0, a1, ra), (b0, b1, rb) = parts
        rb = rb * lam
        d0 = (a0 * ra - b0 * rb).astype(jnp.bfloat16)
        d1 = (a1 * ra - b1 * rb).astype(jnp.bfloat16)
        o = (jnp.dot(d0, v_ref[0, 0:BLOCK, :], preferred_element_type=jnp.float32)
             + jnp.dot(d1, v_ref[0, BLOCK:LP, :], preferred_element_type=jnp.float32))
        ms = jnp.mean(o * o, axis=-1, keepdims=True)
        y = o * lax.rsqrt(ms + RMS_EPS) * w
        y = y * g_ref[0, pl.ds(r0, TQ_B), :].astype(jnp.float32)
        o_ref[0, pl.ds(r0, TQ_B), :] = y.astype(o_ref.dtype)
        return carry

    lax.fori_loop(0, LP // TQ_B, body, 0)


def _mixer_b(lam_init, lam, q3, k3, v3, g3, subln_w):
    qoff = A_WIDTH // LANES
    koff = (A_KV_HEADS * HEAD_DIM) // LANES
    voff = (C_WIDTH + A_KV_HEADS * HEAD_DIM) // LANES
    goff = A_WIDTH // LANES
    return pl.pallas_call(
        functools.partial(_mixer_b_kernel, lam_init),
        grid=(BATCH, B_HEADS),
        in_specs=[
            pl.BlockSpec(memory_space=pltpu.SMEM),
            pl.BlockSpec((1, LP, LANES), lambda b, h: (b, 0, qoff + h)),
            pl.BlockSpec((1, LP, LANES), lambda b, h: (b, 0, koff + h)),
            pl.BlockSpec((1, LP, LANES), lambda b, h: (b, 0, voff + h)),
            pl.BlockSpec((1, LP, LANES), lambda b, h: (b, 0, goff + h)),
            pl.BlockSpec((1, B_V_DIM), lambda b, h: (0, h)),
        ],
        out_specs=pl.BlockSpec((1, LP, LANES), lambda b, h: (b, 0, h)),
        out_shape=jax.ShapeDtypeStruct((BATCH, LP, B_WIDTH), jnp.bfloat16),
        compiler_params=_cparams(("arbitrary", "arbitrary")),
        name="mixer_b_diff_attn",
    )(lam, q3, k3, v3, g3, subln_w.reshape(1, B_WIDTH))


def _scan_rows(x, reverse):
    row = lax.broadcasted_iota(jnp.int32, x.shape, 0)
    s = 1
    while s < BLOCK:
        if reverse:
            x = x + jnp.where(row + s < BLOCK, pltpu.roll(x, BLOCK - s, 0), 0.0)
        else:
            x = x + jnp.where(row >= s, pltpu.roll(x, s, 0), 0.0)
        s *= 2
    return x


def _hgrn_anchor(b, rows, m, reverse):
    a_off = m if reverse else m - 1
    if 2 * m >= 16:
        pieces = [jnp.broadcast_to(b[jb * 2 * m + a_off: jb * 2 * m + a_off + 1, :], (2 * m, C_WIDTH))
                  for jb in range(BLOCK // (2 * m))]
        return pieces[0] if len(pieces) == 1 else jnp.concatenate(pieces, axis=0)
    dist = a_off - rows % (2 * m)
    anchor = jnp.zeros_like(b)
    for d in range(-(2 * m - 1 - a_off), a_off + 1):
        shifted = b if d == 0 else pltpu.roll(b, (-d) % BLOCK, 0)
        anchor = jnp.where(dist == d, shifted, anchor)
    return anchor


def _hgrn_chunk(q, v16, kk, lf, state_t, reverse, consts):
    head_of_lane, same_head, row_i, col_i = consts
    b = _scan_rows(lf, reverse)
    rows = lax.broadcasted_iota(jnp.int32, (BLOCK, C_WIDTH), 0)
    p_acc = [jnp.zeros((BLOCK, BLOCK), jnp.float32) for _ in range(C_HEADS)]
    for m in HGRN_LEVELS:
        if m == 0:
            qt, kt = q, kk
            pair_ok = row_i == col_i
        else:
            in_right = (rows % (2 * m)) >= m
            q_side = jnp.logical_not(in_right) if reverse else in_right
            anchor = _hgrn_anchor(b, rows, m, reverse)
            qt = jnp.where(q_side, q * jnp.exp(jnp.minimum(b - anchor, 0.0)), 0.0)
            kt = jnp.where(q_side, 0.0, kk * jnp.exp(jnp.minimum(anchor - b, 0.0)))
            pair_ok = (row_i // (2 * m)) == (col_i // (2 * m))
        kt16 = kt.astype(jnp.bfloat16)
        for h in range(C_HEADS):
            qh = jnp.where(head_of_lane == h, qt, 0.0).astype(jnp.bfloat16)
            ph = lax.dot_general(qh, kt16, (((1,), (1,)), ((), ())), preferred_element_type=jnp.float32)
            p_acc[h] = p_acc[h] + jnp.where(pair_ok, ph, 0.0)
    o = jnp.zeros((BLOCK, C_WIDTH), jnp.float32)
    for h in range(C_HEADS):
        vh = jnp.where(head_of_lane == h, v16, jnp.zeros_like(v16))
        o = o + jnp.dot(p_acc[h].astype(jnp.bfloat16), vh, preferred_element_type=jnp.float32)
    edge = 0 if reverse else BLOCK - 1
    b_all = b[edge:edge + 1, :]
    q_in = (q * jnp.exp(b)).astype(jnp.bfloat16)
    o = o + lax.dot_general(q_in, state_t.astype(jnp.bfloat16), (((1,), (1,)), ((), ())),
                            preferred_element_type=jnp.float32)
    k_out = (kk * jnp.exp(b_all - b)).astype(jnp.bfloat16)
    u_t = lax.dot_general(v16, k_out, (((0,), (0,)), ((), ())), preferred_element_type=jnp.float32)
    new_state = jnp.exp(b_all) * state_t + jnp.where(same_head, u_t, 0.0)
    return o, new_state


def _mixer_c_kernel(q_ref, v_ref, lff_ref, lfb_ref, kkf_ref, kkb_ref, g_ref, w_ref, o_ref, acc_ref):
    head_of_lane = lax.broadcasted_iota(jnp.int32, (BLOCK, C_WIDTH), 1) // HEAD_DIM
    sr = lax.broadcasted_iota(jnp.int32, (C_WIDTH, C_WIDTH), 0) // HEAD_DIM
    sc = lax.broadcasted_iota(jnp.int32, (C_WIDTH, C_WIDTH), 1) // HEAD_DIM
    same_head = sr == sc
    row_i = lax.broadcasted_iota(jnp.int32, (BLOCK, BLOCK), 0)
    col_i = lax.broadcasted_iota(jnp.int32, (BLOCK, BLOCK), 1)
    consts = (head_of_lane, same_head, row_i, col_i)
    w = w_ref[...]

    def run(reverse, lf_ref, kk_ref, finish):
        def body(i, state_t):
            n = (NBLK - 1 - i) if reverse else i
            r0 = pl.multiple_of(n * BLOCK, BLOCK)
            q = q_ref[0, pl.ds(r0, BLOCK), :].astype(jnp.float32)
            v16 = v_ref[0, pl.ds(r0, BLOCK), :]
            o, new_state = _hgrn_chunk(q, v16, kk_ref[0, pl.ds(r0, BLOCK), :], lf_ref[0, pl.ds(r0, BLOCK), :],
                                       state_t, reverse, consts)
            finish(r0, o)
            return new_state

        lax.fori_loop(0, NBLK, body, jnp.zeros((C_WIDTH, C_WIDTH), jnp.float32))

    def store_fwd(r0, o):
        acc_ref[pl.ds(r0, BLOCK), :] = o

    def store_out(r0, o):
        y = acc_ref[pl.ds(r0, BLOCK), :] + o
        ms = _head_mean_sq(y)
        y = y * lax.rsqrt(ms + RMS_EPS) * w
        y = y * g_ref[0, pl.ds(r0, BLOCK), :].astype(jnp.float32)
        o_ref[0, pl.ds(r0, BLOCK), :] = y.astype(o_ref.dtype)

    def _head_mean_sq(y):
        sq = y * y
        parts = []
        for h in range(C_HEADS):
            s = jnp.sum(sq[:, h * HEAD_DIM:(h + 1) * HEAD_DIM], axis=-1, keepdims=True) * (1.0 / HEAD_DIM)
            parts.append(jnp.broadcast_to(s, (BLOCK, HEAD_DIM)))
        return jnp.concatenate(parts, axis=-1)

    run(False, lff_ref, kkf_ref, store_fwd)
    run(True, lfb_ref, kkb_ref, store_out)


def _mixer_c(q3, v3, lf3, kk3, g3, norm_w):
    blk = lambda j: pl.BlockSpec((1, LP, C_WIDTH), lambda b: (b, 0, j))
    return pl.pallas_call(
        _mixer_c_kernel,
        grid=(BATCH,),
        in_specs=[
            blk(4),
            blk(0),
            blk(0), blk(1),
            blk(0), blk(1),
            blk(3),
            pl.BlockSpec((1, C_WIDTH), lambda b: (0, 0)),
        ],
        out_specs=pl.BlockSpec((1, LP, C_WIDTH), lambda b: (b, 0, 0)),
        out_shape=jax.ShapeDtypeStruct((BATCH, LP, C_WIDTH), jnp.bfloat16),
        scratch_shapes=[pltpu.VMEM((LP, C_WIDTH), jnp.float32)],
        compiler_params=_cparams(("arbitrary",)),
        name="mixer_c_hgrn2",
    )(q3, v3, lf3, lf3, kk3, kk3, g3, norm_w.reshape(1, C_WIDTH))


def _outproj_kernel(ya_ref, yb_ref, yc_ref, wa_ref, wb_ref, wc_ref, h_ref, g_ref, b_ref, h32_ref, h16_ref):
    acc = jnp.dot(ya_ref[...], wa_ref[...], preferred_element_type=jnp.float32)
    acc = acc + jnp.dot(yb_ref[...], wb_ref[...], preferred_element_type=jnp.float32)
    acc = acc + jnp.dot(yc_ref[...], wc_ref[...], preferred_element_type=jnp.float32)
    y = _layer_norm(ALPHA * h_ref[...] + acc, g_ref[...], b_ref[...])
    h32_ref[...] = y
    h16_ref[...] = y.astype(jnp.bfloat16)


def _outproj(ya, yb, yc, wa, wb, wc, h32, g, b):
    row = lambda n: pl.BlockSpec((TM, n), lambda i: (i, 0))
    full = lambda n: pl.BlockSpec((n, D_MODEL), lambda i: (0, 0))
    vec = pl.BlockSpec((1, D_MODEL), lambda i: (0, 0))
    return pl.pallas_call(
        _outproj_kernel,
        grid=(M_ROWS // TM,),
        in_specs=[row(A_WIDTH), row(B_WIDTH), row(C_WIDTH), full(A_WIDTH), full(B_WIDTH), full(C_WIDTH),
                  row(D_MODEL), vec, vec],
        out_specs=[row(D_MODEL), row(D_MODEL)],
        out_shape=[jax.ShapeDtypeStruct((M_ROWS, D_MODEL), jnp.float32),
                   jax.ShapeDtypeStruct((M_ROWS, D_MODEL), jnp.bfloat16)],
        compiler_params=_cparams(("arbitrary",)),
        name="outproj_residual_ln",
    )(ya, yb, yc, wa, wb, wc, h32, g.reshape(1, D_MODEL), b.reshape(1, D_MODEL))


def _regroup_w_in(w):
    def heads(lo, order):
        return [w[:, lo + h * HEAD_DIM: lo + (h + 1) * HEAD_DIM] for h in order]

    c = lambda lo, n: w[:, lo:lo + n]
    o_qa, o_ka, o_va, o_ga = 0, 256, 384, 512
    o_qb, o_kb, o_vb, o_gb = 768, 1280, 1792, 2304
    o_qc, o_ic, o_zf, o_zb, o_gc = 2816, 3072, 3328, 3584, 3840
    cat = lambda xs: jnp.concatenate(xs, axis=1).astype(jnp.bfloat16)
    w_q = cat(heads(o_qa, A_HEAD_ORDER) + [c(o_qb, 512)])
    w_k = cat([c(o_ka, 128), c(o_kb, 512)])
    w_p = cat([c(o_ic, 256), c(o_va, 128), c(o_vb, 512)])
    w_s = cat(heads(o_ga, A_HEAD_ORDER) + [c(o_gb, 512), c(o_gc, 256), c(o_qc, 256)])
    w_g = cat([c(o_zf, 256), c(o_zb, 256)])
    return w_q, w_k, w_p, w_s, w_g


def _rope_tables():
    pos = (jnp.arange(LP) - PAD).astype(jnp.float32)
    inv = ROPE_THETA ** (-jnp.arange(0, HEAD_DIM, 2, dtype=jnp.float32) / HEAD_DIM)
    ang = pos[:, None] * inv[None, :]
    cos, sin = jnp.cos(ang), jnp.sin(ang)
    cos_t = jnp.concatenate([cos, cos, cos, cos], axis=1)
    sin_t = jnp.concatenate([-sin, sin, -sin, sin], axis=1)
    return cos_t, sin_t


def kernel(x, meta, emb_ln_g, emb_ln_b, w_in, w_out, a_sink, b_lam, b_subln_w, c_lb_logits, c_norm_w, ln_g, ln_b):
    cos_t, sin_t = _rope_tables()
    scale = HEAD_DIM ** -0.5
    p_lb = jax.nn.softmax(c_lb_logits.astype(jnp.float32), axis=0)
    lb_all = jnp.cumsum(p_lb, axis=0) - p_lb[0:1]

    h32, h16 = _embed(x, meta, emb_ln_g, emb_ln_b)
    h32 = h32.reshape(M_ROWS, D_MODEL)
    h16 = h16.reshape(M_ROWS, D_MODEL)
    for l in range(DEPTH):
        lam_init = 0.8 - 0.6 * math.exp(-0.3 * l)
        lp32 = b_lam[l].astype(jnp.float32)
        lam = jnp.exp(jnp.sum(lp32[0] * lp32[1])) - jnp.exp(jnp.sum(lp32[2] * lp32[3])) + lam_init
        w_q, w_k, w_p, w_s, w_g = _regroup_w_in(w_in[l])
        wo = w_out[l]
        wa = jnp.concatenate([wo[h * HEAD_DIM:(h + 1) * HEAD_DIM] for h in A_HEAD_ORDER], axis=0).astype(jnp.bfloat16)
        wb = wo[A_WIDTH:A_WIDTH + B_WIDTH].astype(jnp.bfloat16)
        wc = wo[A_WIDTH + B_WIDTH:].astype(jnp.bfloat16)
        sink = jnp.stack([a_sink[l, h] for h in A_HEAD_ORDER]).astype(jnp.float32)

        q_all = _inproj("rope", h16, w_q, (cos_t * scale, sin_t * scale)).reshape(BATCH, LP, N_ROPE_Q)
        k_all = _inproj("rope", h16, w_k, (cos_t, sin_t)).reshape(BATCH, LP, N_ROPE_K)
        p_all = _inproj("plain", h16, w_p).reshape(BATCH, LP, N_PLAIN)
        s_all = _inproj("silu", h16, w_s).reshape(BATCH, LP, N_SILU)
        lf_all, kk_all = _inproj("gate", h16, w_g, (lb_all[l].reshape(1, C_WIDTH),))
        lf_all = lf_all.reshape(BATCH, LP, N_GATE)
        kk_all = kk_all.reshape(BATCH, LP, N_GATE)

        ya = _mixer_a(sink, q_all, k_all, p_all, s_all)
        yb = _mixer_b(lam_init, lam.reshape(1), q_all, k_all, p_all, s_all, b_subln_w[l])
        yc = _mixer_c(s_all, p_all, lf_all, kk_all, s_all, c_norm_w[l])

        h32, h16 = _outproj(ya.reshape(M_ROWS, A_WIDTH), yb.reshape(M_ROWS, B_WIDTH), yc.reshape(M_ROWS, C_WIDTH),
                            wa, wb, wc, h32, ln_g[l], ln_b[l])
    return h32.reshape(BATCH, LP, D_MODEL)[:, PAD + N_META:]
```

```python
import functools
import math

import jax
import jax.numpy as jnp
import numpy as np
from jax import lax
from jax.experimental import pallas as pl
from jax.experimental.pallas import tpu as pltpu

D_MODEL = 1024
BATCH = 8
SEQ = 2048
DEPTH = 2
N_META = 16
BLOCK = 128
PAD = BLOCK - N_META
WINDOW = 128
ROPE_THETA = 10000.0
HEAD_DIM = 64
A_HEADS = 4
A_KV_HEADS = 2
A_WIDTH = A_HEADS * HEAD_DIM
B_HEADS = 4
B_V_DIM = 2 * HEAD_DIM
B_WIDTH = B_HEADS * B_V_DIM
C_HEADS = 4
C_WIDTH = C_HEADS * HEAD_DIM
D_MIX = A_WIDTH + B_WIDTH + C_WIDTH
ALPHA = (2 * DEPTH) ** 0.25
LN_EPS = 1e-5
RMS_EPS = 1e-6
NEG = -1e30
F_MIN = 1e-30

LP = SEQ + N_META + PAD
NBLK = LP // BLOCK
M_ROWS = BATCH * LP
TM = LP // 4
LANES = 128
MXU_N = 256
TQ_B = LP // 8
SM_ROWS = 16
VMEM_LIMIT = 48 * 1024 * 1024

N_ROPE_Q = A_WIDTH + 2 * B_HEADS * HEAD_DIM
N_ROPE_K = A_KV_HEADS * HEAD_DIM + 2 * B_HEADS * HEAD_DIM
N_PLAIN = C_WIDTH + A_KV_HEADS * HEAD_DIM + B_WIDTH
N_SILU = A_WIDTH + B_WIDTH + C_WIDTH + C_WIDTH
N_GATE = 2 * C_WIDTH
A_HEAD_ORDER = (0, 2, 1, 3)

HGRN_LEVELS = (0, 1, 2, 4, 8, 16, 32, 64)


def _cparams(sem):
    return pltpu.CompilerParams(dimension_semantics=sem, vmem_limit_bytes=VMEM_LIMIT)


def _layer_norm(x, g, b):
    mu = jnp.mean(x, axis=-1, keepdims=True)
    xc = x - mu
    var = jnp.mean(xc * xc, axis=-1, keepdims=True)
    return xc * lax.rsqrt(var + LN_EPS) * g + b


def _embed_kernel(x_ref, meta_ref, g_ref, b_ref, h32_ref, h16_ref):
    j = pl.program_id(1)

    @pl.when(j == 0)
    def _():
        m = _layer_norm(meta_ref[...], g_ref[...], b_ref[...])
        h32_ref[0, 0:PAD, :] = jnp.zeros((PAD, D_MODEL), jnp.float32)
        h32_ref[0, PAD:BLOCK, :] = m
        h16_ref[0, 0:PAD, :] = jnp.zeros((PAD, D_MODEL), jnp.bfloat16)
        h16_ref[0, PAD:BLOCK, :] = m.astype(jnp.bfloat16)

    @pl.when(j > 0)
    def _():
        y = _layer_norm(x_ref[0], g_ref[...], b_ref[...])
        h32_ref[0] = y
        h16_ref[0] = y.astype(jnp.bfloat16)


def _embed(x, meta, g, b):
    return pl.pallas_call(
        _embed_kernel,
        grid=(BATCH, NBLK),
        in_specs=[
            pl.BlockSpec((1, BLOCK, D_MODEL), lambda bi, j: (bi, jnp.maximum(j - 1, 0), 0)),
            pl.BlockSpec((N_META, D_MODEL), lambda bi, j: (0, 0)),
            pl.BlockSpec((1, D_MODEL), lambda bi, j: (0, 0)),
            pl.BlockSpec((1, D_MODEL), lambda bi, j: (0, 0)),
        ],
        out_specs=[
            pl.BlockSpec((1, BLOCK, D_MODEL), lambda bi, j: (bi, j, 0)),
            pl.BlockSpec((1, BLOCK, D_MODEL), lambda bi, j: (bi, j, 0)),
        ],
        out_shape=[
            jax.ShapeDtypeStruct((BATCH, LP, D_MODEL), jnp.float32),
            jax.ShapeDtypeStruct((BATCH, LP, D_MODEL), jnp.bfloat16),
        ],
        compiler_params=_cparams(("arbitrary", "arbitrary")),
        name="embed_ln",
    )(x, meta, g.reshape(1, D_MODEL), b.reshape(1, D_MODEL))


def _col_chunks(n):
    return [(c, min(MXU_N, n - c)) for c in range(0, n, MXU_N)]


def _rope_cols(acc, cos, sin_signed, first_half):
    partner = jnp.where(first_half, pltpu.roll(acc, LANES - HEAD_DIM // 2, 1), pltpu.roll(acc, HEAD_DIM // 2, 1))
    return acc * cos + partner * sin_signed


def _inproj_rope_kernel(n_cols, x_ref, w_ref, cos_ref, sin_ref, o_ref):
    x = x_ref[...]
    cos = cos_ref[...]
    sin = sin_ref[...]
    lane = lax.broadcasted_iota(jnp.int32, (TM, LANES), 1)
    first_half = (lane % HEAD_DIM) < (HEAD_DIM // 2)
    for c0, cw in _col_chunks(n_cols):
        acc = jnp.dot(x, w_ref[:, c0:c0 + cw], preferred_element_type=jnp.float32)
        for k in range(cw // LANES):
            t = _rope_cols(acc[:, k * LANES:(k + 1) * LANES], cos, sin, first_half)
            o_ref[:, c0 + k * LANES:c0 + (k + 1) * LANES] = t.astype(o_ref.dtype)


def _inproj_plain_kernel(n_cols, x_ref, w_ref, o_ref):
    x = x_ref[...]
    for c0, cw in _col_chunks(n_cols):
        acc = jnp.dot(x, w_ref[:, c0:c0 + cw], preferred_element_type=jnp.float32)
        o_ref[:, c0:c0 + cw] = acc.astype(o_ref.dtype)


def _inproj_silu_kernel(n_cols, x_ref, w_ref, o_ref):
    x = x_ref[...]
    for c0, cw in _col_chunks(n_cols):
        acc = jnp.dot(x, w_ref[:, c0:c0 + cw], preferred_element_type=jnp.float32)
        o_ref[:, c0:c0 + cw] = (acc * jax.nn.sigmoid(acc)).astype(o_ref.dtype)


def _inproj_gate_kernel(x_ref, w_ref, lb_ref, lf_ref, kk_ref):
    x = x_ref[...]
    lb = lb_ref[...]
    row = (pl.program_id(0) % (LP // TM)) * TM + lax.broadcasted_iota(jnp.int32, (TM, C_WIDTH), 0)
    valid = row >= PAD
    for d in range(2):
        z = jnp.dot(x, w_ref[:, d * C_WIDTH:(d + 1) * C_WIDTH], preferred_element_type=jnp.float32)
        f = lb + (1.0 - lb) * jax.nn.sigmoid(z)
        logf = jnp.log(jnp.maximum(f, F_MIN))
        kk = (1.0 - lb) * jax.nn.sigmoid(-z)
        lf_ref[:, d * C_WIDTH:(d + 1) * C_WIDTH] = jnp.where(valid, logf, 0.0)
        kk_ref[:, d * C_WIDTH:(d + 1) * C_WIDTH] = jnp.where(valid, kk, 0.0)


def _inproj(kind, h16, w, extra=()):
    n = w.shape[1]
    row_spec = pl.BlockSpec((TM, D_MODEL), lambda i: (i, 0))
    w_spec = pl.BlockSpec((D_MODEL, n), lambda i: (0, 0))
    out_spec = pl.BlockSpec((TM, n), lambda i: (i, 0))
    if kind == "rope":
        body = functools.partial(_inproj_rope_kernel, n)
        tab_spec = pl.BlockSpec((TM, LANES), lambda i: (i % (LP // TM), 0))
        in_specs = [row_spec, w_spec, tab_spec, tab_spec]
        out_specs, out_shape = out_spec, jax.ShapeDtypeStruct((M_ROWS, n), jnp.bfloat16)
    elif kind == "plain":
        body = functools.partial(_inproj_plain_kernel, n)
        in_specs = [row_spec, w_spec]
        out_specs, out_shape = out_spec, jax.ShapeDtypeStruct((M_ROWS, n), jnp.bfloat16)
    elif kind == "silu":
        body = functools.partial(_inproj_silu_kernel, n)
        in_specs = [row_spec, w_spec]
        out_specs, out_shape = out_spec, jax.ShapeDtypeStruct((M_ROWS, n), jnp.bfloat16)
    else:
        assert kind == "gate" and n == N_GATE
        body = _inproj_gate_kernel
        in_specs = [row_spec, w_spec, pl.BlockSpec((1, C_WIDTH), lambda i: (0, 0))]
        out_specs = [out_spec, out_spec]
        out_shape = [jax.ShapeDtypeStruct((M_ROWS, n), jnp.float32)] * 2
    return pl.pallas_call(
        body,
        grid=(M_ROWS // TM,),
        in_specs=in_specs,
        out_specs=out_specs,
        out_shape=out_shape,
        compiler_params=_cparams(("arbitrary",)),
        name="inproj_" + kind,
    )(h16, w, *extra)


def _mixer_a_kernel(sink_ref, q_ref, k_ref, v_ref, g_ref, o_ref):
    n = pl.program_id(1)
    start = pl.multiple_of(jnp.clip((n - 1) * BLOCK, 0, LP - 3 * BLOCK), BLOCK)
    kcat = jnp.concatenate([k_ref[0, 0:BLOCK, :], k_ref[0, pl.ds(start, 3 * BLOCK), :]], axis=0)
    vcat = jnp.concatenate([v_ref[0, 0:BLOCK, :], v_ref[0, pl.ds(start, 3 * BLOCK), :]], axis=0)
    col = lax.broadcasted_iota(jnp.int32, (BLOCK, 4 * BLOCK), 1)
    qpos = n * BLOCK + lax.broadcasted_iota(jnp.int32, (BLOCK, 4 * BLOCK), 0)
    kpos = start + col - BLOCK
    ok = ((col < BLOCK) & (col >= PAD)) | ((col >= BLOCK) & (jnp.abs(qpos - kpos) <= WINDOW) & (kpos >= BLOCK))
    lane_hi = lax.broadcasted_iota(jnp.int32, (BLOCK, LANES), 1) >= HEAD_DIM
    for cv in range(A_HEADS // 2):
        qcol = q_ref[0, :, cv * LANES:(cv + 1) * LANES]
        outs = []
        for hf in range(2):
            sink = sink_ref[cv * 2 + hf]
            qm = jnp.where(lane_hi if hf else jnp.logical_not(lane_hi), qcol, jnp.zeros_like(qcol))
            s = lax.dot_general(qm, kcat, (((1,), (1,)), ((), ())), preferred_element_type=jnp.float32)
            s = jnp.where(ok, s, NEG)
            m = jnp.maximum(jnp.max(s, axis=-1, keepdims=True), sink)
            e = jnp.exp(s - m)
            den = jnp.sum(e, axis=-1, keepdims=True) + jnp.exp(sink - m)
            p = (e * (1.0 / den)).astype(jnp.bfloat16)
            outs.append(jnp.dot(p, vcat, preferred_element_type=jnp.float32))
        y = jnp.where(lane_hi, outs[1], outs[0]) * g_ref[0, :, cv * LANES:(cv + 1) * LANES].astype(jnp.float32)
        o_ref[0, :, cv * LANES:(cv + 1) * LANES] = y.astype(o_ref.dtype)


def _mixer_a(sink, q3, k3, v3, g3):
    return pl.pallas_call(
        _mixer_a_kernel,
        grid=(BATCH, NBLK),
        in_specs=[
            pl.BlockSpec(memory_space=pltpu.SMEM),
            pl.BlockSpec((1, BLOCK, A_WIDTH), lambda b, n: (b, n, 0)),
            pl.BlockSpec((1, LP, LANES), lambda b, n: (b, 0, 0)),
            pl.BlockSpec((1, LP, LANES), lambda b, n: (b, 0, C_WIDTH // LANES)),
            pl.BlockSpec((1, BLOCK, A_WIDTH), lambda b, n: (b, n, 0)),
        ],
        out_specs=pl.BlockSpec((1, BLOCK, A_WIDTH), lambda b, n: (b, n, 0)),
        out_shape=jax.ShapeDtypeStruct((BATCH, LP, A_WIDTH), jnp.bfloat16),
        compiler_params=_cparams(("arbitrary", "arbitrary")),
        name="mixer_a_window_gqa",
    )(sink, q3, k3, v3, g3)


def _mixer_b_kernel(lam_init, lam_ref, q_ref, k_ref, v_ref, g_ref, w_ref, o_ref, s_a, s_b, a_a, a_b):
    lam = lam_ref[0]
    lane_hi = lax.broadcasted_iota(jnp.int32, (TQ_B, LANES), 1) >= HEAD_DIM
    key_ok = lax.broadcasted_iota(jnp.int32, (TQ_B, BLOCK), 1) >= PAD
    w = w_ref[...] * (1.0 - lam_init)
    s_bufs, a_bufs = (s_a, s_b), (a_a, a_b)
    nt = LP // TQ_B
    nt_dims = (((1,), (1,)), ((), ()))

    def scores(t):
        s_buf = s_bufs[t % 2]
        q = q_ref[0, t * TQ_B:(t + 1) * TQ_B, :]
        for c in range(2):
            qm = jnp.where(lane_hi if c else jnp.logical_not(lane_hi), q, jnp.zeros_like(q))
            s0 = lax.dot_general(qm, k_ref[0, 0:BLOCK, :], nt_dims, preferred_element_type=jnp.float32)
            s_buf[c, :, 0:BLOCK] = jnp.where(key_ok, s0, NEG)
            s_buf[c, :, BLOCK:LP] = lax.dot_general(qm, k_ref[0, BLOCK:LP, :], nt_dims,
                                                    preferred_element_type=jnp.float32)

    def softmax_pv(t):
        s_buf, a_buf = s_bufs[t % 2], a_bufs[t % 2]
        inv_den0 = []
        for r in range(TQ_B // SM_ROWS):
            rows = slice(r * SM_ROWS, (r + 1) * SM_ROWS)
            es, dens = [], []
            for c in range(2):
                s = s_buf[c, rows, :]
                m = jnp.max(s, axis=-1, keepdims=True)
                e = jnp.exp(s - m)
                es.append(e)
                dens.append(jnp.sum(e, axis=-1, keepdims=True))
            a_buf[rows, :] = (es[0] - es[1] * (dens[0] * lam / dens[1])).astype(jnp.bfloat16)
            inv_den0.append(1.0 / dens[0])
        o = jnp.dot(a_buf[...], v_ref[0], preferred_element_type=jnp.float32)
        o = o * jnp.concatenate(inv_den0, axis=0)
        ms = jnp.mean(o * o, axis=-1, keepdims=True)
        y = o * lax.rsqrt(ms + RMS_EPS) * w
        y = y * g_ref[0, t * TQ_B:(t + 1) * TQ_B, :].astype(jnp.float32)
        o_ref[0, t * TQ_B:(t + 1) * TQ_B, :] = y.astype(o_ref.dtype)

    for t in range(nt + 1):
        if t < nt:
            scores(t)
        if t >= 1:
            softmax_pv(t - 1)


def _mixer_b(lam_init, lam, q3, k3, v3, g3, subln_w):
    qoff = A_WIDTH // LANES
    koff = (A_KV_HEADS * HEAD_DIM) // LANES
    voff = (C_WIDTH + A_KV_HEADS * HEAD_DIM) // LANES
    goff = A_WIDTH // LANES
    return pl.pallas_call(
        functools.partial(_mixer_b_kernel, lam_init),
        grid=(BATCH, B_HEADS),
        in_specs=[
            pl.BlockSpec(memory_space=pltpu.SMEM),
            pl.BlockSpec((1, LP, LANES), lambda b, h: (b, 0, qoff + h)),
            pl.BlockSpec((1, LP, LANES), lambda b, h: (b, 0, koff + h)),
            pl.BlockSpec((1, LP, LANES), lambda b, h: (b, 0, voff + h)),
            pl.BlockSpec((1, LP, LANES), lambda b, h: (b, 0, goff + h)),
            pl.BlockSpec((1, B_V_DIM), lambda b, h: (0, h)),
        ],
        out_specs=pl.BlockSpec((1, LP, LANES), lambda b, h: (b, 0, h)),
        out_shape=jax.ShapeDtypeStruct((BATCH, LP, B_WIDTH), jnp.bfloat16),
        scratch_shapes=[pltpu.VMEM((2, TQ_B, LP), jnp.float32), pltpu.VMEM((2, TQ_B, LP), jnp.float32),
                        pltpu.VMEM((TQ_B, LP), jnp.bfloat16), pltpu.VMEM((TQ_B, LP), jnp.bfloat16)],
        compiler_params=_cparams(("arbitrary", "arbitrary")),
        name="mixer_b_diff_attn",
    )(lam, q3, k3, v3, g3, subln_w.reshape(1, B_WIDTH))


def _scan_rows(x, reverse):
    row = lax.broadcasted_iota(jnp.int32, x.shape, 0)
    s = 1
    while s < BLOCK:
        if reverse:
            x = x + jnp.where(row + s < BLOCK, pltpu.roll(x, BLOCK - s, 0), 0.0)
        else:
            x = x + jnp.where(row >= s, pltpu.roll(x, s, 0), 0.0)
        s *= 2
    return x


def _hgrn_anchor(b, rows, m, reverse):
    a_off = m if reverse else m - 1
    if 2 * m >= 16:
        pieces = [jnp.broadcast_to(b[jb * 2 * m + a_off: jb * 2 * m + a_off + 1, :], (2 * m, C_WIDTH))
                  for jb in range(BLOCK // (2 * m))]
        return pieces[0] if len(pieces) == 1 else jnp.concatenate(pieces, axis=0)
    dist = a_off - rows % (2 * m)
    anchor = jnp.zeros_like(b)
    for d in range(-(2 * m - 1 - a_off), a_off + 1):
        shifted = b if d == 0 else pltpu.roll(b, (-d) % BLOCK, 0)
        anchor = jnp.where(dist == d, shifted, anchor)
    return anchor


def _hgrn_chunk(q, v16, kk, lf, state_t, reverse, consts):
    head_of_lane, same_head, row_i, col_i = consts
    b = _scan_rows(lf, reverse)
    rows = lax.broadcasted_iota(jnp.int32, (BLOCK, C_WIDTH), 0)
    p_acc = [jnp.zeros((BLOCK, BLOCK), jnp.float32) for _ in range(C_HEADS)]
    for m in HGRN_LEVELS:
        if m == 0:
            qt, kt = q, kk
            pair_ok = row_i == col_i
        else:
            in_right = (rows % (2 * m)) >= m
            q_side = jnp.logical_not(in_right) if reverse else in_right
            anchor = _hgrn_anchor(b, rows, m, reverse)
            qt = jnp.where(q_side, q * jnp.exp(jnp.minimum(b - anchor, 0.0)), 0.0)
            kt = jnp.where(q_side, 0.0, kk * jnp.exp(jnp.minimum(anchor - b, 0.0)))
            pair_ok = (row_i // (2 * m)) == (col_i // (2 * m))
        kt16 = kt.astype(jnp.bfloat16)
        for h in range(C_HEADS):
            qh = jnp.where(head_of_lane == h, qt, 0.0).astype(jnp.bfloat16)
            ph = lax.dot_general(qh, kt16, (((1,), (1,)), ((), ())), preferred_element_type=jnp.float32)
            p_acc[h] = p_acc[h] + jnp.where(pair_ok, ph, 0.0)
    o = jnp.zeros((BLOCK, C_WIDTH), jnp.float32)
    for h in range(C_HEADS):
        vh = jnp.where(head_of_lane == h, v16, jnp.zeros_like(v16))
        o = o + jnp.dot(p_acc[h].astype(jnp.bfloat16), vh, preferred_element_type=jnp.float32)
    edge = 0 if reverse else BLOCK - 1
    b_all = b[edge:edge + 1, :]
    q_in = (q * jnp.exp(b)).astype(jnp.bfloat16)
    o = o + lax.dot_general(q_in, state_t.astype(jnp.bfloat16), (((1,), (1,)), ((), ())),
                            preferred_element_type=jnp.float32)
    k_out = (kk * jnp.exp(b_all - b)).astype(jnp.bfloat16)
    u_t = lax.dot_general(v16, k_out, (((0,), (0,)), ((), ())), preferred_element_type=jnp.float32)
    new_state = jnp.exp(b_all) * state_t + jnp.where(same_head, u_t, 0.0)
    return o, new_state


def _mixer_c_kernel(q_ref, v_ref, lff_ref, lfb_ref, kkf_ref, kkb_ref, g_ref, w_ref, o_ref, acc_ref):
    head_of_lane = lax.broadcasted_iota(jnp.int32, (BLOCK, C_WIDTH), 1) // HEAD_DIM
    sr = lax.broadcasted_iota(jnp.int32, (C_WIDTH, C_WIDTH), 0) // HEAD_DIM
    sc = lax.broadcasted_iota(jnp.int32, (C_WIDTH, C_WIDTH), 1) // HEAD_DIM
    same_head = sr == sc
    row_i = lax.broadcasted_iota(jnp.int32, (BLOCK, BLOCK), 0)
    col_i = lax.broadcasted_iota(jnp.int32, (BLOCK, BLOCK), 1)
    consts = (head_of_lane, same_head, row_i, col_i)
    w = w_ref[...]

    def run(reverse, lf_ref, kk_ref, finish):
        def body(i, state_t):
            n = (NBLK - 1 - i) if reverse else i
            r0 = pl.multiple_of(n * BLOCK, BLOCK)
            q = q_ref[0, pl.ds(r0, BLOCK), :].astype(jnp.float32)
            v16 = v_ref[0, pl.ds(r0, BLOCK), :]
            o, new_state = _hgrn_chunk(q, v16, kk_ref[0, pl.ds(r0, BLOCK), :], lf_ref[0, pl.ds(r0, BLOCK), :],
                                       state_t, reverse, consts)
            finish(r0, o)
            return new_state

        lax.fori_loop(0, NBLK, body, jnp.zeros((C_WIDTH, C_WIDTH), jnp.float32))

    def store_fwd(r0, o):
        acc_ref[pl.ds(r0, BLOCK), :] = o

    def store_out(r0, o):
        y = acc_ref[pl.ds(r0, BLOCK), :] + o
        ms = _head_mean_sq(y)
        y = y * lax.rsqrt(ms + RMS_EPS) * w
        y = y * g_ref[0, pl.ds(r0, BLOCK), :].astype(jnp.float32)
        o_ref[0, pl.ds(r0, BLOCK), :] = y.astype(o_ref.dtype)

    def _head_mean_sq(y):
        sq = y * y
        parts = []
        for h in range(C_HEADS):
            s = jnp.sum(sq[:, h * HEAD_DIM:(h + 1) * HEAD_DIM], axis=-1, keepdims=True) * (1.0 / HEAD_DIM)
            parts.append(jnp.broadcast_to(s, (BLOCK, HEAD_DIM)))
        return jnp.concatenate(parts, axis=-1)

    run(False, lff_ref, kkf_ref, store_fwd)
    run(True, lfb_ref, kkb_ref, store_out)


def _mixer_c(q3, v3, lf3, kk3, g3, norm_w):
    blk = lambda j: pl.BlockSpec((1, LP, C_WIDTH), lambda b: (b, 0, j))
    return pl.pallas_call(
        _mixer_c_kernel,
        grid=(BATCH,),
        in_specs=[
            blk(4),
            blk(0),
            blk(0), blk(1),
            blk(0), blk(1),
            blk(3),
            pl.BlockSpec((1, C_WIDTH), lambda b: (0, 0)),
        ],
        out_specs=pl.BlockSpec((1, LP, C_WIDTH), lambda b: (b, 0, 0)),
        out_shape=jax.ShapeDtypeStruct((BATCH, LP, C_WIDTH), jnp.bfloat16),
        scratch_shapes=[pltpu.VMEM((LP, C_WIDTH), jnp.float32)],
        compiler_params=_cparams(("arbitrary",)),
        name="mixer_c_hgrn2",
    )(q3, v3, lf3, lf3, kk3, kk3, g3, norm_w.reshape(1, C_WIDTH))


def _outproj_kernel(ya_ref, yb_ref, yc_ref, wa_ref, wb_ref, wc_ref, h_ref, g_ref, b_ref, h32_ref, h16_ref):
    acc = jnp.dot(ya_ref[...], wa_ref[...], preferred_element_type=jnp.float32)
    acc = acc + jnp.dot(yb_ref[...], wb_ref[...], preferred_element_type=jnp.float32)
    acc = acc + jnp.dot(yc_ref[...], wc_ref[...], preferred_element_type=jnp.float32)
    y = _layer_norm(ALPHA * h_ref[...] + acc, g_ref[...], b_ref[...])
    h32_ref[...] = y
    h16_ref[...] = y.astype(jnp.bfloat16)


def _outproj(ya, yb, yc, wa, wb, wc, h32, g, b):
    row = lambda n: pl.BlockSpec((TM, n), lambda i: (i, 0))
    full = lambda n: pl.BlockSpec((n, D_MODEL), lambda i: (0, 0))
    vec = pl.BlockSpec((1, D_MODEL), lambda i: (0, 0))
    return pl.pallas_call(
        _outproj_kernel,
        grid=(M_ROWS // TM,),
        in_specs=[row(A_WIDTH), row(B_WIDTH), row(C_WIDTH), full(A_WIDTH), full(B_WIDTH), full(C_WIDTH),
                  row(D_MODEL), vec, vec],
        out_specs=[row(D_MODEL), row(D_MODEL)],
        out_shape=[jax.ShapeDtypeStruct((M_ROWS, D_MODEL), jnp.float32),
                   jax.ShapeDtypeStruct((M_ROWS, D_MODEL), jnp.bfloat16)],
        compiler_params=_cparams(("arbitrary",)),
        name="outproj_residual_ln",
    )(ya, yb, yc, wa, wb, wc, h32, g.reshape(1, D_MODEL), b.reshape(1, D_MODEL))


def _regroup_w_in(w):
    def heads(lo, order):
        return [w[:, lo + h * HEAD_DIM: lo + (h + 1) * HEAD_DIM] for h in order]

    c = lambda lo, n: w[:, lo:lo + n]
    o_qa, o_ka, o_va, o_ga = 0, 256, 384, 512
    o_qb, o_kb, o_vb, o_gb = 768, 1280, 1792, 2304
    o_qc, o_ic, o_zf, o_zb, o_gc = 2816, 3072, 3328, 3584, 3840
    cat = lambda xs: jnp.concatenate(xs, axis=1).astype(jnp.bfloat16)
    w_q = cat(heads(o_qa, A_HEAD_ORDER) + [c(o_qb, 512)])
    w_k = cat([c(o_ka, 128), c(o_kb, 512)])
    w_p = cat([c(o_ic, 256), c(o_va, 128), c(o_vb, 512)])
    w_s = cat(heads(o_ga, A_HEAD_ORDER) + [c(o_gb, 512), c(o_gc, 256), c(o_qc, 256)])
    w_g = cat([c(o_zf, 256), c(o_zb, 256)])
    return w_q, w_k, w_p, w_s, w_g


def _rope_tables():
    pos = (jnp.arange(LP) - PAD).astype(jnp.float32)
    inv = ROPE_THETA ** (-jnp.arange(0, HEAD_DIM, 2, dtype=jnp.float32) / HEAD_DIM)
    ang = pos[:, None] * inv[None, :]
    cos, sin = jnp.cos(ang), jnp.sin(ang)
    cos_t = jnp.concatenate([cos, cos, cos, cos], axis=1)
    sin_t = jnp.concatenate([-sin, sin, -sin, sin], axis=1)
    return cos_t, sin_t


def kernel(x, meta, emb_ln_g, emb_ln_b, w_in, w_out, a_sink, b_lam, b_subln_w, c_lb_logits, c_norm_w, ln_g, ln_b):
    cos_t, sin_t = _rope_tables()
    scale = HEAD_DIM ** -0.5
    p_lb = jax.nn.softmax(c_lb_logits.astype(jnp.float32), axis=0)
    lb_all = jnp.cumsum(p_lb, axis=0) - p_lb[0:1]

    h32, h16 = _embed(x, meta, emb_ln_g, emb_ln_b)
    h32 = h32.reshape(M_ROWS, D_MODEL)
    h16 = h16.reshape(M_ROWS, D_MODEL)
    for l in range(DEPTH):
        lam_init = 0.8 - 0.6 * math.exp(-0.3 * l)
        lp32 = b_lam[l].astype(jnp.float32)
        lam = jnp.exp(jnp.sum(lp32[0] * lp32[1])) - jnp.exp(jnp.sum(lp32[2] * lp32[3])) + lam_init
        w_q, w_k, w_p, w_s, w_g = _regroup_w_in(w_in[l])
        wo = w_out[l]
        wa = jnp.concatenate([wo[h * HEAD_DIM:(h + 1) * HEAD_DIM] for h in A_HEAD_ORDER], axis=0).astype(jnp.bfloat16)
        wb = wo[A_WIDTH:A_WIDTH + B_WIDTH].astype(jnp.bfloat16)
        wc = wo[A_WIDTH + B_WIDTH:].astype(jnp.bfloat16)
        sink = jnp.stack([a_sink[l, h] for h in A_HEAD_ORDER]).astype(jnp.float32)

        q_all = _inproj("rope", h16, w_q, (cos_t * scale, sin_t * scale)).reshape(BATCH, LP, N_ROPE_Q)
        k_all = _inproj("rope", h16, w_k, (cos_t, sin_t)).reshape(BATCH, LP, N_ROPE_K)
        p_all = _inproj("plain", h16, w_p).reshape(BATCH, LP, N_PLAIN)
        s_all = _inproj("silu", h16, w_s).reshape(BATCH, LP, N_SILU)
        lf_all, kk_all = _inproj("gate", h16, w_g, (lb_all[l].reshape(1, C_WIDTH),))
        lf_all = lf_all.reshape(BATCH, LP, N_GATE)
        kk_all = kk_all.reshape(BATCH, LP, N_GATE)

        ya = _mixer_a(sink, q_all, k_all, p_all, s_all)
        yb = _mixer_b(lam_init, lam.reshape(1), q_all, k_all, p_all, s_all, b_subln_w[l])
        yc = _mixer_c(s_all, p_all, lf_all, kk_all, s_all, c_norm_w[l])

        h32, h16 = _outproj(ya.reshape(M_ROWS, A_WIDTH), yb.reshape(M_ROWS, B_WIDTH), yc.reshape(M_ROWS, C_WIDTH),
                            wa, wb, wc, h32, ln_g[l], ln_b[l])
    return h32.reshape(BATCH, LP, D_MODEL)[:, PAD + N_META:]
```

```python
import functools
import math

import jax
import jax.numpy as jnp
import numpy as np
from jax import lax
from jax.experimental import pallas as pl
from jax.experimental.pallas import tpu as pltpu

D_MODEL = 1024
BATCH = 8
SEQ = 2048
DEPTH = 2
N_META = 16
BLOCK = 128
PAD = BLOCK - N_META
WINDOW = 128
ROPE_THETA = 10000.0
HEAD_DIM = 64
A_HEADS = 4
A_KV_HEADS = 2
A_WIDTH = A_HEADS * HEAD_DIM
B_HEADS = 4
B_V_DIM = 2 * HEAD_DIM
B_WIDTH = B_HEADS * B_V_DIM
C_HEADS = 4
C_WIDTH = C_HEADS * HEAD_DIM
D_MIX = A_WIDTH + B_WIDTH + C_WIDTH
ALPHA = (2 * DEPTH) ** 0.25
LN_EPS = 1e-5
RMS_EPS = 1e-6
NEG = -1e30
F_MIN = 1e-30

LP = SEQ + N_META + PAD
NBLK = LP // BLOCK
M_ROWS = BATCH * LP
TM = LP // 4
LANES = 128
SUBLANES = 8
MXU_N = 256
TQ_B = LP // 8
SM_ROWS = 16
VMEM_LIMIT = 48 * 1024 * 1024

N_ROPE_Q = A_WIDTH + 2 * B_HEADS * HEAD_DIM
N_ROPE_K = A_KV_HEADS * HEAD_DIM + 2 * B_HEADS * HEAD_DIM
N_PLAIN = C_WIDTH + A_KV_HEADS * HEAD_DIM + B_WIDTH
N_SILU = A_WIDTH + B_WIDTH + C_WIDTH + C_WIDTH
N_GATE = 2 * C_WIDTH
N_IN = N_ROPE_Q + N_ROPE_K + N_PLAIN + N_SILU + N_GATE
A_HEAD_ORDER = (0, 2, 1, 3)

HGRN_LEVELS = (0, 1, 2, 4, 8, 16, 32, 64)


def _cparams(sem):
    return pltpu.CompilerParams(dimension_semantics=sem, vmem_limit_bytes=VMEM_LIMIT)


def _layer_norm(x, g, b):
    mu = jnp.mean(x, axis=-1, keepdims=True)
    xc = x - mu
    var = jnp.mean(xc * xc, axis=-1, keepdims=True)
    return xc * lax.rsqrt(var + LN_EPS) * g + b


def _embed_kernel(x_ref, meta_ref, g_ref, b_ref, h32_ref, h16_ref):
    j = pl.program_id(1)
    head = TM - BLOCK

    def put(rows, y):
        h32_ref[0, rows, :] = y
        h16_ref[0, rows, :] = y.astype(jnp.bfloat16)

    @pl.when(j == 0)
    def _():
        put(slice(0, PAD), jnp.zeros((PAD, D_MODEL), jnp.float32))
        put(slice(PAD, BLOCK), _layer_norm(meta_ref[...], g_ref[...], b_ref[...]))
        put(slice(BLOCK, TM), _layer_norm(x_ref[0, 0:head, :], g_ref[...], b_ref[...]))

    @pl.when(j > 0)
    def _():
        start = pl.multiple_of(j * TM - BLOCK, SUBLANES)
        put(slice(0, TM), _layer_norm(x_ref[0, pl.ds(start, TM), :], g_ref[...], b_ref[...]))


def _embed(x, meta, g, b):
    return pl.pallas_call(
        _embed_kernel,
        grid=(BATCH, LP // TM),
        in_specs=[
            pl.BlockSpec((1, SEQ, D_MODEL), lambda bi, j: (bi, 0, 0)),
            pl.BlockSpec((N_META, D_MODEL), lambda bi, j: (0, 0)),
            pl.BlockSpec((1, D_MODEL), lambda bi, j: (0, 0)),
            pl.BlockSpec((1, D_MODEL), lambda bi, j: (0, 0)),
        ],
        out_specs=[
            pl.BlockSpec((1, TM, D_MODEL), lambda bi, j: (bi, j, 0)),
            pl.BlockSpec((1, TM, D_MODEL), lambda bi, j: (bi, j, 0)),
        ],
        out_shape=[
            jax.ShapeDtypeStruct((BATCH, LP, D_MODEL), jnp.float32),
            jax.ShapeDtypeStruct((BATCH, LP, D_MODEL), jnp.bfloat16),
        ],
        compiler_params=_cparams(("arbitrary", "arbitrary")),
        name="embed_ln",
    )(x, meta, g.reshape(1, D_MODEL), b.reshape(1, D_MODEL))


def _col_chunks(n):
    return [(c, min(MXU_N, n - c)) for c in range(0, n, MXU_N)]


def _rope_cols(acc, cos, sin_signed, first_half):
    partner = jnp.where(first_half, pltpu.roll(acc, LANES - HEAD_DIM // 2, 1), pltpu.roll(acc, HEAD_DIM // 2, 1))
    return acc * cos + partner * sin_signed


def _inproj_kernel(x_ref, w_ref, cos_ref, sin_ref, lb_ref, q_ref, k_ref, p_ref, s_ref, lf_ref, kk_ref):
    x = x_ref[...]
    cos = cos_ref[...]
    sin = sin_ref[...]
    lane = lax.broadcasted_iota(jnp.int32, (TM, LANES), 1)
    first_half = (lane % HEAD_DIM) < (HEAD_DIM // 2)
    row = (pl.program_id(0) % (LP // TM)) * TM + lax.broadcasted_iota(jnp.int32, (TM, LANES), 0)
    valid = row >= PAD
    pieces = ([("rope_q", q_ref, c) for c in range(0, N_ROPE_Q, LANES)]
              + [("rope_k", k_ref, c) for c in range(0, N_ROPE_K, LANES)]
              + [("plain", p_ref, c) for c in range(0, N_PLAIN, LANES)]
              + [("silu", s_ref, c) for c in range(0, N_SILU, LANES)]
              + [("gate", None, c) for c in range(0, N_GATE, LANES)])
    for c0, cw in _col_chunks(N_IN):
        acc = jnp.dot(x, w_ref[:, c0:c0 + cw], preferred_element_type=jnp.float32)
        for j in range(cw // LANES):
            kind, dst, dc = pieces[c0 // LANES + j]
            a = acc[:, j * LANES:(j + 1) * LANES]
            if kind == "rope_q":
                dst[:, dc:dc + LANES] = (_rope_cols(a, cos, sin, first_half) * (HEAD_DIM ** -0.5)).astype(dst.dtype)
            elif kind == "rope_k":
                dst[:, dc:dc + LANES] = _rope_cols(a, cos, sin, first_half).astype(dst.dtype)
            elif kind == "plain":
                dst[:, dc:dc + LANES] = a.astype(dst.dtype)
            elif kind == "silu":
                dst[:, dc:dc + LANES] = (a * jax.nn.sigmoid(a)).astype(dst.dtype)
            else:
                lb = lb_ref[:, dc % C_WIDTH: dc % C_WIDTH + LANES]
                f = lb + (1.0 - lb) * jax.nn.sigmoid(a)
                lf_ref[:, dc:dc + LANES] = jnp.where(valid, jnp.log(jnp.maximum(f, F_MIN)), 0.0)
                kk_ref[:, dc:dc + LANES] = jnp.where(valid, (1.0 - lb) * jax.nn.sigmoid(-a), 0.0)


def _inproj(h16, w, cos_t, sin_t, lb):
    row = lambda n: pl.BlockSpec((TM, n), lambda i: (i, 0))
    tab_spec = pl.BlockSpec((TM, LANES), lambda i: (i % (LP // TM), 0))
    widths = (N_ROPE_Q, N_ROPE_K, N_PLAIN, N_SILU, N_GATE, N_GATE)
    dtypes = (jnp.bfloat16,) * 4 + (jnp.float32,) * 2
    return pl.pallas_call(
        _inproj_kernel,
        grid=(M_ROWS // TM,),
        in_specs=[row(D_MODEL), pl.BlockSpec((D_MODEL, N_IN), lambda i: (0, 0)), tab_spec, tab_spec,
                  pl.BlockSpec((1, C_WIDTH), lambda i: (0, 0))],
        out_specs=[row(n) for n in widths],
        out_shape=[jax.ShapeDtypeStruct((M_ROWS, n), dt) for n, dt in zip(widths, dtypes)],
        compiler_params=_cparams(("arbitrary",)),
        name="inproj_fused",
    )(h16, w, cos_t, sin_t, lb)


def _mixer_a_kernel(sink_ref, q_ref, k_ref, v_ref, g_ref, o_ref):
    two = 2 * BLOCK
    row = lax.broadcasted_iota(jnp.int32, (two, 4 * BLOCK), 0) % BLOCK
    col = lax.broadcasted_iota(jnp.int32, (two, 4 * BLOCK), 1)
    is_meta = (col < BLOCK) & (col >= PAD)
    top = lax.broadcasted_iota(jnp.int32, (two, 1), 0) < BLOCK
    lane_hi = lax.broadcasted_iota(jnp.int32, (BLOCK, LANES), 1) >= HEAD_DIM
    masks = {}
    for n in range(NBLK):
        start = min(max((n - 1) * BLOCK, 0), LP - 3 * BLOCK)
        key = (n * BLOCK - start, start < BLOCK)
        if key not in masks:
            kpos = start + col - BLOCK
            band = (col >= BLOCK) & (jnp.abs(n * BLOCK + row - kpos) <= WINDOW)
            if start < BLOCK:
                band = band & (kpos >= BLOCK)
            masks[key] = is_meta | band
        ok = masks[key]
        kcat = jnp.concatenate([k_ref[0, 0:BLOCK, :], k_ref[0, start:start + 3 * BLOCK, :]], axis=0)
        vcat = jnp.concatenate([v_ref[0, 0:BLOCK, :], v_ref[0, start:start + 3 * BLOCK, :]], axis=0)
        rows = slice(n * BLOCK, (n + 1) * BLOCK)
        qcols = [q_ref[0, rows, cv * LANES:(cv + 1) * LANES] for cv in range(2)]
        outs = []
        for hf in range(2):
            sel = lane_hi if hf else jnp.logical_not(lane_hi)
            q2 = jnp.concatenate([jnp.where(sel, qc, jnp.zeros_like(qc)) for qc in qcols], axis=0)
            sink = jnp.where(top, sink_ref[hf], sink_ref[2 + hf])
            s = lax.dot_general(q2, kcat, (((1,), (1,)), ((), ())), preferred_element_type=jnp.float32)
            s = jnp.where(ok, s, NEG)
            m = jnp.maximum(jnp.max(s, axis=-1, keepdims=True), sink)
            e = jnp.exp(s - m)
            den = jnp.sum(e, axis=-1, keepdims=True) + jnp.exp(sink - m)
            p = (e * (1.0 / den)).astype(jnp.bfloat16)
            outs.append(jnp.dot(p, vcat, preferred_element_type=jnp.float32))
        for cv in range(2):
            o = jnp.where(lane_hi, outs[1][cv * BLOCK:(cv + 1) * BLOCK], outs[0][cv * BLOCK:(cv + 1) * BLOCK])
            y = o * g_ref[0, rows, cv * LANES:(cv + 1) * LANES].astype(jnp.float32)
            o_ref[0, rows, cv * LANES:(cv + 1) * LANES] = y.astype(o_ref.dtype)


def _mixer_a(sink, q3, k3, v3, g3):
    return pl.pallas_call(
        _mixer_a_kernel,
        grid=(BATCH,),
        in_specs=[
            pl.BlockSpec(memory_space=pltpu.SMEM),
            pl.BlockSpec((1, LP, A_WIDTH), lambda b: (b, 0, 0)),
            pl.BlockSpec((1, LP, LANES), lambda b: (b, 0, 0)),
            pl.BlockSpec((1, LP, LANES), lambda b: (b, 0, C_WIDTH // LANES)),
            pl.BlockSpec((1, LP, A_WIDTH), lambda b: (b, 0, 0)),
        ],
        out_specs=pl.BlockSpec((1, LP, A_WIDTH), lambda b: (b, 0, 0)),
        out_shape=jax.ShapeDtypeStruct((BATCH, LP, A_WIDTH), jnp.bfloat16),
        compiler_params=_cparams(("arbitrary",)),
        name="mixer_a_window_gqa",
    )(sink, q3, k3, v3, g3)


def _mixer_b_kernel(lam_init, lam_ref, q_ref, k_ref, v_ref, g_ref, w_ref, o_ref, s_a, s_b, a_a, a_b):
    lam = lam_ref[0]
    lane_hi = lax.broadcasted_iota(jnp.int32, (TQ_B, LANES), 1) >= HEAD_DIM
    key_ok = lax.broadcasted_iota(jnp.int32, (TQ_B, BLOCK), 1) >= PAD
    w = w_ref[...] * (1.0 - lam_init)
    s_bufs, a_bufs = (s_a, s_b), (a_a, a_b)
    nt = LP // TQ_B
    nt_dims = (((1,), (1,)), ((), ()))

    def scores(t):
        s_buf = s_bufs[t % 2]
        q = q_ref[0, t * TQ_B:(t + 1) * TQ_B, :]
        for c in range(2):
            qm = jnp.where(lane_hi if c else jnp.logical_not(lane_hi), q, jnp.zeros_like(q))
            s0 = lax.dot_general(qm, k_ref[0, 0:BLOCK, :], nt_dims, preferred_element_type=jnp.float32)
            s_buf[c, :, 0:BLOCK] = jnp.where(key_ok, s0, NEG)
            s_buf[c, :, BLOCK:LP] = lax.dot_general(qm, k_ref[0, BLOCK:LP, :], nt_dims,
                                                    preferred_element_type=jnp.float32)

    def softmax_pv(t):
        s_buf, a_buf = s_bufs[t % 2], a_bufs[t % 2]
        inv_den0 = []
        for r in range(TQ_B // SM_ROWS):
            rows = slice(r * SM_ROWS, (r + 1) * SM_ROWS)
            es, dens = [], []
            for c in range(2):
                s = s_buf[c, rows, :]
                m = jnp.max(s, axis=-1, keepdims=True)
                e = jnp.exp(s - m)
                es.append(e)
                dens.append(jnp.sum(e, axis=-1, keepdims=True))
            a_buf[rows, :] = (es[0] - es[1] * (dens[0] * lam / dens[1])).astype(jnp.bfloat16)
            inv_den0.append(1.0 / dens[0])
        o = jnp.dot(a_buf[...], v_ref[0], preferred_element_type=jnp.float32)
        o = o * jnp.concatenate(inv_den0, axis=0)
        ms = jnp.mean(o * o, axis=-1, keepdims=True)
        y = o * lax.rsqrt(ms + RMS_EPS) * w
        y = y * g_ref[0, t * TQ_B:(t + 1) * TQ_B, :].astype(jnp.float32)
        o_ref[0, t * TQ_B:(t + 1) * TQ_B, :] = y.astype(o_ref.dtype)

    for t in range(nt + 1):
        if t < nt:
            scores(t)
        if t >= 1:
            softmax_pv(t - 1)


def _mixer_b(lam_init, lam, q3, k3, v3, g3, subln_w):
    qoff = A_WIDTH // LANES
    koff = (A_KV_HEADS * HEAD_DIM) // LANES
    voff = (C_WIDTH + A_KV_HEADS * HEAD_DIM) // LANES
    goff = A_WIDTH // LANES
    return pl.pallas_call(
        functools.partial(_mixer_b_kernel, lam_init),
        grid=(BATCH, B_HEADS),
        in_specs=[
            pl.BlockSpec(memory_space=pltpu.SMEM),
            pl.BlockSpec((1, LP, LANES), lambda b, h: (b, 0, qoff + h)),
            pl.BlockSpec((1, LP, LANES), lambda b, h: (b, 0, koff + h)),
            pl.BlockSpec((1, LP, LANES), lambda b, h: (b, 0, voff + h)),
            pl.BlockSpec((1, LP, LANES), lambda b, h: (b, 0, goff + h)),
            pl.BlockSpec((1, B_V_DIM), lambda b, h: (0, h)),
        ],
        out_specs=pl.BlockSpec((1, LP, LANES), lambda b, h: (b, 0, h)),
        out_shape=jax.ShapeDtypeStruct((BATCH, LP, B_WIDTH), jnp.bfloat16),
        scratch_shapes=[pltpu.VMEM((2, TQ_B, LP), jnp.float32), pltpu.VMEM((2, TQ_B, LP), jnp.float32),
                        pltpu.VMEM((TQ_B, LP), jnp.bfloat16), pltpu.VMEM((TQ_B, LP), jnp.bfloat16)],
        compiler_params=_cparams(("arbitrary", "arbitrary")),
        name="mixer_b_diff_attn",
    )(lam, q3, k3, v3, g3, subln_w.reshape(1, B_WIDTH))


def _split3(x):
    hi = x.astype(jnp.bfloat16)
    r1 = x - hi.astype(jnp.float32)
    mid = r1.astype(jnp.bfloat16)
    lo = (r1 - mid.astype(jnp.float32)).astype(jnp.bfloat16)
    return hi, mid, lo


def _hgrn_anchor(b, m, reverse):
    a_off = m if reverse else m - 1
    if 2 * m >= SUBLANES:
        b3 = b.reshape(BLOCK // (2 * m), 2 * m, C_WIDTH)
        return jnp.broadcast_to(b3[:, a_off:a_off + 1, :], b3.shape).reshape(BLOCK, C_WIDTH)
    b3 = b.reshape(BLOCK // SUBLANES, SUBLANES, C_WIDTH)
    sub = lax.broadcasted_iota(jnp.int32, b3.shape, 1)
    anchor = None
    for blk in range(SUBLANES // (2 * m)):
        r = blk * 2 * m + a_off
        cand = jnp.broadcast_to(b3[:, r:r + 1, :], b3.shape)
        anchor = cand if anchor is None else jnp.where(sub >= blk * 2 * m, cand, anchor)
    return anchor.reshape(BLOCK, C_WIDTH)


def _hgrn_chunk(q, v16, kk, lf, state_t, reverse, cst):
    tri16, head16, pair_key, tri_ok, same_head = cst
    nt_dims = (((1,), (1,)), ((), ()))
    hi, mid, lo = _split3(lf)
    b = (jnp.dot(tri16, hi, preferred_element_type=jnp.float32)
         + jnp.dot(tri16, mid, preferred_element_type=jnp.float32)
         + jnp.dot(tri16, lo, preferred_element_type=jnp.float32))
    p = None
    for m in reversed(HGRN_LEVELS):
        if m == 0:
            qt16, kt16 = q.astype(jnp.bfloat16), kk.astype(jnp.bfloat16)
        else:
            wgt = jnp.exp(-jnp.abs(b - _hgrn_anchor(b, m, reverse)))
            qt16, kt16 = (q * wgt).astype(jnp.bfloat16), (kk * wgt).astype(jnp.bfloat16)
        k_heads = jnp.concatenate([kt16 * head16[h] for h in range(C_HEADS)], axis=0)
        pm = lax.dot_general(qt16, k_heads, nt_dims, preferred_element_type=jnp.float32)
        p = pm if p is None else jnp.where(pair_key < max(2 * m, 1), pm, p)
    p = jnp.where(tri_ok, p, 0.0)
    v_heads = jnp.concatenate([v16 * head16[h] for h in range(C_HEADS)], axis=0)
    o = jnp.dot(p.astype(jnp.bfloat16), v_heads, preferred_element_type=jnp.float32)
    edge = 0 if reverse else BLOCK - 1
    b_all = b[edge:edge + 1, :]
    q_in = (q * jnp.exp(b)).astype(jnp.bfloat16)
    o = o + lax.dot_general(q_in, state_t.astype(jnp.bfloat16), nt_dims, preferred_element_type=jnp.float32)
    k_out = (kk * jnp.exp(b_all - b)).astype(jnp.bfloat16)
    u_t = lax.dot_general(v16, k_out, (((0,), (0,)), ((), ())), preferred_element_type=jnp.float32)
    new_state = jnp.exp(b_all) * state_t + jnp.where(same_head, u_t, 0.0)
    return o, new_state


def _mixer_c_kernel(q_ref, v_ref, lff_ref, lfb_ref, kkf_ref, kkb_ref, g_ref, w_ref, o_ref, accf_ref, accr_ref):
    lane_head = lax.broadcasted_iota(jnp.int32, (BLOCK, C_WIDTH), 1) // HEAD_DIM
    head16 = [(lane_head == h).astype(jnp.bfloat16) for h in range(C_HEADS)]
    sr = lax.broadcasted_iota(jnp.int32, (C_WIDTH, C_WIDTH), 0) // HEAD_DIM
    sc = lax.broadcasted_iota(jnp.int32, (C_WIDTH, C_WIDTH), 1) // HEAD_DIM
    same_head = sr == sc
    head_ones16 = same_head.astype(jnp.bfloat16)
    t_i = lax.broadcasted_iota(jnp.int32, (BLOCK, C_HEADS * BLOCK), 0)
    s_i = lax.broadcasted_iota(jnp.int32, (BLOCK, C_HEADS * BLOCK), 1) % BLOCK
    pair_key = t_i ^ s_i
    r_i = lax.broadcasted_iota(jnp.int32, (BLOCK, BLOCK), 0)
    c_i = lax.broadcasted_iota(jnp.int32, (BLOCK, BLOCK), 1)
    cst_f = ((c_i <= r_i).astype(jnp.bfloat16), head16, pair_key, s_i <= t_i, same_head)
    cst_r = ((c_i >= r_i).astype(jnp.bfloat16), head16, pair_key, s_i >= t_i, same_head)

    def body(i, states):
        state_f, state_r = states
        outs = []
        for reverse, lf_ref, kk_ref, cst, state in ((False, lff_ref, kkf_ref, cst_f, state_f),
                                                     (True, lfb_ref, kkb_ref, cst_r, state_r)):
            n = (NBLK - 1 - i) if reverse else i
            r0 = pl.multiple_of(n * BLOCK, BLOCK)
            q = q_ref[0, pl.ds(r0, BLOCK), :].astype(jnp.float32)
            o, new_state = _hgrn_chunk(q, v_ref[0, pl.ds(r0, BLOCK), :], kk_ref[0, pl.ds(r0, BLOCK), :],
                                       lf_ref[0, pl.ds(r0, BLOCK), :], state, reverse, cst)
            (accr_ref if reverse else accf_ref)[pl.ds(r0, BLOCK), :] = o
            outs.append(new_state)
        return tuple(outs)

    zero_state = jnp.zeros((C_WIDTH, C_WIDTH), jnp.float32)
    lax.fori_loop(0, NBLK, body, (zero_state, zero_state), unroll=2)

    w = w_ref[...]

    def finish(n, carry):
        r0 = pl.multiple_of(n * BLOCK, BLOCK)
        y = accf_ref[pl.ds(r0, BLOCK), :] + accr_ref[pl.ds(r0, BLOCK), :]
        sq = y * y
        sq_hi = sq.astype(jnp.bfloat16)
        sq_lo = (sq - sq_hi.astype(jnp.float32)).astype(jnp.bfloat16)
        ms = (jnp.dot(sq_hi, head_ones16, preferred_element_type=jnp.float32)
              + jnp.dot(sq_lo, head_ones16, preferred_element_type=jnp.float32)) * (1.0 / HEAD_DIM)
        y = y * lax.rsqrt(ms + RMS_EPS) * w
        y = y * g_ref[0, pl.ds(r0, BLOCK), :].astype(jnp.float32)
        o_ref[0, pl.ds(r0, BLOCK), :] = y.astype(o_ref.dtype)
        return carry

    lax.fori_loop(0, NBLK, finish, 0)


def _mixer_c(q3, v3, lf3, kk3, g3, norm_w):
    blk = lambda j: pl.BlockSpec((1, LP, C_WIDTH), lambda b: (b, 0, j))
    return pl.pallas_call(
        _mixer_c_kernel,
        grid=(BATCH,),
        in_specs=[
            blk(4),
            blk(0),
            blk(0), blk(1),
            blk(0), blk(1),
            blk(3),
            pl.BlockSpec((1, C_WIDTH), lambda b: (0, 0)),
        ],
        out_specs=pl.BlockSpec((1, LP, C_WIDTH), lambda b: (b, 0, 0)),
        out_shape=jax.ShapeDtypeStruct((BATCH, LP, C_WIDTH), jnp.bfloat16),
        scratch_shapes=[pltpu.VMEM((LP, C_WIDTH), jnp.float32), pltpu.VMEM((LP, C_WIDTH), jnp.float32)],
        compiler_params=_cparams(("arbitrary",)),
        name="mixer_c_hgrn2",
    )(q3, v3, lf3, lf3, kk3, kk3, g3, norm_w.reshape(1, C_WIDTH))


def _outproj_kernel(ya_ref, yb_ref, yc_ref, wa_ref, wb_ref, wc_ref, h_ref, g_ref, b_ref, h32_ref, h16_ref):
    acc = jnp.dot(ya_ref[...], wa_ref[...], preferred_element_type=jnp.float32)
    acc = acc + jnp.dot(yb_ref[...], wb_ref[...], preferred_element_type=jnp.float32)
    acc = acc + jnp.dot(yc_ref[...], wc_ref[...], preferred_element_type=jnp.float32)
    y = _layer_norm(ALPHA * h_ref[...] + acc, g_ref[...], b_ref[...])
    h32_ref[...] = y
    h16_ref[...] = y.astype(jnp.bfloat16)


def _outproj_last_kernel(ya_ref, yb_ref, yc_ref, wa_ref, wb_ref, wc_ref, h_ref, g_ref, b_ref, o_ref):
    acc = jnp.dot(ya_ref[...], wa_ref[...], preferred_element_type=jnp.float32)
    acc = acc + jnp.dot(yb_ref[...], wb_ref[...], preferred_element_type=jnp.float32)
    acc = acc + jnp.dot(yc_ref[...], wc_ref[...], preferred_element_type=jnp.float32)
    o_ref[0] = _layer_norm(ALPHA * h_ref[...] + acc, g_ref[...], b_ref[...])


def _outproj_last(ya, yb, yc, wa, wb, wc, h32, g, b):
    tl = SEQ // 4
    row = lambda n: pl.BlockSpec((pl.Element(tl), pl.Element(n)),
                                 lambda bi, j: (pl.multiple_of(bi * LP + BLOCK + j * tl, BLOCK), 0))
    full = lambda n: pl.BlockSpec((n, D_MODEL), lambda bi, j: (0, 0))
    vec = pl.BlockSpec((1, D_MODEL), lambda bi, j: (0, 0))
    return pl.pallas_call(
        _outproj_last_kernel,
        grid=(BATCH, SEQ // tl),
        in_specs=[row(A_WIDTH), row(B_WIDTH), row(C_WIDTH), full(A_WIDTH), full(B_WIDTH), full(C_WIDTH),
                  row(D_MODEL), vec, vec],
        out_specs=pl.BlockSpec((1, tl, D_MODEL), lambda bi, j: (bi, j, 0)),
        out_shape=jax.ShapeDtypeStruct((BATCH, SEQ, D_MODEL), jnp.float32),
        compiler_params=_cparams(("arbitrary", "arbitrary")),
        name="outproj_last_residual_ln",
    )(ya.reshape(M_ROWS, A_WIDTH), yb.reshape(M_ROWS, B_WIDTH), yc.reshape(M_ROWS, C_WIDTH), wa, wb, wc, h32,
      g.reshape(1, D_MODEL), b.reshape(1, D_MODEL))


def _outproj(ya, yb, yc, wa, wb, wc, h32, g, b):
    row = lambda n: pl.BlockSpec((TM, n), lambda i: (i, 0))
    full = lambda n: pl.BlockSpec((n, D_MODEL), lambda i: (0, 0))
    vec = pl.BlockSpec((1, D_MODEL), lambda i: (0, 0))
    return pl.pallas_call(
        _outproj_kernel,
        grid=(M_ROWS // TM,),
        in_specs=[row(A_WIDTH), row(B_WIDTH), row(C_WIDTH), full(A_WIDTH), full(B_WIDTH), full(C_WIDTH),
                  row(D_MODEL), vec, vec],
        out_specs=[row(D_MODEL), row(D_MODEL)],
        out_shape=[jax.ShapeDtypeStruct((M_ROWS, D_MODEL), jnp.float32),
                   jax.ShapeDtypeStruct((M_ROWS, D_MODEL), jnp.bfloat16)],
        compiler_params=_cparams(("arbitrary",)),
        name="outproj_residual_ln",
    )(ya, yb, yc, wa, wb, wc, h32, g.reshape(1, D_MODEL), b.reshape(1, D_MODEL))


def _regroup_w_in(w):
    def heads(lo, order):
        return [w[:, lo + h * HEAD_DIM: lo + (h + 1) * HEAD_DIM] for h in order]

    c = lambda lo, n: w[:, lo:lo + n]
    o_qa, o_ka, o_va, o_ga = 0, 256, 384, 512
    o_qb, o_kb, o_vb, o_gb = 768, 1280, 1792, 2304
    o_qc, o_ic, o_zf, o_zb, o_gc = 2816, 3072, 3328, 3584, 3840
    cat = lambda xs: jnp.concatenate(xs, axis=1).astype(jnp.bfloat16)
    w_q = cat(heads(o_qa, A_HEAD_ORDER) + [c(o_qb, 512)])
    w_k = cat([c(o_ka, 128), c(o_kb, 512)])
    w_p = cat([c(o_ic, 256), c(o_va, 128), c(o_vb, 512)])
    w_s = cat(heads(o_ga, A_HEAD_ORDER) + [c(o_gb, 512), c(o_gc, 256), c(o_qc, 256)])
    w_g = cat([c(o_zf, 256), c(o_zb, 256)])
    return jnp.concatenate([w_q, w_k, w_p, w_s, w_g], axis=1)


def _rope_tables():
    pos = (jnp.arange(LP) - PAD).astype(jnp.float32)
    inv = ROPE_THETA ** (-jnp.arange(0, HEAD_DIM, 2, dtype=jnp.float32) / HEAD_DIM)
    ang = pos[:, None] * inv[None, :]
    cos, sin = jnp.cos(ang), jnp.sin(ang)
    cos_t = jnp.concatenate([cos, cos, cos, cos], axis=1)
    sin_t = jnp.concatenate([-sin, sin, -sin, sin], axis=1)
    return cos_t, sin_t


def kernel(x, meta, emb_ln_g, emb_ln_b, w_in, w_out, a_sink, b_lam, b_subln_w, c_lb_logits, c_norm_w, ln_g, ln_b):
    cos_t, sin_t = _rope_tables()
    p_lb = jax.nn.softmax(c_lb_logits.astype(jnp.float32), axis=0)
    lb_all = jnp.cumsum(p_lb, axis=0) - p_lb[0:1]

    h32, h16 = _embed(x, meta, emb_ln_g, emb_ln_b)
    h32 = h32.reshape(M_ROWS, D_MODEL)
    h16 = h16.reshape(M_ROWS, D_MODEL)
    for l in range(DEPTH):
        lam_init = 0.8 - 0.6 * math.exp(-0.3 * l)
        lp32 = b_lam[l].astype(jnp.float32)
        lam = jnp.exp(jnp.sum(lp32[0] * lp32[1])) - jnp.exp(jnp.sum(lp32[2] * lp32[3])) + lam_init
        wo = w_out[l]
        wa = jnp.concatenate([wo[h * HEAD_DIM:(h + 1) * HEAD_DIM] for h in A_HEAD_ORDER], axis=0).astype(jnp.bfloat16)
        wb = wo[A_WIDTH:A_WIDTH + B_WIDTH].astype(jnp.bfloat16)
        wc = wo[A_WIDTH + B_WIDTH:].astype(jnp.bfloat16)
        sink = jnp.stack([a_sink[l, h] for h in A_HEAD_ORDER]).astype(jnp.float32)

        proj = _inproj(h16, _regroup_w_in(w_in[l]), cos_t, sin_t, lb_all[l].reshape(1, C_WIDTH))
        q_all, k_all, p_all, s_all, lf_all, kk_all = [t.reshape(BATCH, LP, t.shape[-1]) for t in proj]

        ya = _mixer_a(sink, q_all, k_all, p_all, s_all)
        yb = _mixer_b(lam_init, lam.reshape(1), q_all, k_all, p_all, s_all, b_subln_w[l])
        yc = _mixer_c(s_all, p_all, lf_all, kk_all, s_all, c_norm_w[l])

        if l == DEPTH - 1:
            return _outproj_last(ya, yb, yc, wa, wb, wc, h32, ln_g[l], ln_b[l])
        h32, h16 = _outproj(ya.reshape(M_ROWS, A_WIDTH), yb.reshape(M_ROWS, B_WIDTH), yc.reshape(M_ROWS, C_WIDTH),
                            wa, wb, wc, h32, ln_g[l], ln_b[l])
```

```python
import functools
import math

import jax
import jax.numpy as jnp
import numpy as np
from jax import lax
from jax.experimental import pallas as pl
from jax.experimental.pallas import tpu as pltpu

D_MODEL = 1024
BATCH = 8
SEQ = 2048
DEPTH = 2
N_META = 16
BLOCK = 128
PAD = BLOCK - N_META
WINDOW = 128
ROPE_THETA = 10000.0
HEAD_DIM = 64
A_HEADS = 4
A_KV_HEADS = 2
A_WIDTH = A_HEADS * HEAD_DIM
B_HEADS = 4
B_V_DIM = 2 * HEAD_DIM
B_WIDTH = B_HEADS * B_V_DIM
C_HEADS = 4
C_WIDTH = C_HEADS * HEAD_DIM
D_MIX = A_WIDTH + B_WIDTH + C_WIDTH
ALPHA = (2 * DEPTH) ** 0.25
LN_EPS = 1e-5
RMS_EPS = 1e-6
NEG = -1e30
F_MIN = 1e-30

LP = SEQ + N_META + PAD
NBLK = LP // BLOCK
M_ROWS = BATCH * LP
TM = LP // 4
LANES = 128
SUBLANES = 8
MXU_N = 256
Q0_B = 96
TQ_B = (LP - Q0_B) // 5
SM_ROWS = 16
VMEM_LIMIT = 48 * 1024 * 1024

N_ROPE_Q = A_WIDTH + 2 * B_HEADS * HEAD_DIM
N_ROPE_K = A_KV_HEADS * HEAD_DIM + 2 * B_HEADS * HEAD_DIM
N_PLAIN = C_WIDTH + A_KV_HEADS * HEAD_DIM + B_WIDTH
N_SILU = A_WIDTH + B_WIDTH + C_WIDTH + C_WIDTH
N_GATE = 2 * C_WIDTH
N_IN = N_ROPE_Q + N_ROPE_K + N_PLAIN + N_SILU + N_GATE
A_HEAD_ORDER = (0, 2, 1, 3)

HGRN_LEVELS = (0, 1, 2, 4, 8, 16, 32, 64)


def _cparams(sem, flags=None):
    return pltpu.CompilerParams(dimension_semantics=sem, vmem_limit_bytes=VMEM_LIMIT, flags=flags)


def _layer_norm(x, g, b):
    mu = jnp.mean(x, axis=-1, keepdims=True)
    xc = x - mu
    var = jnp.mean(xc * xc, axis=-1, keepdims=True)
    return xc * lax.rsqrt(var + LN_EPS) * g + b


def _front_embed_kernel(x_ref, meta_ref, g_ref, b_ref, w_ref, cos_ref, sin_ref, lb_ref,
                        h32_ref, q_ref, k_ref, p_ref, s_ref, lf_ref, kk_ref, h16_ref):
    j = pl.program_id(0) % (LP // TM)

    def put(rows, y):
        h32_ref[rows, :] = y
        h16_ref[rows, :] = y.astype(jnp.bfloat16)

    @pl.when(j == 0)
    def _():
        put(slice(0, PAD), jnp.zeros((PAD, D_MODEL), jnp.float32))
        put(slice(PAD, BLOCK), _layer_norm(meta_ref[...], g_ref[...], b_ref[...]))
        put(slice(BLOCK, TM), _layer_norm(x_ref[0:TM - BLOCK, :], g_ref[...], b_ref[...]))

    @pl.when(j > 0)
    def _():
        put(slice(0, TM), _layer_norm(x_ref[...], g_ref[...], b_ref[...]))

    _inproj_body(h16_ref[...], w_ref, cos_ref, sin_ref, lb_ref, (q_ref, k_ref, p_ref, s_ref, lf_ref, kk_ref))


def _front_outproj_kernel(ya_ref, yb_ref, yc_ref, wa_ref, wb_ref, wc_ref, h_ref, g_ref, b_ref,
                          w_ref, cos_ref, sin_ref, lb_ref, h32_ref, q_ref, k_ref, p_ref, s_ref, lf_ref, kk_ref):
    acc = jnp.dot(ya_ref[...], wa_ref[...], preferred_element_type=jnp.float32)
    acc = acc + jnp.dot(yb_ref[...], wb_ref[...], preferred_element_type=jnp.float32)
    acc = acc + jnp.dot(yc_ref[...], wc_ref[...], preferred_element_type=jnp.float32)
    h = _layer_norm(ALPHA * h_ref[...] + acc, g_ref[...], b_ref[...])
    h32_ref[...] = h
    _inproj_body(h.astype(jnp.bfloat16), w_ref, cos_ref, sin_ref, lb_ref, (q_ref, k_ref, p_ref, s_ref, lf_ref, kk_ref))


def _front(prologue_args, w, cos_t, sin_t, lb):
    row = lambda n: pl.BlockSpec((TM, n), lambda i: (i, 0))
    const = lambda r, n: pl.BlockSpec((r, n), lambda i: (0, 0))
    vec = const(1, D_MODEL)
    tab_spec = pl.BlockSpec((TM, LANES), lambda i: (i % (LP // TM), 0))
    tiles_per_b = LP // TM
    if len(prologue_args) == 4:
        x, meta, g, b = prologue_args
        x_spec = pl.BlockSpec(
            (pl.Element(TM), pl.Element(D_MODEL)),
            lambda i: (pl.multiple_of((i // tiles_per_b) * SEQ + jnp.maximum((i % tiles_per_b) * TM - BLOCK, 0),
                                      SUBLANES), 0))
        body, name = _front_embed_kernel, "front_embed_inproj"
        args = (x.reshape(BATCH * SEQ, D_MODEL), meta, g.reshape(1, D_MODEL), b.reshape(1, D_MODEL))
        in_specs = [x_spec, const(N_META, D_MODEL), vec, vec]
        scratch = [pltpu.VMEM((TM, D_MODEL), jnp.bfloat16)]
    else:
        ya, yb, yc, wa, wb, wc, h32, g, b = prologue_args
        body, name = _front_outproj_kernel, "front_outproj_inproj"
        args = (ya, yb, yc, wa, wb, wc, h32, g.reshape(1, D_MODEL), b.reshape(1, D_MODEL))
        in_specs = [row(A_WIDTH), row(B_WIDTH), row(C_WIDTH), const(A_WIDTH, D_MODEL), const(B_WIDTH, D_MODEL),
                    const(C_WIDTH, D_MODEL), row(D_MODEL), vec, vec]
        scratch = []
    widths = (D_MODEL, N_ROPE_Q, N_ROPE_K, N_PLAIN, N_SILU, N_GATE, N_GATE)
    dtypes = (jnp.float32,) + (jnp.bfloat16,) * 4 + (jnp.float32,) * 2
    return pl.pallas_call(
        body,
        grid=(M_ROWS // TM,),
        in_specs=in_specs + [const(D_MODEL, N_IN), tab_spec, tab_spec, const(1, C_WIDTH)],
        out_specs=[row(n) for n in widths],
        out_shape=[jax.ShapeDtypeStruct((M_ROWS, n), dt) for n, dt in zip(widths, dtypes)],
        scratch_shapes=scratch,
        compiler_params=_cparams(("arbitrary",)),
        name=name,
    )(*args, w, cos_t, sin_t, lb)


def _col_chunks(n):
    return [(c, min(MXU_N, n - c)) for c in range(0, n, MXU_N)]


def _rope_cols(acc, cos, sin_signed, first_half):
    partner = jnp.where(first_half, pltpu.roll(acc, LANES - HEAD_DIM // 2, 1), pltpu.roll(acc, HEAD_DIM // 2, 1))
    return acc * cos + partner * sin_signed


def _inproj_body(x, w_ref, cos_ref, sin_ref, lb_ref, outs):
    q_ref, k_ref, p_ref, s_ref, lf_ref, kk_ref = outs
    cos = cos_ref[...]
    sin = sin_ref[...]
    lane = lax.broadcasted_iota(jnp.int32, (TM, LANES), 1)
    first_half = (lane % HEAD_DIM) < (HEAD_DIM // 2)
    row = (pl.program_id(0) % (LP // TM)) * TM + lax.broadcasted_iota(jnp.int32, (TM, LANES), 0)
    valid = row >= PAD
    pieces = ([("rope_q", q_ref, c) for c in range(0, N_ROPE_Q, LANES)]
              + [("rope_k", k_ref, c) for c in range(0, N_ROPE_K, LANES)]
              + [("plain", p_ref, c) for c in range(0, N_PLAIN, LANES)]
              + [("silu", s_ref, c) for c in range(0, N_SILU, LANES)]
              + [("gate", None, c) for c in range(0, N_GATE, LANES)])
    for c0, cw in _col_chunks(N_IN):
        acc = jnp.dot(x, w_ref[:, c0:c0 + cw], preferred_element_type=jnp.float32)
        for j in range(cw // LANES):
            kind, dst, dc = pieces[c0 // LANES + j]
            a = acc[:, j * LANES:(j + 1) * LANES]
            if kind == "rope_q":
                dst[:, dc:dc + LANES] = (_rope_cols(a, cos, sin, first_half) * (HEAD_DIM ** -0.5)).astype(dst.dtype)
            elif kind == "rope_k":
                dst[:, dc:dc + LANES] = _rope_cols(a, cos, sin, first_half).astype(dst.dtype)
            elif kind == "plain":
                dst[:, dc:dc + LANES] = a.astype(dst.dtype)
            elif kind == "silu":
                dst[:, dc:dc + LANES] = (a * jax.nn.sigmoid(a)).astype(dst.dtype)
            else:
                lb = lb_ref[:, dc % C_WIDTH: dc % C_WIDTH + LANES]
                f = lb + (1.0 - lb) * jax.nn.sigmoid(a)
                lf_ref[:, dc:dc + LANES] = jnp.where(valid, jnp.log(jnp.maximum(f, F_MIN)), 0.0)
                kk_ref[:, dc:dc + LANES] = jnp.where(valid, (1.0 - lb) * jax.nn.sigmoid(-a), 0.0)


def _mixer_a_kernel(sink_ref, q_ref, k_ref, v_ref, g_ref, o_ref):
    two = 2 * BLOCK
    row = lax.broadcasted_iota(jnp.int32, (two, 4 * BLOCK), 0) % BLOCK
    col = lax.broadcasted_iota(jnp.int32, (two, 4 * BLOCK), 1)
    is_meta = (col < BLOCK) & (col >= PAD)
    top = lax.broadcasted_iota(jnp.int32, (two, 1), 0) < BLOCK
    lane_hi = lax.broadcasted_iota(jnp.int32, (BLOCK, LANES), 1) >= HEAD_DIM
    masks = {}
    for n in range(NBLK):
        start = min(max((n - 1) * BLOCK, 0), LP - 3 * BLOCK)
        key = (n * BLOCK - start, start < BLOCK)
        if key not in masks:
            kpos = start + col - BLOCK
            band = (col >= BLOCK) & (jnp.abs(n * BLOCK + row - kpos) <= WINDOW)
            if start < BLOCK:
                band = band & (kpos >= BLOCK)
            masks[key] = is_meta | band
        ok = masks[key]
        kcat = jnp.concatenate([k_ref[0, 0:BLOCK, :], k_ref[0, start:start + 3 * BLOCK, :]], axis=0)
        vcat = jnp.concatenate([v_ref[0, 0:BLOCK, :], v_ref[0, start:start + 3 * BLOCK, :]], axis=0)
        rows = slice(n * BLOCK, (n + 1) * BLOCK)
        qcols = [q_ref[0, rows, cv * LANES:(cv + 1) * LANES] for cv in range(2)]
        outs = []
        for hf in range(2):
            sel = lane_hi if hf else jnp.logical_not(lane_hi)
            q2 = jnp.concatenate([jnp.where(sel, qc, jnp.zeros_like(qc)) for qc in qcols], axis=0)
            sink = jnp.where(top, sink_ref[hf], sink_ref[2 + hf])
            s = lax.dot_general(q2, kcat, (((1,), (1,)), ((), ())), preferred_element_type=jnp.float32)
            s = jnp.where(ok, s, NEG)
            m = jnp.maximum(jnp.max(s, axis=-1, keepdims=True), sink)
            e = jnp.exp(s - m)
            den = jnp.sum(e, axis=-1, keepdims=True) + jnp.exp(sink - m)
            p = (e * (1.0 / den)).astype(jnp.bfloat16)
            outs.append(jnp.dot(p, vcat, preferred_element_type=jnp.float32))
        for cv in range(2):
            o = jnp.where(lane_hi, outs[1][cv * BLOCK:(cv + 1) * BLOCK], outs[0][cv * BLOCK:(cv + 1) * BLOCK])
            y = o * g_ref[0, rows, cv * LANES:(cv + 1) * LANES].astype(jnp.float32)
            o_ref[0, rows, cv * LANES:(cv + 1) * LANES] = y.astype(o_ref.dtype)


def _mixer_a(sink, q3, k3, v3, g3):
    return pl.pallas_call(
        _mixer_a_kernel,
        grid=(BATCH,),
        in_specs=[
            pl.BlockSpec(memory_space=pltpu.SMEM),
            pl.BlockSpec((1, LP, A_WIDTH), lambda b: (b, 0, 0)),
            pl.BlockSpec((1, LP, LANES), lambda b: (b, 0, 0)),
            pl.BlockSpec((1, LP, LANES), lambda b: (b, 0, C_WIDTH // LANES)),
            pl.BlockSpec((1, LP, A_WIDTH), lambda b: (b, 0, 0)),
        ],
        out_specs=pl.BlockSpec((1, LP, A_WIDTH), lambda b: (b, 0, 0)),
        out_shape=jax.ShapeDtypeStruct((BATCH, LP, A_WIDTH), jnp.bfloat16),
        compiler_params=_cparams(("arbitrary",)),
        name="mixer_a_window_gqa",
    )(sink, q3, k3, v3, g3)


def _mixer_b_kernel(lam_init, lam_ref, q_ref, k_ref, v_ref, g_ref, w_ref, o_ref, s_a, s_b, a_a, a_b):
    lam = lam_ref[0]
    lane_hi = lax.broadcasted_iota(jnp.int32, (TQ_B, LANES), 1) >= HEAD_DIM
    key_ok = lax.broadcasted_iota(jnp.int32, (TQ_B, BLOCK), 1) >= PAD
    w = w_ref[...] * (1.0 - lam_init)
    s_bufs, a_bufs = (s_a, s_b), (a_a, a_b)
    nt = (LP - Q0_B) // TQ_B
    nt_dims = (((1,), (1,)), ((), ()))
    tile_rows = lambda t: slice(Q0_B + t * TQ_B, Q0_B + (t + 1) * TQ_B)
    o_ref[0, 0:Q0_B, :] = jnp.zeros((Q0_B, LANES), o_ref.dtype)

    def scores(t):
        s_buf = s_bufs[t % 2]
        q = q_ref[0, tile_rows(t), :]
        for c in range(2):
            qm = jnp.where(lane_hi if c else jnp.logical_not(lane_hi), q, jnp.zeros_like(q))
            s0 = lax.dot_general(qm, k_ref[0, 0:BLOCK, :], nt_dims, preferred_element_type=jnp.float32)
            s_buf[c, :, 0:BLOCK] = jnp.where(key_ok, s0, NEG)
            s_buf[c, :, BLOCK:LP] = lax.dot_general(qm, k_ref[0, BLOCK:LP, :], nt_dims,
                                                    preferred_element_type=jnp.float32)

    def softmax_pv(t):
        s_buf, a_buf = s_bufs[t % 2], a_bufs[t % 2]
        inv_den0 = []
        for r in range(TQ_B // SM_ROWS):
            rows = slice(r * SM_ROWS, (r + 1) * SM_ROWS)
            es, dens = [], []
            for c in range(2):
                s = s_buf[c, rows, :]
                m = jnp.max(s, axis=-1, keepdims=True)
                e = jnp.exp(s - m)
                es.append(e)
                dens.append(jnp.sum(e, axis=-1, keepdims=True))
            a_buf[rows, :] = (es[0] - es[1] * (dens[0] * lam / dens[1])).astype(jnp.bfloat16)
            inv_den0.append(1.0 / dens[0])
        o = jnp.dot(a_buf[...], v_ref[0], preferred_element_type=jnp.float32)
        o = o * jnp.concatenate(inv_den0, axis=0)
        ms = jnp.mean(o * o, axis=-1, keepdims=True)
        y = o * lax.rsqrt(ms + RMS_EPS) * w
        y = y * g_ref[0, tile_rows(t), :].astype(jnp.float32)
        o_ref[0, tile_rows(t), :] = y.astype(o_ref.dtype)

    for t in range(nt + 1):
        if t < nt:
            scores(t)
        if t >= 1:
            softmax_pv(t - 1)


def _mixer_b(lam_init, lam, q3, k3, v3, g3, subln_w):
    qoff = A_WIDTH // LANES
    koff = (A_KV_HEADS * HEAD_DIM) // LANES
    voff = (C_WIDTH + A_KV_HEADS * HEAD_DIM) // LANES
    goff = A_WIDTH // LANES
    return pl.pallas_call(
        functools.partial(_mixer_b_kernel, lam_init),
        grid=(BATCH, B_HEADS),
        in_specs=[
            pl.BlockSpec(memory_space=pltpu.SMEM),
            pl.BlockSpec((1, LP, LANES), lambda b, h: (b, 0, qoff + h)),
            pl.BlockSpec((1, LP, LANES), lambda b, h: (b, 0, koff + h)),
            pl.BlockSpec((1, LP, LANES), lambda b, h: (b, 0, voff + h)),
            pl.BlockSpec((1, LP, LANES), lambda b, h: (b, 0, goff + h)),
            pl.BlockSpec((1, B_V_DIM), lambda b, h: (0, h)),
        ],
        out_specs=pl.BlockSpec((1, LP, LANES), lambda b, h: (b, 0, h)),
        out_shape=jax.ShapeDtypeStruct((BATCH, LP, B_WIDTH), jnp.bfloat16),
        scratch_shapes=[pltpu.VMEM((2, TQ_B, LP), jnp.float32), pltpu.VMEM((2, TQ_B, LP), jnp.float32),
                        pltpu.VMEM((TQ_B, LP), jnp.bfloat16), pltpu.VMEM((TQ_B, LP), jnp.bfloat16)],
        compiler_params=_cparams(("arbitrary", "arbitrary")),
        name="mixer_b_diff_attn",
    )(lam, q3, k3, v3, g3, subln_w.reshape(1, B_WIDTH))


def _split3(x):
    hi = x.astype(jnp.bfloat16)
    r1 = x - hi.astype(jnp.float32)
    mid = r1.astype(jnp.bfloat16)
    lo = (r1 - mid.astype(jnp.float32)).astype(jnp.bfloat16)
    return hi, mid, lo


def _hgrn_anchor(b, m, reverse):
    a_off = m if reverse else m - 1
    if 2 * m >= SUBLANES:
        b3 = b.reshape(BLOCK // (2 * m), 2 * m, C_WIDTH)
        return jnp.broadcast_to(b3[:, a_off:a_off + 1, :], b3.shape).reshape(BLOCK, C_WIDTH)
    b3 = b.reshape(BLOCK // SUBLANES, SUBLANES, C_WIDTH)
    sub = lax.broadcasted_iota(jnp.int32, b3.shape, 1)
    anchor = None
    for blk in range(SUBLANES // (2 * m)):
        r = blk * 2 * m + a_off
        cand = jnp.broadcast_to(b3[:, r:r + 1, :], b3.shape)
        anchor = cand if anchor is None else jnp.where(sub >= blk * 2 * m, cand, anchor)
    return anchor.reshape(BLOCK, C_WIDTH)


def _hgrn_chunk(q, v16, kk, lf, state_t, reverse, cst):
    tri16, head16, pair_key, tri_ok, same_head = cst
    nt_dims = (((1,), (1,)), ((), ()))
    hi, mid, lo = _split3(lf)
    b = (jnp.dot(tri16, hi, preferred_element_type=jnp.float32)
         + jnp.dot(tri16, mid, preferred_element_type=jnp.float32)
         + jnp.dot(tri16, lo, preferred_element_type=jnp.float32))
    p = None
    for m in reversed(HGRN_LEVELS):
        if m == 0:
            qt16, kt16 = q.astype(jnp.bfloat16), kk.astype(jnp.bfloat16)
        else:
            wgt = jnp.exp(-jnp.abs(b - _hgrn_anchor(b, m, reverse)))
            qt16, kt16 = (q * wgt).astype(jnp.bfloat16), (kk * wgt).astype(jnp.bfloat16)
        k_heads = jnp.concatenate([kt16 * head16[h] for h in range(C_HEADS)], axis=0)
        pm = lax.dot_general(qt16, k_heads, nt_dims, preferred_element_type=jnp.float32)
        p = pm if p is None else jnp.where(pair_key < max(2 * m, 1), pm, p)
    p = jnp.where(tri_ok, p, 0.0)
    v_heads = jnp.concatenate([v16 * head16[h] for h in range(C_HEADS)], axis=0)
    o = jnp.dot(p.astype(jnp.bfloat16), v_heads, preferred_element_type=jnp.float32)
    edge = 0 if reverse else BLOCK - 1
    b_all = b[edge:edge + 1, :]
    q_in = (q * jnp.exp(b)).astype(jnp.bfloat16)
    o = o + lax.dot_general(q_in, state_t.astype(jnp.bfloat16), nt_dims, preferred_element_type=jnp.float32)
    k_out = (kk * jnp.exp(b_all - b)).astype(jnp.bfloat16)
    u_t = lax.dot_general(v16, k_out, (((0,), (0,)), ((), ())), preferred_element_type=jnp.float32)
    new_state = jnp.exp(b_all) * state_t + jnp.where(same_head, u_t, 0.0)
    return o, new_state


def _mixer_c_kernel(q_ref, v_ref, lff_ref, lfb_ref, kkf_ref, kkb_ref, g_ref, w_ref, o_ref, accf_ref, accr_ref):
    lane_head = lax.broadcasted_iota(jnp.int32, (BLOCK, C_WIDTH), 1) // HEAD_DIM
    head16 = [(lane_head == h).astype(jnp.bfloat16) for h in range(C_HEADS)]
    sr = lax.broadcasted_iota(jnp.int32, (C_WIDTH, C_WIDTH), 0) // HEAD_DIM
    sc = lax.broadcasted_iota(jnp.int32, (C_WIDTH, C_WIDTH), 1) // HEAD_DIM
    same_head = sr == sc
    head_ones16 = same_head.astype(jnp.bfloat16)
    t_i = lax.broadcasted_iota(jnp.int32, (BLOCK, C_HEADS * BLOCK), 0)
    s_i = lax.broadcasted_iota(jnp.int32, (BLOCK, C_HEADS * BLOCK), 1) % BLOCK
    pair_key = t_i ^ s_i
    r_i = lax.broadcasted_iota(jnp.int32, (BLOCK, BLOCK), 0)
    c_i = lax.broadcasted_iota(jnp.int32, (BLOCK, BLOCK), 1)
    cst_f = ((c_i <= r_i).astype(jnp.bfloat16), head16, pair_key, s_i <= t_i, same_head)
    cst_r = ((c_i >= r_i).astype(jnp.bfloat16), head16, pair_key, s_i >= t_i, same_head)

    def body(i, states):
        state_f, state_r = states
        outs = []
        for reverse, lf_ref, kk_ref, cst, state in ((False, lff_ref, kkf_ref, cst_f, state_f),
                                                     (True, lfb_ref, kkb_ref, cst_r, state_r)):
            n = (NBLK - 1 - i) if reverse else i
            r0 = pl.multiple_of(n * BLOCK, BLOCK)
            q = q_ref[0, pl.ds(r0, BLOCK), :].astype(jnp.float32)
            o, new_state = _hgrn_chunk(q, v_ref[0, pl.ds(r0, BLOCK), :], kk_ref[0, pl.ds(r0, BLOCK), :],
                                       lf_ref[0, pl.ds(r0, BLOCK), :], state, reverse, cst)
            (accr_ref if reverse else accf_ref)[pl.ds(r0, BLOCK), :] = o
            outs.append(new_state)
        return tuple(outs)

    zero_state = jnp.zeros((C_WIDTH, C_WIDTH), jnp.float32)
    lax.fori_loop(0, NBLK, body, (zero_state, zero_state), unroll=2)

    w = w_ref[...]

    def finish(n, carry):
        r0 = pl.multiple_of(n * BLOCK, BLOCK)
        y = accf_ref[pl.ds(r0, BLOCK), :] + accr_ref[pl.ds(r0, BLOCK), :]
        sq = y * y
        sq_hi = sq.astype(jnp.bfloat16)
        sq_lo = (sq - sq_hi.astype(jnp.float32)).astype(jnp.bfloat16)
        ms = (jnp.dot(sq_hi, head_ones16, preferred_element_type=jnp.float32)
              + jnp.dot(sq_lo, head_ones16, preferred_element_type=jnp.float32)) * (1.0 / HEAD_DIM)
        y = y * lax.rsqrt(ms + RMS_EPS) * w
        y = y * g_ref[0, pl.ds(r0, BLOCK), :].astype(jnp.float32)
        o_ref[0, pl.ds(r0, BLOCK), :] = y.astype(o_ref.dtype)
        return carry

    lax.fori_loop(0, NBLK, finish, 0)


def _mixer_c(q3, v3, lf3, kk3, g3, norm_w):
    blk = lambda j: pl.BlockSpec((1, LP, C_WIDTH), lambda b: (b, 0, j))
    return pl.pallas_call(
        _mixer_c_kernel,
        grid=(BATCH,),
        in_specs=[
            blk(4),
            blk(0),
            blk(0), blk(1),
            blk(0), blk(1),
            blk(3),
            pl.BlockSpec((1, C_WIDTH), lambda b: (0, 0)),
        ],
        out_specs=pl.BlockSpec((1, LP, C_WIDTH), lambda b: (b, 0, 0)),
        out_shape=jax.ShapeDtypeStruct((BATCH, LP, C_WIDTH), jnp.bfloat16),
        scratch_shapes=[pltpu.VMEM((LP, C_WIDTH), jnp.float32), pltpu.VMEM((LP, C_WIDTH), jnp.float32)],
        compiler_params=_cparams(("arbitrary",)),
        name="mixer_c_hgrn2",
    )(q3, v3, lf3, lf3, kk3, kk3, g3, norm_w.reshape(1, C_WIDTH))


def _outproj_last_kernel(ya_ref, yb_ref, yc_ref, wa_ref, wb_ref, wc_ref, h_ref, g_ref, b_ref, o_ref):
    acc = jnp.dot(ya_ref[...], wa_ref[...], preferred_element_type=jnp.float32)
    acc = acc + jnp.dot(yb_ref[...], wb_ref[...], preferred_element_type=jnp.float32)
    acc = acc + jnp.dot(yc_ref[...], wc_ref[...], preferred_element_type=jnp.float32)
    o_ref[0] = _layer_norm(ALPHA * h_ref[...] + acc, g_ref[...], b_ref[...])


def _outproj_last(ya, yb, yc, wa, wb, wc, h32, g, b):
    tl = SEQ // 4
    row = lambda n: pl.BlockSpec((pl.Element(tl), pl.Element(n)),
                                 lambda bi, j: (pl.multiple_of(bi * LP + BLOCK + j * tl, BLOCK), 0))
    full = lambda n: pl.BlockSpec((n, D_MODEL), lambda bi, j: (0, 0))
    vec = pl.BlockSpec((1, D_MODEL), lambda bi, j: (0, 0))
    return pl.pallas_call(
        _outproj_last_kernel,
        grid=(BATCH, SEQ // tl),
        in_specs=[row(A_WIDTH), row(B_WIDTH), row(C_WIDTH), full(A_WIDTH), full(B_WIDTH), full(C_WIDTH),
                  row(D_MODEL), vec, vec],
        out_specs=pl.BlockSpec((1, tl, D_MODEL), lambda bi, j: (bi, j, 0)),
        out_shape=jax.ShapeDtypeStruct((BATCH, SEQ, D_MODEL), jnp.float32),
        compiler_params=_cparams(("arbitrary", "arbitrary")),
        name="outproj_last_residual_ln",
    )(ya.reshape(M_ROWS, A_WIDTH), yb.reshape(M_ROWS, B_WIDTH), yc.reshape(M_ROWS, C_WIDTH), wa, wb, wc, h32,
      g.reshape(1, D_MODEL), b.reshape(1, D_MODEL))


def _regroup_w_in(w):
    def heads(lo, order):
        return [w[:, lo + h * HEAD_DIM: lo + (h + 1) * HEAD_DIM] for h in order]

    c = lambda lo, n: w[:, lo:lo + n]
    o_qa, o_ka, o_va, o_ga = 0, 256, 384, 512
    o_qb, o_kb, o_vb, o_gb = 768, 1280, 1792, 2304
    o_qc, o_ic, o_zf, o_zb, o_gc = 2816, 3072, 3328, 3584, 3840
    cat = lambda xs: jnp.concatenate(xs, axis=1).astype(jnp.bfloat16)
    w_q = cat(heads(o_qa, A_HEAD_ORDER) + [c(o_qb, 512)])
    w_k = cat([c(o_ka, 128), c(o_kb, 512)])
    w_p = cat([c(o_ic, 256), c(o_va, 128), c(o_vb, 512)])
    w_s = cat(heads(o_ga, A_HEAD_ORDER) + [c(o_gb, 512), c(o_gc, 256), c(o_qc, 256)])
    w_g = cat([c(o_zf, 256), c(o_zb, 256)])
    return jnp.concatenate([w_q, w_k, w_p, w_s, w_g], axis=1)


def _rope_tables():
    pos = (jnp.arange(LP) - PAD).astype(jnp.float32)
    inv = ROPE_THETA ** (-jnp.arange(0, HEAD_DIM, 2, dtype=jnp.float32) / HEAD_DIM)
    ang = pos[:, None] * inv[None, :]
    cos, sin = jnp.cos(ang), jnp.sin(ang)
    cos_t = jnp.concatenate([cos, cos, cos, cos], axis=1)
    sin_t = jnp.concatenate([-sin, sin, -sin, sin], axis=1)
    return cos_t, sin_t


def kernel(x, meta, emb_ln_g, emb_ln_b, w_in, w_out, a_sink, b_lam, b_subln_w, c_lb_logits, c_norm_w, ln_g, ln_b):
    cos_t, sin_t = _rope_tables()
    p_lb = jax.nn.softmax(c_lb_logits.astype(jnp.float32), axis=0)
    lb_all = jnp.cumsum(p_lb, axis=0) - p_lb[0:1]

    front_args = (x, meta, emb_ln_g, emb_ln_b)
    for l in range(DEPTH):
        lam_init = 0.8 - 0.6 * math.exp(-0.3 * l)
        lp32 = b_lam[l].astype(jnp.float32)
        lam = jnp.exp(jnp.sum(lp32[0] * lp32[1])) - jnp.exp(jnp.sum(lp32[2] * lp32[3])) + lam_init
        wo = w_out[l]
        wa = jnp.concatenate([wo[h * HEAD_DIM:(h + 1) * HEAD_DIM] for h in A_HEAD_ORDER], axis=0).astype(jnp.bfloat16)
        wb = wo[A_WIDTH:A_WIDTH + B_WIDTH].astype(jnp.bfloat16)
        wc = wo[A_WIDTH + B_WIDTH:].astype(jnp.bfloat16)
        sink = jnp.stack([a_sink[l, h] for h in A_HEAD_ORDER]).astype(jnp.float32)

        h32, *proj = _front(front_args, _regroup_w_in(w_in[l]), cos_t, sin_t, lb_all[l].reshape(1, C_WIDTH))
        q_all, k_all, p_all, s_all, lf_all, kk_all = [t.reshape(BATCH, LP, t.shape[-1]) for t in proj]

        ya = _mixer_a(sink, q_all, k_all, p_all, s_all)
        yb = _mixer_b(lam_init, lam.reshape(1), q_all, k_all, p_all, s_all, b_subln_w[l])
        yc = _mixer_c(s_all, p_all, lf_all, kk_all, s_all, c_norm_w[l])

        if l == DEPTH - 1:
            return _outproj_last(ya, yb, yc, wa, wb, wc, h32, ln_g[l], ln_b[l])
        front_args = (ya.reshape(M_ROWS, A_WIDTH), yb.reshape(M_ROWS, B_WIDTH), yc.reshape(M_ROWS, C_WIDTH),
                      wa, wb, wc, h32, ln_g[l], ln_b[l])
```

```python
import functools
import math

import jax
import jax.numpy as jnp
import numpy as np
from jax import lax
from jax.experimental import pallas as pl
from jax.experimental.pallas import tpu as pltpu

D_MODEL = 1024
BATCH = 8
SEQ = 2048
DEPTH = 2
N_META = 16
BLOCK = 128
PAD = BLOCK - N_META
WINDOW = 128
ROPE_THETA = 10000.0
HEAD_DIM = 64
A_HEADS = 4
A_KV_HEADS = 2
A_WIDTH = A_HEADS * HEAD_DIM
B_HEADS = 4
B_V_DIM = 2 * HEAD_DIM
B_WIDTH = B_HEADS * B_V_DIM
C_HEADS = 4
C_WIDTH = C_HEADS * HEAD_DIM
D_MIX = A_WIDTH + B_WIDTH + C_WIDTH
ALPHA = (2 * DEPTH) ** 0.25
LN_EPS = 1e-5
RMS_EPS = 1e-6
NEG = -1e30
F_MIN = 1e-30

LP = SEQ + N_META + PAD
NBLK = LP // BLOCK
M_ROWS = BATCH * LP
TM = LP // 4
LANES = 128
SUBLANES = 8
MXU_N = 256
Q0_B = 96
TQ_B = (LP - Q0_B) // 5
SM_ROWS = 16
VMEM_LIMIT = 48 * 1024 * 1024

N_ROPE_Q = A_WIDTH + 2 * B_HEADS * HEAD_DIM
N_ROPE_K = A_KV_HEADS * HEAD_DIM + 2 * B_HEADS * HEAD_DIM
N_PLAIN = C_WIDTH + A_KV_HEADS * HEAD_DIM + B_WIDTH
N_SILU = A_WIDTH + B_WIDTH + C_WIDTH + C_WIDTH
N_GATE = 2 * C_WIDTH
N_IN = N_ROPE_Q + N_ROPE_K + N_PLAIN + N_SILU + N_GATE
A_HEAD_ORDER = (0, 2, 1, 3)

HGRN_LEVELS = (0, 1, 2, 4, 8, 16, 32, 64)
HGRN_STEPS_PER_ITER = 2


def _cparams(sem, flags=None):
    return pltpu.CompilerParams(dimension_semantics=sem, vmem_limit_bytes=VMEM_LIMIT, flags=flags)


def _layer_norm(x, g, b):
    mu = jnp.mean(x, axis=-1, keepdims=True)
    xc = x - mu
    var = jnp.mean(xc * xc, axis=-1, keepdims=True)
    return xc * lax.rsqrt(var + LN_EPS) * g + b


def _front_embed_kernel(x_ref, meta_ref, g_ref, b_ref, w_ref, cos_ref, sin_ref, lb_ref,
                        h32_ref, q_ref, k_ref, p_ref, s_ref, lf_ref, kk_ref, h16_ref):
    j = pl.program_id(0) % (LP // TM)

    def put(rows, y):
        h32_ref[rows, :] = y
        h16_ref[rows, :] = y.astype(jnp.bfloat16)

    @pl.when(j == 0)
    def _():
        put(slice(0, PAD), jnp.zeros((PAD, D_MODEL), jnp.float32))
        put(slice(PAD, BLOCK), _layer_norm(meta_ref[...], g_ref[...], b_ref[...]))
        put(slice(BLOCK, TM), _layer_norm(x_ref[0:TM - BLOCK, :], g_ref[...], b_ref[...]))

    @pl.when(j > 0)
    def _():
        put(slice(0, TM), _layer_norm(x_ref[...], g_ref[...], b_ref[...]))

    _inproj_body(h16_ref[...], slice(0, TM), w_ref, cos_ref, sin_ref, lb_ref,
                 (q_ref, k_ref, p_ref, s_ref, lf_ref, kk_ref))


def _front_outproj_kernel(ya_ref, yb_ref, yc_ref, wa_ref, wb_ref, wc_ref, h_ref, g_ref, b_ref,
                          w_ref, cos_ref, sin_ref, lb_ref, h32_ref, q_ref, k_ref, p_ref, s_ref, lf_ref, kk_ref):
    halves = (slice(0, TM // 2), slice(TM // 2, TM))
    accs = []
    for rows in halves:
        acc = jnp.dot(ya_ref[rows, :], wa_ref[...], preferred_element_type=jnp.float32)
        acc = acc + jnp.dot(yb_ref[rows, :], wb_ref[...], preferred_element_type=jnp.float32)
        accs.append(acc + jnp.dot(yc_ref[rows, :], wc_ref[...], preferred_element_type=jnp.float32))
    for rows, acc in zip(halves, accs):
        h = _layer_norm(ALPHA * h_ref[rows, :] + acc, g_ref[...], b_ref[...])
        h32_ref[rows, :] = h
        _inproj_body(h.astype(jnp.bfloat16), rows, w_ref, cos_ref, sin_ref, lb_ref,
                     (q_ref, k_ref, p_ref, s_ref, lf_ref, kk_ref))


def _front(prologue_args, w, cos_t, sin_t, lb):
    row = lambda n: pl.BlockSpec((TM, n), lambda i: (i, 0))
    const = lambda r, n: pl.BlockSpec((r, n), lambda i: (0, 0))
    vec = const(1, D_MODEL)
    tab_spec = pl.BlockSpec((TM, LANES), lambda i: (i % (LP // TM), 0))
    tiles_per_b = LP // TM
    if len(prologue_args) == 4:
        x, meta, g, b = prologue_args
        x_spec = pl.BlockSpec(
            (pl.Element(TM), pl.Element(D_MODEL)),
            lambda i: (pl.multiple_of((i // tiles_per_b) * SEQ + jnp.maximum((i % tiles_per_b) * TM - BLOCK, 0),
                                      SUBLANES), 0))
        body, name = _front_embed_kernel, "front_embed_inproj"
        args = (x.reshape(BATCH * SEQ, D_MODEL), meta, g.reshape(1, D_MODEL), b.reshape(1, D_MODEL))
        in_specs = [x_spec, const(N_META, D_MODEL), vec, vec]
        scratch = [pltpu.VMEM((TM, D_MODEL), jnp.bfloat16)]
    else:
        ya, yb, yc, wa, wb, wc, h32, g, b = prologue_args
        body, name = _front_outproj_kernel, "front_outproj_inproj"
        args = (ya, yb, yc, wa, wb, wc, h32, g.reshape(1, D_MODEL), b.reshape(1, D_MODEL))
        in_specs = [row(A_WIDTH), row(B_WIDTH), row(C_WIDTH), const(A_WIDTH, D_MODEL), const(B_WIDTH, D_MODEL),
                    const(C_WIDTH, D_MODEL), row(D_MODEL), vec, vec]
        scratch = []
    widths = (D_MODEL, N_ROPE_Q, N_ROPE_K, N_PLAIN, N_SILU, N_GATE, N_GATE)
    dtypes = (jnp.float32,) + (jnp.bfloat16,) * 4 + (jnp.float32,) * 2
    return pl.pallas_call(
        body,
        grid=(M_ROWS // TM,),
        in_specs=in_specs + [const(D_MODEL, N_IN), tab_spec, tab_spec, const(1, C_WIDTH)],
        out_specs=[row(n) for n in widths],
        out_shape=[jax.ShapeDtypeStruct((M_ROWS, n), dt) for n, dt in zip(widths, dtypes)],
        scratch_shapes=scratch,
        compiler_params=_cparams(("arbitrary",)),
        name=name,
    )(*args, w, cos_t, sin_t, lb)


def _col_chunks(n):
    return [(c, min(MXU_N, n - c)) for c in range(0, n, MXU_N)]


def _rope_cols(acc, cos, sin_signed, first_half):
    partner = jnp.where(first_half, pltpu.roll(acc, LANES - HEAD_DIM // 2, 1), pltpu.roll(acc, HEAD_DIM // 2, 1))
    return acc * cos + partner * sin_signed


def _inproj_body(x, rows, w_ref, cos_ref, sin_ref, lb_ref, outs):
    q_ref, k_ref, p_ref, s_ref, lf_ref, kk_ref = [o.at[rows, :] for o in outs]
    n_rows = rows.stop - rows.start
    cos = cos_ref[rows, :]
    sin = sin_ref[rows, :]
    lane = lax.broadcasted_iota(jnp.int32, (n_rows, LANES), 1)
    first_half = (lane % HEAD_DIM) < (HEAD_DIM // 2)
    row = ((pl.program_id(0) % (LP // TM)) * TM + rows.start
           + lax.broadcasted_iota(jnp.int32, (n_rows, LANES), 0))
    valid = row >= PAD
    pieces = ([("rope_q", q_ref, c) for c in range(0, N_ROPE_Q, LANES)]
              + [("rope_k", k_ref, c) for c in range(0, N_ROPE_K, LANES)]
              + [("plain", p_ref, c) for c in range(0, N_PLAIN, LANES)]
              + [("silu", s_ref, c) for c in range(0, N_SILU, LANES)]
              + [("gate", None, c) for c in range(0, N_GATE, LANES)])
    for c0, cw in _col_chunks(N_IN):
        acc = jnp.dot(x, w_ref[:, c0:c0 + cw], preferred_element_type=jnp.float32)
        for j in range(cw // LANES):
            kind, dst, dc = pieces[c0 // LANES + j]
            a = acc[:, j * LANES:(j + 1) * LANES]
            if kind == "rope_q":
                dst[:, dc:dc + LANES] = (_rope_cols(a, cos, sin, first_half) * (HEAD_DIM ** -0.5)).astype(dst.dtype)
            elif kind == "rope_k":
                dst[:, dc:dc + LANES] = _rope_cols(a, cos, sin, first_half).astype(dst.dtype)
            elif kind == "plain":
                dst[:, dc:dc + LANES] = a.astype(dst.dtype)
            elif kind == "silu":
                dst[:, dc:dc + LANES] = (a * jax.nn.sigmoid(a)).astype(dst.dtype)
            else:
                lb = lb_ref[:, dc % C_WIDTH: dc % C_WIDTH + LANES]
                f = lb + (1.0 - lb) * jax.nn.sigmoid(a)
                lf_ref[:, dc:dc + LANES] = jnp.where(valid, jnp.log2(jnp.maximum(f, F_MIN)), 0.0)
                kk_ref[:, dc:dc + LANES] = jnp.where(valid, (1.0 - lb) * jax.nn.sigmoid(-a), 0.0)


def _mixer_a_kernel(sink_ref, q_ref, k_ref, v_ref, g_ref, o_ref):
    two = 2 * BLOCK
    row = lax.broadcasted_iota(jnp.int32, (two, 4 * BLOCK), 0) % BLOCK
    col = lax.broadcasted_iota(jnp.int32, (two, 4 * BLOCK), 1)
    is_meta = (col < BLOCK) & (col >= PAD)
    top = lax.broadcasted_iota(jnp.int32, (two, 1), 0) < BLOCK
    lane_hi = lax.broadcasted_iota(jnp.int32, (BLOCK, LANES), 1) >= HEAD_DIM
    masks = {}

    def band_start(n):
        return min(max((n - 1) * BLOCK, 0), LP - 3 * BLOCK)

    def scores(n):
        start = band_start(n)
        key = (n * BLOCK - start, start < BLOCK)
        if key not in masks:
            kpos = start + col - BLOCK
            band = (col >= BLOCK) & (jnp.abs(n * BLOCK + row - kpos) <= WINDOW)
            if start < BLOCK:
                band = band & (kpos >= BLOCK)
            masks[key] = is_meta | band
        kcat = jnp.concatenate([k_ref[0, 0:BLOCK, :], k_ref[0, start:start + 3 * BLOCK, :]], axis=0)
        qcols = [q_ref[0, n * BLOCK:(n + 1) * BLOCK, cv * LANES:(cv + 1) * LANES] for cv in range(2)]
        out = []
        for hf in range(2):
            sel = lane_hi if hf else jnp.logical_not(lane_hi)
            q2 = jnp.concatenate([jnp.where(sel, qc, jnp.zeros_like(qc)) for qc in qcols], axis=0)
            s = lax.dot_general(q2, kcat, (((1,), (1,)), ((), ())), preferred_element_type=jnp.float32)
            out.append(jnp.where(masks[key], s, NEG))
        return out

    def softmax_pv(n, s_halves):
        start = band_start(n)
        vcat = jnp.concatenate([v_ref[0, 0:BLOCK, :], v_ref[0, start:start + 3 * BLOCK, :]], axis=0)
        rows = slice(n * BLOCK, (n + 1) * BLOCK)
        outs = []
        for hf, s in enumerate(s_halves):
            sink = jnp.where(top, sink_ref[hf], sink_ref[2 + hf])
            m = jnp.maximum(jnp.max(s, axis=-1, keepdims=True), sink)
            e = jnp.exp(s - m)
            den = jnp.sum(e, axis=-1, keepdims=True) + jnp.exp(sink - m)
            p = (e * (1.0 / den)).astype(jnp.bfloat16)
            outs.append(jnp.dot(p, vcat, preferred_element_type=jnp.float32))
        for cv in range(2):
            o = jnp.where(lane_hi, outs[1][cv * BLOCK:(cv + 1) * BLOCK], outs[0][cv * BLOCK:(cv + 1) * BLOCK])
            y = o * g_ref[0, rows, cv * LANES:(cv + 1) * LANES].astype(jnp.float32)
            o_ref[0, rows, cv * LANES:(cv + 1) * LANES] = y.astype(o_ref.dtype)

    ahead = 2
    pending = {}
    for n in range(NBLK + ahead):
        if n < NBLK:
            pending[n] = scores(n)
        if n >= ahead:
            softmax_pv(n - ahead, pending.pop(n - ahead))


def _mixer_a(sink, q3, k3, v3, g3):
    return pl.pallas_call(
        _mixer_a_kernel,
        grid=(BATCH,),
        in_specs=[
            pl.BlockSpec(memory_space=pltpu.SMEM),
            pl.BlockSpec((1, LP, A_WIDTH), lambda b: (b, 0, 0)),
            pl.BlockSpec((1, LP, LANES), lambda b: (b, 0, 0)),
            pl.BlockSpec((1, LP, LANES), lambda b: (b, 0, C_WIDTH // LANES)),
            pl.BlockSpec((1, LP, A_WIDTH), lambda b: (b, 0, 0)),
        ],
        out_specs=pl.BlockSpec((1, LP, A_WIDTH), lambda b: (b, 0, 0)),
        out_shape=jax.ShapeDtypeStruct((BATCH, LP, A_WIDTH), jnp.bfloat16),
        compiler_params=_cparams(("arbitrary",)),
        name="mixer_a_window_gqa",
    )(sink, q3, k3, v3, g3)


def _mixer_b_kernel(lam_init, lam_ref, q_ref, k_ref, v_ref, g_ref, w_ref, o_ref, s_a, s_b, a_a, a_b):
    lam = lam_ref[0]
    lane_hi = lax.broadcasted_iota(jnp.int32, (TQ_B, LANES), 1) >= HEAD_DIM
    key_ok = lax.broadcasted_iota(jnp.int32, (TQ_B, BLOCK), 1) >= PAD
    w = w_ref[...] * (1.0 - lam_init)
    s_bufs, a_bufs = (s_a, s_b), (a_a, a_b)
    nt = (LP - Q0_B) // TQ_B
    nt_dims = (((1,), (1,)), ((), ()))
    tile_rows = lambda t: slice(Q0_B + t * TQ_B, Q0_B + (t + 1) * TQ_B)
    o_ref[0, 0:Q0_B, :] = jnp.zeros((Q0_B, LANES), o_ref.dtype)

    def scores(t):
        s_buf = s_bufs[t % 2]
        q = q_ref[0, tile_rows(t), :]
        for c in range(2):
            qm = jnp.where(lane_hi if c else jnp.logical_not(lane_hi), q, jnp.zeros_like(q))
            s0 = lax.dot_general(qm, k_ref[0, 0:BLOCK, :], nt_dims, preferred_element_type=jnp.float32)
            s_buf[c, :, 0:BLOCK] = jnp.where(key_ok, s0, NEG)
            s_buf[c, :, BLOCK:LP] = lax.dot_general(qm, k_ref[0, BLOCK:LP, :], nt_dims,
                                                    preferred_element_type=jnp.float32)

    def softmax_pv(t):
        s_buf, a_buf = s_bufs[t % 2], a_bufs[t % 2]
        inv_den0 = []
        for r in range(TQ_B // SM_ROWS):
            rows = slice(r * SM_ROWS, (r + 1) * SM_ROWS)
            es, dens = [], []
            for c in range(2):
                s = s_buf[c, rows, :]
                m = jnp.max(s, axis=-1, keepdims=True)
                e = jnp.exp(s - m)
                es.append(e)
                dens.append(jnp.sum(e, axis=-1, keepdims=True))
            a_buf[rows, :] = (es[0] - es[1] * (dens[0] * lam / dens[1])).astype(jnp.bfloat16)
            inv_den0.append(1.0 / dens[0])
        o = jnp.dot(a_buf[...], v_ref[0], preferred_element_type=jnp.float32)
        o = o * jnp.concatenate(inv_den0, axis=0)
        ms = jnp.mean(o * o, axis=-1, keepdims=True)
        y = o * lax.rsqrt(ms + RMS_EPS) * w
        y = y * g_ref[0, tile_rows(t), :].astype(jnp.float32)
        o_ref[0, tile_rows(t), :] = y.astype(o_ref.dtype)

    for t in range(nt + 1):
        if t < nt:
            scores(t)
        if t >= 1:
            softmax_pv(t - 1)


def _mixer_b(lam_init, lam, q3, k3, v3, g3, subln_w):
    qoff = A_WIDTH // LANES
    koff = (A_KV_HEADS * HEAD_DIM) // LANES
    voff = (C_WIDTH + A_KV_HEADS * HEAD_DIM) // LANES
    goff = A_WIDTH // LANES
    return pl.pallas_call(
        functools.partial(_mixer_b_kernel, lam_init),
        grid=(BATCH, B_HEADS),
        in_specs=[
            pl.BlockSpec(memory_space=pltpu.SMEM),
            pl.BlockSpec((1, LP, LANES), lambda b, h: (b, 0, qoff + h)),
            pl.BlockSpec((1, LP, LANES), lambda b, h: (b, 0, koff + h)),
            pl.BlockSpec((1, LP, LANES), lambda b, h: (b, 0, voff + h)),
            pl.BlockSpec((1, LP, LANES), lambda b, h: (b, 0, goff + h)),
            pl.BlockSpec((1, B_V_DIM), lambda b, h: (0, h)),
        ],
        out_specs=pl.BlockSpec((1, LP, LANES), lambda b, h: (b, 0, h)),
        out_shape=jax.ShapeDtypeStruct((BATCH, LP, B_WIDTH), jnp.bfloat16),
        scratch_shapes=[pltpu.VMEM((2, TQ_B, LP), jnp.float32), pltpu.VMEM((2, TQ_B, LP), jnp.float32),
                        pltpu.VMEM((TQ_B, LP), jnp.bfloat16), pltpu.VMEM((TQ_B, LP), jnp.bfloat16)],
        compiler_params=_cparams(("arbitrary", "arbitrary")),
        name="mixer_b_diff_attn",
    )(lam, q3, k3, v3, g3, subln_w.reshape(1, B_WIDTH))


def _split3(x):
    hi = x.astype(jnp.bfloat16)
    r1 = x - hi.astype(jnp.float32)
    mid = r1.astype(jnp.bfloat16)
    lo = (r1 - mid.astype(jnp.float32)).astype(jnp.bfloat16)
    return hi, mid, lo


def _hgrn_anchor(b, m, reverse):
    a_off = m if reverse else m - 1
    if 2 * m >= SUBLANES:
        b3 = b.reshape(BLOCK // (2 * m), 2 * m, C_WIDTH)
        return jnp.broadcast_to(b3[:, a_off:a_off + 1, :], b3.shape).reshape(BLOCK, C_WIDTH)
    b3 = b.reshape(BLOCK // SUBLANES, SUBLANES, C_WIDTH)
    sub = lax.broadcasted_iota(jnp.int32, b3.shape, 1)
    anchor = None
    for blk in range(SUBLANES // (2 * m)):
        r = blk * 2 * m + a_off
        cand = jnp.broadcast_to(b3[:, r:r + 1, :], b3.shape)
        anchor = cand if anchor is None else jnp.where(sub >= blk * 2 * m, cand, anchor)
    return anchor.reshape(BLOCK, C_WIDTH)


def _hgrn_chunks(chains, head16, half16, pair_key, same_head):
    nt_dims = (((1,), (1,)), ((), ()))
    f32 = jnp.float32
    for c in chains:
        hi, mid, lo = _split3(c["lf"])
        c["b"] = (jnp.dot(c["tri16"], hi, preferred_element_type=f32)
                  + jnp.dot(c["tri16"], mid, preferred_element_type=f32)
                  + jnp.dot(c["tri16"], lo, preferred_element_type=f32))
    for c in chains:
        state = c["state"] if c["prev"] is None else chains[c["prev"]]["new_state"]
        edge = 0 if c["reverse"] else BLOCK - 1
        b, b_all = c["b"], c["b"][edge:edge + 1, :]
        q_in = (c["q"] * jnp.exp2(b)).astype(jnp.bfloat16)
        c["o_inter"] = lax.dot_general(q_in, state.astype(jnp.bfloat16), nt_dims, preferred_element_type=f32)
        k_out = (c["kk"] * jnp.exp2(b_all - b)).astype(jnp.bfloat16)
        u_t = lax.dot_general(c["v16"], k_out, (((0,), (0,)), ((), ())), preferred_element_type=f32)
        c["new_state"] = jnp.exp2(b_all) * state + jnp.where(same_head, u_t, 0.0)
        c["p"] = None
    for m in reversed(HGRN_LEVELS):
        for c in chains:
            if m == 0:
                qt, kt = c["q"], c["kk"]
            else:
                wgt = jnp.exp2(-jnp.abs(c["b"] - _hgrn_anchor(c["b"], m, c["reverse"])))
                qt, kt = c["q"] * wgt, c["kk"] * wgt
            parts = []
            for col in range(C_WIDTH // LANES):
                q_col = qt[:, col * LANES:(col + 1) * LANES].astype(jnp.bfloat16)
                k_col = kt[:, col * LANES:(col + 1) * LANES].astype(jnp.bfloat16)
                k_two = jnp.concatenate([k_col * half16[hf] for hf in range(2)], axis=0)
                parts.append(lax.dot_general(q_col, k_two, nt_dims, preferred_element_type=f32))
            pm = jnp.concatenate(parts, axis=1)
            c["p"] = pm if c["p"] is None else jnp.where(pair_key < max(2 * m, 1), pm, c["p"])
    outs = []
    for c in chains:
        p = jnp.where(c["tri_ok"], c["p"], 0.0).astype(jnp.bfloat16)
        v_heads = jnp.concatenate([c["v16"] * head16[h] for h in range(C_HEADS)], axis=0)
        outs.append((jnp.dot(p, v_heads, preferred_element_type=f32) + c["o_inter"], c["new_state"]))
    return outs


def _mixer_c_kernel(q_ref, v_ref, lff_ref, lfb_ref, kkf_ref, kkb_ref, g_ref, w_ref, o_ref, accf_ref, accr_ref):
    lane_head = lax.broadcasted_iota(jnp.int32, (BLOCK, C_WIDTH), 1) // HEAD_DIM
    head16 = [(lane_head == h).astype(jnp.bfloat16) for h in range(C_HEADS)]
    lane_half = lax.broadcasted_iota(jnp.int32, (BLOCK, LANES), 1) // HEAD_DIM
    half16 = [(lane_half == hf).astype(jnp.bfloat16) for hf in range(2)]
    sr = lax.broadcasted_iota(jnp.int32, (C_WIDTH, C_WIDTH), 0) // HEAD_DIM
    sc = lax.broadcasted_iota(jnp.int32, (C_WIDTH, C_WIDTH), 1) // HEAD_DIM
    same_head = sr == sc
    head_ones16 = same_head.astype(jnp.bfloat16)
    t_i = lax.broadcasted_iota(jnp.int32, (BLOCK, C_HEADS * BLOCK), 0)
    s_i = lax.broadcasted_iota(jnp.int32, (BLOCK, C_HEADS * BLOCK), 1) % BLOCK
    pair_key = t_i ^ s_i
    r_i = lax.broadcasted_iota(jnp.int32, (BLOCK, BLOCK), 0)
    c_i = lax.broadcasted_iota(jnp.int32, (BLOCK, BLOCK), 1)
    tri_f, tri_r = (c_i <= r_i).astype(jnp.bfloat16), (c_i >= r_i).astype(jnp.bfloat16)
    ok_f, ok_r = s_i <= t_i, s_i >= t_i

    def run(steps, states):
        chains = []
        for si, step in enumerate(steps):
            for reverse in (False, True):
                n = (NBLK - 1 - step) if reverse else step
                r0 = n * BLOCK if isinstance(n, int) else pl.multiple_of(n * BLOCK, BLOCK)
                rows = pl.ds(r0, BLOCK)
                lf_ref, kk_ref = (lfb_ref, kkb_ref) if reverse else (lff_ref, kkf_ref)
                chains.append(dict(
                    reverse=reverse, rows=rows, q=q_ref[0, rows, :].astype(jnp.float32), v16=v_ref[0, rows, :],
                    kk=kk_ref[0, rows, :], lf=lf_ref[0, rows, :], tri16=tri_r if reverse else tri_f,
                    tri_ok=ok_r if reverse else ok_f, state=states[reverse], prev=None if si == 0 else len(chains) - 2))
        outs = _hgrn_chunks(chains, head16, half16, pair_key, same_head)
        for c, (o, _) in zip(chains, outs):
            (accr_ref if c["reverse"] else accf_ref)[c["rows"], :] = o
        return outs[-2][1], outs[-1][1]

    zero_state = jnp.zeros((C_WIDTH, C_WIDTH), jnp.float32)
    group = HGRN_STEPS_PER_ITER
    states = lax.fori_loop(0, NBLK // group, lambda j, st: run(tuple(group * j + u for u in range(group)), st),
                           (zero_state, zero_state))
    if NBLK % group:
        run(tuple(range(NBLK - NBLK % group, NBLK)), states)

    w = w_ref[...]

    def finish(n, carry):
        r0 = pl.multiple_of(n * BLOCK, BLOCK)
        y = accf_ref[pl.ds(r0, BLOCK), :] + accr_ref[pl.ds(r0, BLOCK), :]
        sq = y * y
        sq_hi = sq.astype(jnp.bfloat16)
        sq_lo = (sq - sq_hi.astype(jnp.float32)).astype(jnp.bfloat16)
        ms = (jnp.dot(sq_hi, head_ones16, preferred_element_type=jnp.float32)
              + jnp.dot(sq_lo, head_ones16, preferred_element_type=jnp.float32)) * (1.0 / HEAD_DIM)
        y = y * lax.rsqrt(ms + RMS_EPS) * w
        y = y * g_ref[0, pl.ds(r0, BLOCK), :].astype(jnp.float32)
        o_ref[0, pl.ds(r0, BLOCK), :] = y.astype(o_ref.dtype)
        return carry

    lax.fori_loop(0, NBLK, finish, 0)


def _mixer_c(q3, v3, lf3, kk3, g3, norm_w):
    blk = lambda j: pl.BlockSpec((1, LP, C_WIDTH), lambda b: (b, 0, j))
    return pl.pallas_call(
        _mixer_c_kernel,
        grid=(BATCH,),
        in_specs=[
            blk(4),
            blk(0),
            blk(0), blk(1),
            blk(0), blk(1),
            blk(3),
            pl.BlockSpec((1, C_WIDTH), lambda b: (0, 0)),
        ],
        out_specs=pl.BlockSpec((1, LP, C_WIDTH), lambda b: (b, 0, 0)),
        out_shape=jax.ShapeDtypeStruct((BATCH, LP, C_WIDTH), jnp.bfloat16),
        scratch_shapes=[pltpu.VMEM((LP, C_WIDTH), jnp.float32), pltpu.VMEM((LP, C_WIDTH), jnp.float32)],
        compiler_params=_cparams(("arbitrary",)),
        name="mixer_c_hgrn2",
    )(q3, v3, lf3, lf3, kk3, kk3, g3, norm_w.reshape(1, C_WIDTH))


def _outproj_last_kernel(ya_ref, yb_ref, yc_ref, wa_ref, wb_ref, wc_ref, h_ref, g_ref, b_ref, o_ref):
    acc = jnp.dot(ya_ref[...], wa_ref[...], preferred_element_type=jnp.float32)
    acc = acc + jnp.dot(yb_ref[...], wb_ref[...], preferred_element_type=jnp.float32)
    acc = acc + jnp.dot(yc_ref[...], wc_ref[...], preferred_element_type=jnp.float32)
    o_ref[0] = _layer_norm(ALPHA * h_ref[...] + acc, g_ref[...], b_ref[...])


def _outproj_last(ya, yb, yc, wa, wb, wc, h32, g, b):
    tl = SEQ // 4
    row = lambda n: pl.BlockSpec((pl.Element(tl), pl.Element(n)),
                                 lambda bi, j: (pl.multiple_of(bi * LP + BLOCK + j * tl, BLOCK), 0))
    full = lambda n: pl.BlockSpec((n, D_MODEL), lambda bi, j: (0, 0))
    vec = pl.BlockSpec((1, D_MODEL), lambda bi, j: (0, 0))
    return pl.pallas_call(
        _outproj_last_kernel,
        grid=(BATCH, SEQ // tl),
        in_specs=[row(A_WIDTH), row(B_WIDTH), row(C_WIDTH), full(A_WIDTH), full(B_WIDTH), full(C_WIDTH),
                  row(D_MODEL), vec, vec],
        out_specs=pl.BlockSpec((1, tl, D_MODEL), lambda bi, j: (bi, j, 0)),
        out_shape=jax.ShapeDtypeStruct((BATCH, SEQ, D_MODEL), jnp.float32),
        compiler_params=_cparams(("arbitrary", "arbitrary")),
        name="outproj_last_residual_ln",
    )(ya.reshape(M_ROWS, A_WIDTH), yb.reshape(M_ROWS, B_WIDTH), yc.reshape(M_ROWS, C_WIDTH), wa, wb, wc, h32,
      g.reshape(1, D_MODEL), b.reshape(1, D_MODEL))


def _regroup_w_in(w):
    def heads(lo, order):
        return [w[:, lo + h * HEAD_DIM: lo + (h + 1) * HEAD_DIM] for h in order]

    c = lambda lo, n: w[:, lo:lo + n]
    o_qa, o_ka, o_va, o_ga = 0, 256, 384, 512
    o_qb, o_kb, o_vb, o_gb = 768, 1280, 1792, 2304
    o_qc, o_ic, o_zf, o_zb, o_gc = 2816, 3072, 3328, 3584, 3840
    cat = lambda xs: jnp.concatenate(xs, axis=1).astype(jnp.bfloat16)
    w_q = cat(heads(o_qa, A_HEAD_ORDER) + [c(o_qb, 512)])
    w_k = cat([c(o_ka, 128), c(o_kb, 512)])
    w_p = cat([c(o_ic, 256), c(o_va, 128), c(o_vb, 512)])
    w_s = cat(heads(o_ga, A_HEAD_ORDER) + [c(o_gb, 512), c(o_gc, 256), c(o_qc, 256)])
    w_g = cat([c(o_zf, 256), c(o_zb, 256)])
    return jnp.concatenate([w_q, w_k, w_p, w_s, w_g], axis=1)


def _rope_tables():
    pos = (jnp.arange(LP) - PAD).astype(jnp.float32)
    inv = ROPE_THETA ** (-jnp.arange(0, HEAD_DIM, 2, dtype=jnp.float32) / HEAD_DIM)
    ang = pos[:, None] * inv[None, :]
    cos, sin = jnp.cos(ang), jnp.sin(ang)
    cos_t = jnp.concatenate([cos, cos, cos, cos], axis=1)
    sin_t = jnp.concatenate([-sin, sin, -sin, sin], axis=1)
    return cos_t, sin_t


def kernel(x, meta, emb_ln_g, emb_ln_b, w_in, w_out, a_sink, b_lam, b_subln_w, c_lb_logits, c_norm_w, ln_g, ln_b):
    cos_t, sin_t = _rope_tables()
    p_lb = jax.nn.softmax(c_lb_logits.astype(jnp.float32), axis=0)
    lb_all = jnp.cumsum(p_lb, axis=0) - p_lb[0:1]

    front_args = (x, meta, emb_ln_g, emb_ln_b)
    for l in range(DEPTH):
        lam_init = 0.8 - 0.6 * math.exp(-0.3 * l)
        lp32 = b_lam[l].astype(jnp.float32)
        lam = jnp.exp(jnp.sum(lp32[0] * lp32[1])) - jnp.exp(jnp.sum(lp32[2] * lp32[3])) + lam_init
        wo = w_out[l]
        wa = jnp.concatenate([wo[h * HEAD_DIM:(h + 1) * HEAD_DIM] for h in A_HEAD_ORDER], axis=0).astype(jnp.bfloat16)
        wb = wo[A_WIDTH:A_WIDTH + B_WIDTH].astype(jnp.bfloat16)
        wc = wo[A_WIDTH + B_WIDTH:].astype(jnp.bfloat16)
        sink = jnp.stack([a_sink[l, h] for h in A_HEAD_ORDER]).astype(jnp.float32)

        h32, *proj = _front(front_args, _regroup_w_in(w_in[l]), cos_t, sin_t, lb_all[l].reshape(1, C_WIDTH))
        q_all, k_all, p_all, s_all, lf_all, kk_all = [t.reshape(BATCH, LP, t.shape[-1]) for t in proj]

        ya = _mixer_a(sink, q_all, k_all, p_all, s_all)
        yb = _mixer_b(lam_init, lam.reshape(1), q_all, k_all, p_all, s_all, b_subln_w[l])
        yc = _mixer_c(s_all, p_all, lf_all, kk_all, s_all, c_norm_w[l])

        if l == DEPTH - 1:
            return _outproj_last(ya, yb, yc, wa, wb, wc, h32, ln_g[l], ln_b[l])
        front_args = (ya.reshape(M_ROWS, A_WIDTH), yb.reshape(M_ROWS, B_WIDTH), yc.reshape(M_ROWS, C_WIDTH),
                      wa, wb, wc, h32, ln_g[l], ln_b[l])
```

```python
import functools
import math

import jax
import jax.numpy as jnp
import numpy as np
from jax import lax
from jax.experimental import pallas as pl
from jax.experimental.pallas import tpu as pltpu

D_MODEL = 1024
BATCH = 8
SEQ = 2048
DEPTH = 2
N_META = 16
BLOCK = 128
PAD = BLOCK - N_META
WINDOW = 128
ROPE_THETA = 10000.0
HEAD_DIM = 64
A_HEADS = 4
A_KV_HEADS = 2
A_WIDTH = A_HEADS * HEAD_DIM
B_HEADS = 4
B_V_DIM = 2 * HEAD_DIM
B_WIDTH = B_HEADS * B_V_DIM
C_HEADS = 4
C_WIDTH = C_HEADS * HEAD_DIM
D_MIX = A_WIDTH + B_WIDTH + C_WIDTH
ALPHA = (2 * DEPTH) ** 0.25
LN_EPS = 1e-5
RMS_EPS = 1e-6
NEG = -1e30
F_MIN = 1e-30

LP = SEQ + N_META + PAD
NBLK = LP // BLOCK
M_ROWS = BATCH * LP
TM = LP // 4
LANES = 128
SUBLANES = 8
MXU_N = 256
LOG2E = math.log2(math.e)
Q_SCALE = HEAD_DIM ** -0.5 * LOG2E
Q0_B = 96
TQ_B = (LP - Q0_B) // 5
SM_ROWS = 16
VMEM_LIMIT = 48 * 1024 * 1024

N_ROPE_Q = A_WIDTH + 2 * B_HEADS * HEAD_DIM
N_ROPE_K = A_KV_HEADS * HEAD_DIM + 2 * B_HEADS * HEAD_DIM
N_PLAIN = C_WIDTH + A_KV_HEADS * HEAD_DIM + B_WIDTH
N_SILU = A_WIDTH + B_WIDTH + C_WIDTH + C_WIDTH
N_GATE = 2 * C_WIDTH
N_IN = N_ROPE_Q + N_ROPE_K + N_PLAIN + N_SILU + N_GATE
A_HEAD_ORDER = (0, 2, 1, 3)

HGRN_LEVELS = (0, 1, 2, 4, 8, 16, 32, 64)
HGRN_STEPS_PER_ITER = 2


def _cparams(sem, flags=None):
    return pltpu.CompilerParams(dimension_semantics=sem, vmem_limit_bytes=VMEM_LIMIT, flags=flags)


def _layer_norm(x, g, b):
    mu = jnp.mean(x, axis=-1, keepdims=True)
    xc = x - mu
    var = jnp.mean(xc * xc, axis=-1, keepdims=True)
    return xc * lax.rsqrt(var + LN_EPS) * g + b


def _front_embed_kernel(x_ref, meta_ref, g_ref, b_ref, w_ref, cos_ref, sin_ref, lb_ref,
                        h32_ref, q_ref, k_ref, p_ref, s_ref, lf_ref, kk_ref, h16_ref):
    j = pl.program_id(0) % (LP // TM)

    def put(rows, y):
        h32_ref[rows, :] = y
        h16_ref[rows, :] = y.astype(jnp.bfloat16)

    @pl.when(j == 0)
    def _():
        put(slice(0, PAD), jnp.zeros((PAD, D_MODEL), jnp.float32))
        put(slice(PAD, BLOCK), _layer_norm(meta_ref[...], g_ref[...], b_ref[...]))
        put(slice(BLOCK, TM), _layer_norm(x_ref[0:TM - BLOCK, :], g_ref[...], b_ref[...]))

    @pl.when(j > 0)
    def _():
        put(slice(0, TM), _layer_norm(x_ref[...], g_ref[...], b_ref[...]))

    _inproj_body(h16_ref[...], slice(0, TM), w_ref, cos_ref, sin_ref, lb_ref,
                 (q_ref, k_ref, p_ref, s_ref, lf_ref, kk_ref))


def _front_outproj_kernel(ya_ref, yb_ref, yc_ref, wa_ref, wb_ref, wc_ref, h_ref, g_ref, b_ref,
                          w_ref, cos_ref, sin_ref, lb_ref, h32_ref, q_ref, k_ref, p_ref, s_ref, lf_ref, kk_ref):
    halves = (slice(0, TM // 2), slice(TM // 2, TM))
    accs = []
    for rows in halves:
        acc = jnp.dot(ya_ref[rows, :], wa_ref[...], preferred_element_type=jnp.float32)
        acc = acc + jnp.dot(yb_ref[rows, :], wb_ref[...], preferred_element_type=jnp.float32)
        accs.append(acc + jnp.dot(yc_ref[rows, :], wc_ref[...], preferred_element_type=jnp.float32))
    for rows, acc in zip(halves, accs):
        h = _layer_norm(ALPHA * h_ref[rows, :] + acc, g_ref[...], b_ref[...])
        h32_ref[rows, :] = h
        _inproj_body(h.astype(jnp.bfloat16), rows, w_ref, cos_ref, sin_ref, lb_ref,
                     (q_ref, k_ref, p_ref, s_ref, lf_ref, kk_ref))


def _front(prologue_args, w, cos_t, sin_t, lb):
    row = lambda n: pl.BlockSpec((TM, n), lambda i: (i, 0))
    const = lambda r, n: pl.BlockSpec((r, n), lambda i: (0, 0))
    vec = const(1, D_MODEL)
    tab_spec = pl.BlockSpec((TM, LANES), lambda i: (i % (LP // TM), 0))
    tiles_per_b = LP // TM
    if len(prologue_args) == 4:
        x, meta, g, b = prologue_args
        x_spec = pl.BlockSpec(
            (pl.Element(TM), pl.Element(D_MODEL)),
            lambda i: (pl.multiple_of((i // tiles_per_b) * SEQ + jnp.maximum((i % tiles_per_b) * TM - BLOCK, 0),
                                      SUBLANES), 0))
        body, name = _front_embed_kernel, "front_embed_inproj"
        args = (x.reshape(BATCH * SEQ, D_MODEL), meta, g.reshape(1, D_MODEL), b.reshape(1, D_MODEL))
        in_specs = [x_spec, const(N_META, D_MODEL), vec, vec]
        scratch = [pltpu.VMEM((TM, D_MODEL), jnp.bfloat16)]
    else:
        ya, yb, yc, wa, wb, wc, h32, g, b = prologue_args
        body, name = _front_outproj_kernel, "front_outproj_inproj"
        args = (ya, yb, yc, wa, wb, wc, h32, g.reshape(1, D_MODEL), b.reshape(1, D_MODEL))
        in_specs = [row(A_WIDTH), row(B_WIDTH), row(C_WIDTH), const(A_WIDTH, D_MODEL), const(B_WIDTH, D_MODEL),
                    const(C_WIDTH, D_MODEL), row(D_MODEL), vec, vec]
        scratch = []
    widths = (D_MODEL, N_ROPE_Q, N_ROPE_K, N_PLAIN, N_SILU, N_GATE, N_GATE)
    dtypes = (jnp.float32,) + (jnp.bfloat16,) * 4 + (jnp.float32,) * 2
    return pl.pallas_call(
        body,
        grid=(M_ROWS // TM,),
        in_specs=in_specs + [const(D_MODEL, N_IN), tab_spec, tab_spec, const(1, C_WIDTH)],
        out_specs=[row(n) for n in widths],
        out_shape=[jax.ShapeDtypeStruct((M_ROWS, n), dt) for n, dt in zip(widths, dtypes)],
        scratch_shapes=scratch,
        compiler_params=_cparams(("arbitrary",)),
        name=name,
    )(*args, w, cos_t, sin_t, lb)


def _col_chunks(n):
    return [(c, min(MXU_N, n - c)) for c in range(0, n, MXU_N)]


def _rope_cols(acc, cos, sin_signed, first_half):
    partner = jnp.where(first_half, pltpu.roll(acc, LANES - HEAD_DIM // 2, 1), pltpu.roll(acc, HEAD_DIM // 2, 1))
    return acc * cos + partner * sin_signed


def _inproj_body(x, rows, w_ref, cos_ref, sin_ref, lb_ref, outs):
    q_ref, k_ref, p_ref, s_ref, lf_ref, kk_ref = [o.at[rows, :] for o in outs]
    n_rows = rows.stop - rows.start
    cos = cos_ref[rows, :]
    sin = sin_ref[rows, :]
    lane = lax.broadcasted_iota(jnp.int32, (n_rows, LANES), 1)
    first_half = (lane % HEAD_DIM) < (HEAD_DIM // 2)
    row = ((pl.program_id(0) % (LP // TM)) * TM + rows.start
           + lax.broadcasted_iota(jnp.int32, (n_rows, LANES), 0))
    valid = row >= PAD
    pieces = ([("rope_q", q_ref, c) for c in range(0, N_ROPE_Q, LANES)]
              + [("rope_k", k_ref, c) for c in range(0, N_ROPE_K, LANES)]
              + [("plain", p_ref, c) for c in range(0, N_PLAIN, LANES)]
              + [("silu", s_ref, c) for c in range(0, N_SILU, LANES)]
              + [("gate", None, c) for c in range(0, N_GATE, LANES)])
    for c0, cw in _col_chunks(N_IN):
        acc = jnp.dot(x, w_ref[:, c0:c0 + cw], preferred_element_type=jnp.float32)
        for j in range(cw // LANES):
            kind, dst, dc = pieces[c0 // LANES + j]
            a = acc[:, j * LANES:(j + 1) * LANES]
            if kind == "rope_q":
                dst[:, dc:dc + LANES] = (_rope_cols(a, cos, sin, first_half) * Q_SCALE).astype(dst.dtype)
            elif kind == "rope_k":
                dst[:, dc:dc + LANES] = _rope_cols(a, cos, sin, first_half).astype(dst.dtype)
            elif kind == "plain":
                dst[:, dc:dc + LANES] = a.astype(dst.dtype)
            elif kind == "silu":
                dst[:, dc:dc + LANES] = (a * jax.nn.sigmoid(a)).astype(dst.dtype)
            else:
                lb = lb_ref[:, dc % C_WIDTH: dc % C_WIDTH + LANES]
                f = lb + (1.0 - lb) * jax.nn.sigmoid(a)
                lf_ref[:, dc:dc + LANES] = jnp.where(valid, jnp.log2(jnp.maximum(f, F_MIN)), 0.0)
                kk_ref[:, dc:dc + LANES] = jnp.where(valid, (1.0 - lb) * jax.nn.sigmoid(-a), 0.0)


def _mixer_a_kernel(sink_ref, q_ref, k_ref, v_ref, g_ref, o_ref):
    two = 2 * BLOCK
    row = lax.broadcasted_iota(jnp.int32, (two, 4 * BLOCK), 0) % BLOCK
    col = lax.broadcasted_iota(jnp.int32, (two, 4 * BLOCK), 1)
    is_meta = (col < BLOCK) & (col >= PAD)
    top = lax.broadcasted_iota(jnp.int32, (two, 1), 0) < BLOCK
    lane_hi = lax.broadcasted_iota(jnp.int32, (BLOCK, LANES), 1) >= HEAD_DIM
    masks = {}

    def band_start(n):
        return min(max((n - 1) * BLOCK, 0), LP - 3 * BLOCK)

    def scores(n):
        start = band_start(n)
        key = (n * BLOCK - start, start < BLOCK)
        if key not in masks:
            kpos = start + col - BLOCK
            band = (col >= BLOCK) & (jnp.abs(n * BLOCK + row - kpos) <= WINDOW)
            if start < BLOCK:
                band = band & (kpos >= BLOCK)
            masks[key] = is_meta | band
        kcat = jnp.concatenate([k_ref[0, 0:BLOCK, :], k_ref[0, start:start + 3 * BLOCK, :]], axis=0)
        qcols = [q_ref[0, n * BLOCK:(n + 1) * BLOCK, cv * LANES:(cv + 1) * LANES] for cv in range(2)]
        out = []
        for hf in range(2):
            sel = lane_hi if hf else jnp.logical_not(lane_hi)
            q2 = jnp.concatenate([jnp.where(sel, qc, jnp.zeros_like(qc)) for qc in qcols], axis=0)
            s = lax.dot_general(q2, kcat, (((1,), (1,)), ((), ())), preferred_element_type=jnp.float32)
            out.append(jnp.where(masks[key], s, NEG))
        return out

    def softmax_pv(n, s_halves):
        start = band_start(n)
        vcat = jnp.concatenate([v_ref[0, 0:BLOCK, :], v_ref[0, start:start + 3 * BLOCK, :]], axis=0)
        rows = slice(n * BLOCK, (n + 1) * BLOCK)
        outs = []
        for hf, s in enumerate(s_halves):
            sink = jnp.where(top, sink_ref[hf], sink_ref[2 + hf]) * LOG2E
            m = jnp.maximum(jnp.max(s, axis=-1, keepdims=True), sink)
            e = jnp.exp2(s - m)
            den = jnp.sum(e, axis=-1, keepdims=True) + jnp.exp2(sink - m)
            p = (e * (1.0 / den)).astype(jnp.bfloat16)
            outs.append(jnp.dot(p, vcat, preferred_element_type=jnp.float32))
        for cv in range(2):
            o = jnp.where(lane_hi, outs[1][cv * BLOCK:(cv + 1) * BLOCK], outs[0][cv * BLOCK:(cv + 1) * BLOCK])
            y = o * g_ref[0, rows, cv * LANES:(cv + 1) * LANES].astype(jnp.float32)
            o_ref[0, rows, cv * LANES:(cv + 1) * LANES] = y.astype(o_ref.dtype)

    ahead = 2
    pending = {}
    for n in range(NBLK + ahead):
        if n < NBLK:
            pending[n] = scores(n)
        if n >= ahead:
            softmax_pv(n - ahead, pending.pop(n - ahead))


def _mixer_a(sink, q3, k3, v3, g3):
    return pl.pallas_call(
        _mixer_a_kernel,
        grid=(BATCH,),
        in_specs=[
            pl.BlockSpec(memory_space=pltpu.SMEM),
            pl.BlockSpec((1, LP, A_WIDTH), lambda b: (b, 0, 0)),
            pl.BlockSpec((1, LP, LANES), lambda b: (b, 0, 0)),
            pl.BlockSpec((1, LP, LANES), lambda b: (b, 0, C_WIDTH // LANES)),
            pl.BlockSpec((1, LP, A_WIDTH), lambda b: (b, 0, 0)),
        ],
        out_specs=pl.BlockSpec((1, LP, A_WIDTH), lambda b: (b, 0, 0)),
        out_shape=jax.ShapeDtypeStruct((BATCH, LP, A_WIDTH), jnp.bfloat16),
        compiler_params=_cparams(("arbitrary",)),
        name="mixer_a_window_gqa",
    )(sink, q3, k3, v3, g3)


def _mixer_b_kernel(lam_init, lam_ref, q_ref, k_ref, v_ref, g_ref, w_ref, o_ref, s_a, s_b, a_a, a_b):
    lam = lam_ref[0]
    lane_hi = lax.broadcasted_iota(jnp.int32, (TQ_B, LANES), 1) >= HEAD_DIM
    key_ok = lax.broadcasted_iota(jnp.int32, (TQ_B, BLOCK), 1) >= PAD
    w = w_ref[...] * (1.0 - lam_init)
    s_bufs, a_bufs = (s_a, s_b), (a_a, a_b)
    nt = (LP - Q0_B) // TQ_B
    nt_dims = (((1,), (1,)), ((), ()))
    tile_rows = lambda t: slice(Q0_B + t * TQ_B, Q0_B + (t + 1) * TQ_B)
    o_ref[0, 0:Q0_B, :] = jnp.zeros((Q0_B, LANES), o_ref.dtype)

    def scores(t):
        s_buf = s_bufs[t % 2]
        q = q_ref[0, tile_rows(t), :]
        for c in range(2):
            qm = jnp.where(lane_hi if c else jnp.logical_not(lane_hi), q, jnp.zeros_like(q))
            s0 = lax.dot_general(qm, k_ref[0, 0:BLOCK, :], nt_dims, preferred_element_type=jnp.float32)
            s_buf[c, :, 0:BLOCK] = jnp.where(key_ok, s0, NEG)
            s_buf[c, :, BLOCK:LP] = lax.dot_general(qm, k_ref[0, BLOCK:LP, :], nt_dims,
                                                    preferred_element_type=jnp.float32)

    def softmax_pv(t):
        s_buf, a_buf = s_bufs[t % 2], a_bufs[t % 2]
        inv_den0 = []
        for r in range(TQ_B // SM_ROWS):
            rows = slice(r * SM_ROWS, (r + 1) * SM_ROWS)
            es, dens = [], []
            for c in range(2):
                s = s_buf[c, rows, :]
                m = jnp.max(s, axis=-1, keepdims=True)
                e = jnp.exp2(s - m)
                es.append(e)
                dens.append(jnp.sum(e, axis=-1, keepdims=True))
            a_buf[rows, :] = (es[0] - es[1] * (dens[0] * lam / dens[1])).astype(jnp.bfloat16)
            inv_den0.append(1.0 / dens[0])
        o = jnp.dot(a_buf[...], v_ref[0], preferred_element_type=jnp.float32)
        o = o * jnp.concatenate(inv_den0, axis=0)
        ms = jnp.mean(o * o, axis=-1, keepdims=True)
        y = o * lax.rsqrt(ms + RMS_EPS) * w
        y = y * g_ref[0, tile_rows(t), :].astype(jnp.float32)
        o_ref[0, tile_rows(t), :] = y.astype(o_ref.dtype)

    for t in range(nt + 1):
        if t < nt:
            scores(t)
        if t >= 1:
            softmax_pv(t - 1)


def _mixer_b(lam_init, lam, q3, k3, v3, g3, subln_w):
    qoff = A_WIDTH // LANES
    koff = (A_KV_HEADS * HEAD_DIM) // LANES
    voff = (C_WIDTH + A_KV_HEADS * HEAD_DIM) // LANES
    goff = A_WIDTH // LANES
    return pl.pallas_call(
        functools.partial(_mixer_b_kernel, lam_init),
        grid=(BATCH, B_HEADS),
        in_specs=[
            pl.BlockSpec(memory_space=pltpu.SMEM),
            pl.BlockSpec((1, LP, LANES), lambda b, h: (b, 0, qoff + h)),
            pl.BlockSpec((1, LP, LANES), lambda b, h: (b, 0, koff + h)),
            pl.BlockSpec((1, LP, LANES), lambda b, h: (b, 0, voff + h)),
            pl.BlockSpec((1, LP, LANES), lambda b, h: (b, 0, goff + h)),
            pl.BlockSpec((1, B_V_DIM), lambda b, h: (0, h)),
        ],
        out_specs=pl.BlockSpec((1, LP, LANES), lambda b, h: (b, 0, h)),
        out_shape=jax.ShapeDtypeStruct((BATCH, LP, B_WIDTH), jnp.bfloat16),
        scratch_shapes=[pltpu.VMEM((2, TQ_B, LP), jnp.float32), pltpu.VMEM((2, TQ_B, LP), jnp.float32),
                        pltpu.VMEM((TQ_B, LP), jnp.bfloat16), pltpu.VMEM((TQ_B, LP), jnp.bfloat16)],
        compiler_params=_cparams(("arbitrary", "arbitrary")),
        name="mixer_b_diff_attn",
    )(lam, q3, k3, v3, g3, subln_w.reshape(1, B_WIDTH))


def _split3(x):
    hi = x.astype(jnp.bfloat16)
    r1 = x - hi.astype(jnp.float32)
    mid = r1.astype(jnp.bfloat16)
    lo = (r1 - mid.astype(jnp.float32)).astype(jnp.bfloat16)
    return hi, mid, lo


def _hgrn_anchor(b, m, reverse):
    a_off = m if reverse else m - 1
    if 2 * m >= SUBLANES:
        b3 = b.reshape(BLOCK // (2 * m), 2 * m, C_WIDTH)
        return jnp.broadcast_to(b3[:, a_off:a_off + 1, :], b3.shape).reshape(BLOCK, C_WIDTH)
    b3 = b.reshape(BLOCK // SUBLANES, SUBLANES, C_WIDTH)
    sub = lax.broadcasted_iota(jnp.int32, b3.shape, 1)
    anchor = None
    for blk in range(SUBLANES // (2 * m)):
        r = blk * 2 * m + a_off
        cand = jnp.broadcast_to(b3[:, r:r + 1, :], b3.shape)
        anchor = cand if anchor is None else jnp.where(sub >= blk * 2 * m, cand, anchor)
    return anchor.reshape(BLOCK, C_WIDTH)


def _hgrn_chunks(chains, head16, half16, pair_key, same_head):
    nt_dims = (((1,), (1,)), ((), ()))
    f32 = jnp.float32
    for c in chains:
        hi, mid, lo = _split3(c["lf"])
        c["b"] = (jnp.dot(c["tri16"], hi, preferred_element_type=f32)
                  + jnp.dot(c["tri16"], mid, preferred_element_type=f32)
                  + jnp.dot(c["tri16"], lo, preferred_element_type=f32))
    for c in chains:
        state = c["state"] if c["prev"] is None else chains[c["prev"]]["new_state"]
        edge = 0 if c["reverse"] else BLOCK - 1
        b, b_all = c["b"], c["b"][edge:edge + 1, :]
        q_in = (c["q"] * jnp.exp2(b)).astype(jnp.bfloat16)
        c["o_inter"] = lax.dot_general(q_in, state.astype(jnp.bfloat16), nt_dims, preferred_element_type=f32)
        k_out = (c["kk"] * jnp.exp2(b_all - b)).astype(jnp.bfloat16)
        u_t = lax.dot_general(c["v16"], k_out, (((0,), (0,)), ((), ())), preferred_element_type=f32)
        c["new_state"] = jnp.exp2(b_all) * state + jnp.where(same_head, u_t, 0.0)
        c["p"] = None
    for m in reversed(HGRN_LEVELS):
        for c in chains:
            if m == 0:
                qt, kt = c["q"], c["kk"]
            else:
                wgt = jnp.exp2(-jnp.abs(c["b"] - _hgrn_anchor(c["b"], m, c["reverse"])))
                qt, kt = c["q"] * wgt, c["kk"] * wgt
            parts = []
            for col in range(C_WIDTH // LANES):
                q_col = qt[:, col * LANES:(col + 1) * LANES].astype(jnp.bfloat16)
                k_col = kt[:, col * LANES:(col + 1) * LANES].astype(jnp.bfloat16)
                k_two = jnp.concatenate([k_col * half16[hf] for hf in range(2)], axis=0)
                parts.append(lax.dot_general(q_col, k_two, nt_dims, preferred_element_type=f32))
            pm = jnp.concatenate(parts, axis=1)
            c["p"] = pm if c["p"] is None else jnp.where(pair_key < max(2 * m, 1), pm, c["p"])
    outs = []
    for c in chains:
        p = jnp.where(c["tri_ok"], c["p"], 0.0).astype(jnp.bfloat16)
        v_heads = jnp.concatenate([c["v16"] * head16[h] for h in range(C_HEADS)], axis=0)
        outs.append((jnp.dot(p, v_heads, preferred_element_type=f32) + c["o_inter"], c["new_state"]))
    return outs


def _mixer_c_kernel(q_ref, v_ref, lff_ref, lfb_ref, kkf_ref, kkb_ref, g_ref, w_ref, o_ref, accf_ref, accr_ref):
    lane_head = lax.broadcasted_iota(jnp.int32, (BLOCK, C_WIDTH), 1) // HEAD_DIM
    head16 = [(lane_head == h).astype(jnp.bfloat16) for h in range(C_HEADS)]
    lane_half = lax.broadcasted_iota(jnp.int32, (BLOCK, LANES), 1) // HEAD_DIM
    half16 = [(lane_half == hf).astype(jnp.bfloat16) for hf in range(2)]
    sr = lax.broadcasted_iota(jnp.int32, (C_WIDTH, C_WIDTH), 0) // HEAD_DIM
    sc = lax.broadcasted_iota(jnp.int32, (C_WIDTH, C_WIDTH), 1) // HEAD_DIM
    same_head = sr == sc
    head_ones16 = same_head.astype(jnp.bfloat16)
    t_i = lax.broadcasted_iota(jnp.int32, (BLOCK, C_HEADS * BLOCK), 0)
    s_i = lax.broadcasted_iota(jnp.int32, (BLOCK, C_HEADS * BLOCK), 1) % BLOCK
    pair_key = t_i ^ s_i
    r_i = lax.broadcasted_iota(jnp.int32, (BLOCK, BLOCK), 0)
    c_i = lax.broadcasted_iota(jnp.int32, (BLOCK, BLOCK), 1)
    tri_f, tri_r = (c_i <= r_i).astype(jnp.bfloat16), (c_i >= r_i).astype(jnp.bfloat16)
    ok_f, ok_r = s_i <= t_i, s_i >= t_i

    def run(steps, states):
        chains = []
        for si, step in enumerate(steps):
            for reverse in (False, True):
                n = (NBLK - 1 - step) if reverse else step
                r0 = n * BLOCK if isinstance(n, int) else pl.multiple_of(n * BLOCK, BLOCK)
                rows = pl.ds(r0, BLOCK)
                lf_ref, kk_ref = (lfb_ref, kkb_ref) if reverse else (lff_ref, kkf_ref)
                chains.append(dict(
                    reverse=reverse, rows=rows, q=q_ref[0, rows, :].astype(jnp.float32), v16=v_ref[0, rows, :],
                    kk=kk_ref[0, rows, :], lf=lf_ref[0, rows, :], tri16=tri_r if reverse else tri_f,
                    tri_ok=ok_r if reverse else ok_f, state=states[reverse], prev=None if si == 0 else len(chains) - 2))
        outs = _hgrn_chunks(chains, head16, half16, pair_key, same_head)
        for c, (o, _) in zip(chains, outs):
            (accr_ref if c["reverse"] else accf_ref)[c["rows"], :] = o
        return outs[-2][1], outs[-1][1]

    zero_state = jnp.zeros((C_WIDTH, C_WIDTH), jnp.float32)
    group = HGRN_STEPS_PER_ITER
    states = lax.fori_loop(0, NBLK // group, lambda j, st: run(tuple(group * j + u for u in range(group)), st),
                           (zero_state, zero_state))
    if NBLK % group:
        run(tuple(range(NBLK - NBLK % group, NBLK)), states)

    w = w_ref[...]

    def finish(n, carry):
        r0 = pl.multiple_of(n * BLOCK, BLOCK)
        y = accf_ref[pl.ds(r0, BLOCK), :] + accr_ref[pl.ds(r0, BLOCK), :]
        sq = y * y
        sq_hi = sq.astype(jnp.bfloat16)
        sq_lo = (sq - sq_hi.astype(jnp.float32)).astype(jnp.bfloat16)
        ms = (jnp.dot(sq_hi, head_ones16, preferred_element_type=jnp.float32)
              + jnp.dot(sq_lo, head_ones16, preferred_element_type=jnp.float32)) * (1.0 / HEAD_DIM)
        y = y * lax.rsqrt(ms + RMS_EPS) * w
        y = y * g_ref[0, pl.ds(r0, BLOCK), :].astype(jnp.float32)
        o_ref[0, pl.ds(r0, BLOCK), :] = y.astype(o_ref.dtype)
        return carry

    lax.fori_loop(0, NBLK, finish, 0)


def _mixer_c(q3, v3, lf3, kk3, g3, norm_w):
    blk = lambda j: pl.BlockSpec((1, LP, C_WIDTH), lambda b: (b, 0, j))
    return pl.pallas_call(
        _mixer_c_kernel,
        grid=(BATCH,),
        in_specs=[
            blk(4),
            blk(0),
            blk(0), blk(1),
            blk(0), blk(1),
            blk(3),
            pl.BlockSpec((1, C_WIDTH), lambda b: (0, 0)),
        ],
        out_specs=pl.BlockSpec((1, LP, C_WIDTH), lambda b: (b, 0, 0)),
        out_shape=jax.ShapeDtypeStruct((BATCH, LP, C_WIDTH), jnp.bfloat16),
        scratch_shapes=[pltpu.VMEM((LP, C_WIDTH), jnp.float32), pltpu.VMEM((LP, C_WIDTH), jnp.float32)],
        compiler_params=_cparams(("arbitrary",)),
        name="mixer_c_hgrn2",
    )(q3, v3, lf3, lf3, kk3, kk3, g3, norm_w.reshape(1, C_WIDTH))


def _outproj_last_kernel(ya_ref, yb_ref, yc_ref, wa_ref, wb_ref, wc_ref, h_ref, g_ref, b_ref, o_ref):
    n_rows = o_ref.shape[1]
    halves = (slice(0, n_rows // 2), slice(n_rows // 2, n_rows))
    accs = []
    for rows in halves:
        acc = jnp.dot(ya_ref[rows, :], wa_ref[...], preferred_element_type=jnp.float32)
        acc = acc + jnp.dot(yb_ref[rows, :], wb_ref[...], preferred_element_type=jnp.float32)
        accs.append(acc + jnp.dot(yc_ref[rows, :], wc_ref[...], preferred_element_type=jnp.float32))
    for rows, acc in zip(halves, accs):
        o_ref[0, rows, :] = _layer_norm(ALPHA * h_ref[rows, :] + acc, g_ref[...], b_ref[...])


def _outproj_last(ya, yb, yc, wa, wb, wc, h32, g, b):
    tl = SEQ // 4
    row = lambda n: pl.BlockSpec((pl.Element(tl), pl.Element(n)),
                                 lambda bi, j: (pl.multiple_of(bi * LP + BLOCK + j * tl, BLOCK), 0))
    full = lambda n: pl.BlockSpec((n, D_MODEL), lambda bi, j: (0, 0))
    vec = pl.BlockSpec((1, D_MODEL), lambda bi, j: (0, 0))
    return pl.pallas_call(
        _outproj_last_kernel,
        grid=(BATCH, SEQ // tl),
        in_specs=[row(A_WIDTH), row(B_WIDTH), row(C_WIDTH), full(A_WIDTH), full(B_WIDTH), full(C_WIDTH),
                  row(D_MODEL), vec, vec],
        out_specs=pl.BlockSpec((1, tl, D_MODEL), lambda bi, j: (bi, j, 0)),
        out_shape=jax.ShapeDtypeStruct((BATCH, SEQ, D_MODEL), jnp.float32),
        compiler_params=_cparams(("arbitrary", "arbitrary")),
        name="outproj_last_residual_ln",
    )(ya.reshape(M_ROWS, A_WIDTH), yb.reshape(M_ROWS, B_WIDTH), yc.reshape(M_ROWS, C_WIDTH), wa, wb, wc, h32,
      g.reshape(1, D_MODEL), b.reshape(1, D_MODEL))


def _regroup_w_in(w):
    def heads(lo, order):
        return [w[:, lo + h * HEAD_DIM: lo + (h + 1) * HEAD_DIM] for h in order]

    c = lambda lo, n: w[:, lo:lo + n]
    o_qa, o_ka, o_va, o_ga = 0, 256, 384, 512
    o_qb, o_kb, o_vb, o_gb = 768, 1280, 1792, 2304
    o_qc, o_ic, o_zf, o_zb, o_gc = 2816, 3072, 3328, 3584, 3840
    cat = lambda xs: jnp.concatenate(xs, axis=1).astype(jnp.bfloat16)
    w_q = cat(heads(o_qa, A_HEAD_ORDER) + [c(o_qb, 512)])
    w_k = cat([c(o_ka, 128), c(o_kb, 512)])
    w_p = cat([c(o_ic, 256), c(o_va, 128), c(o_vb, 512)])
    w_s = cat(heads(o_ga, A_HEAD_ORDER) + [c(o_gb, 512), c(o_gc, 256), c(o_qc, 256)])
    w_g = cat([c(o_zf, 256), c(o_zb, 256)])
    return jnp.concatenate([w_q, w_k, w_p, w_s, w_g], axis=1)


def _rope_tables():
    pos = np.arange(LP, dtype=np.float64) - PAD
    inv = ROPE_THETA ** (-np.arange(0, HEAD_DIM, 2, dtype=np.float64) / HEAD_DIM)
    ang = pos[:, None] * inv[None, :]
    cos, sin = np.cos(ang), np.sin(ang)
    cos_t = np.concatenate([cos, cos, cos, cos], axis=1)
    sin_t = np.concatenate([-sin, sin, -sin, sin], axis=1)
    return jnp.asarray(cos_t, jnp.float32), jnp.asarray(sin_t, jnp.float32)


def kernel(x, meta, emb_ln_g, emb_ln_b, w_in, w_out, a_sink, b_lam, b_subln_w, c_lb_logits, c_norm_w, ln_g, ln_b):
    cos_t, sin_t = _rope_tables()
    p_lb = jax.nn.softmax(c_lb_logits.astype(jnp.float32), axis=0)
    lb_all = jnp.cumsum(p_lb, axis=0) - p_lb[0:1]

    front_args = (x, meta, emb_ln_g, emb_ln_b)
    for l in range(DEPTH):
        lam_init = 0.8 - 0.6 * math.exp(-0.3 * l)
        lp32 = b_lam[l].astype(jnp.float32)
        lam = jnp.exp(jnp.sum(lp32[0] * lp32[1])) - jnp.exp(jnp.sum(lp32[2] * lp32[3])) + lam_init
        wo = w_out[l]
        wa = jnp.concatenate([wo[h * HEAD_DIM:(h + 1) * HEAD_DIM] for h in A_HEAD_ORDER], axis=0).astype(jnp.bfloat16)
        wb = wo[A_WIDTH:A_WIDTH + B_WIDTH].astype(jnp.bfloat16)
        wc = wo[A_WIDTH + B_WIDTH:].astype(jnp.bfloat16)
        sink = jnp.stack([a_sink[l, h] for h in A_HEAD_ORDER]).astype(jnp.float32)

        h32, *proj = _front(front_args, _regroup_w_in(w_in[l]), cos_t, sin_t, lb_all[l].reshape(1, C_WIDTH))
        q_all, k_all, p_all, s_all, lf_all, kk_all = [t.reshape(BATCH, LP, t.shape[-1]) for t in proj]

        ya = _mixer_a(sink, q_all, k_all, p_all, s_all)
        yb = _mixer_b(lam_init, lam.reshape(1), q_all, k_all, p_all, s_all, b_subln_w[l])
        yc = _mixer_c(s_all, p_all, lf_all, kk_all, s_all, c_norm_w[l])

        if l == DEPTH - 1:
            return _outproj_last(ya, yb, yc, wa, wb, wc, h32, ln_g[l], ln_b[l])
        front_args = (ya.reshape(M_ROWS, A_WIDTH), yb.reshape(M_ROWS, B_WIDTH), yc.reshape(M_ROWS, C_WIDTH),
                      wa, wb, wc, h32, ln_g[l], ln_b[l])
```

```python
import functools
import math

import jax
import jax.numpy as jnp
import numpy as np
from jax import lax
from jax.experimental import pallas as pl
from jax.experimental.pallas import tpu as pltpu

D_MODEL = 1024
BATCH = 8
SEQ = 2048
DEPTH = 2
N_META = 16
BLOCK = 128
PAD = BLOCK - N_META
WINDOW = 128
ROPE_THETA = 10000.0
HEAD_DIM = 64
A_HEADS = 4
A_KV_HEADS = 2
A_WIDTH = A_HEADS * HEAD_DIM
B_HEADS = 4
B_V_DIM = 2 * HEAD_DIM
B_WIDTH = B_HEADS * B_V_DIM
C_HEADS = 4
C_WIDTH = C_HEADS * HEAD_DIM
D_MIX = A_WIDTH + B_WIDTH + C_WIDTH
ALPHA = (2 * DEPTH) ** 0.25
LN_EPS = 1e-5
RMS_EPS = 1e-6
NEG = -1e30
F_MIN = 1e-30

LP = SEQ + N_META + PAD
NBLK = LP // BLOCK
M_ROWS = BATCH * LP
TM = LP // 4
LANES = 128
SUBLANES = 8
MXU_N = 256
LOG2E = math.log2(math.e)
Q_SCALE = HEAD_DIM ** -0.5 * LOG2E
Q0_B = 96
TQ_B = (LP - Q0_B) // 5
B_HEADS_PER_STEP = 1
SM_ROWS = 16
VMEM_LIMIT = 48 * 1024 * 1024

N_ROPE_Q = A_WIDTH + 2 * B_HEADS * HEAD_DIM
N_ROPE_K = A_KV_HEADS * HEAD_DIM + 2 * B_HEADS * HEAD_DIM
N_PLAIN = C_WIDTH + A_KV_HEADS * HEAD_DIM + B_WIDTH
N_SILU = A_WIDTH + B_WIDTH + C_WIDTH + C_WIDTH
N_GATE = 2 * C_WIDTH
N_IN = N_ROPE_Q + N_ROPE_K + N_PLAIN + N_SILU + N_GATE
A_HEAD_ORDER = (0, 2, 1, 3)

HGRN_LEVELS = (0, 1, 2, 4, 8, 16, 32, 64)
HGRN_STEPS_PER_ITER = 2


def _cparams(sem, flags=None):
    return pltpu.CompilerParams(dimension_semantics=sem, vmem_limit_bytes=VMEM_LIMIT, flags=flags)


def _layer_norm(x, g, b):
    mu = jnp.mean(x, axis=-1, keepdims=True)
    xc = x - mu
    var = jnp.mean(xc * xc, axis=-1, keepdims=True)
    return xc * lax.rsqrt(var + LN_EPS) * g + b


def _front_embed_kernel(x_ref, meta_ref, g_ref, b_ref, w_ref, cos_ref, sin_ref, lb_ref,
                        h32_ref, q_ref, k_ref, p_ref, s_ref, lf_ref, kk_ref):
    first = pl.program_id(0) % (LP // TM) == 0
    g, b = g_ref[...], b_ref[...]
    half = TM // 2
    pad_meta = jnp.concatenate([jnp.zeros((PAD, D_MODEL), jnp.float32), meta_ref[...]], axis=0)
    top = _layer_norm(jnp.where(first, pad_meta, x_ref[0:BLOCK, :]), g, b)
    pad_row = lax.broadcasted_iota(jnp.int32, (BLOCK, 1), 0) < PAD
    top = jnp.where(jnp.logical_and(first, pad_row), 0.0, top)
    off = pl.multiple_of(jnp.where(first, 0, BLOCK), SUBLANES)
    outs = (q_ref, k_ref, p_ref, s_ref, lf_ref, kk_ref)
    h_a = jnp.concatenate([top, _layer_norm(x_ref[pl.ds(off, half - BLOCK), :], g, b)], axis=0)
    h32_ref[0:half, :] = h_a
    _inproj_body(h_a.astype(jnp.bfloat16), slice(0, half), w_ref, cos_ref, sin_ref, lb_ref, outs)
    h_b = _layer_norm(x_ref[pl.ds(off + (half - BLOCK), TM - half), :], g, b)
    h32_ref[half:TM, :] = h_b
    _inproj_body(h_b.astype(jnp.bfloat16), slice(half, TM), w_ref, cos_ref, sin_ref, lb_ref, outs)


def _front_outproj_kernel(ya_ref, yb_ref, yc_ref, wa_ref, wb_ref, wc_ref, h_ref, g_ref, b_ref,
                          w_ref, cos_ref, sin_ref, lb_ref, h32_ref, q_ref, k_ref, p_ref, s_ref, lf_ref, kk_ref):
    halves = (slice(0, TM // 2), slice(TM // 2, TM))
    accs = []
    for rows in halves:
        acc = jnp.dot(ya_ref[rows, :], wa_ref[...], preferred_element_type=jnp.float32)
        acc = acc + jnp.dot(yb_ref[rows, :], wb_ref[...], preferred_element_type=jnp.float32)
        accs.append(acc + jnp.dot(yc_ref[rows, :], wc_ref[...], preferred_element_type=jnp.float32))
    for rows, acc in zip(halves, accs):
        h = _layer_norm(ALPHA * h_ref[rows, :] + acc, g_ref[...], b_ref[...])
        h32_ref[rows, :] = h
        _inproj_body(h.astype(jnp.bfloat16), rows, w_ref, cos_ref, sin_ref, lb_ref,
                     (q_ref, k_ref, p_ref, s_ref, lf_ref, kk_ref))


def _front(prologue_args, w, cos_t, sin_t, lb):
    row = lambda n: pl.BlockSpec((TM, n), lambda i: (i, 0))
    const = lambda r, n: pl.BlockSpec((r, n), lambda i: (0, 0))
    vec = const(1, D_MODEL)
    tab_spec = pl.BlockSpec((TM, LANES), lambda i: (i % (LP // TM), 0))
    tiles_per_b = LP // TM
    if len(prologue_args) == 4:
        x, meta, g, b = prologue_args
        x_spec = pl.BlockSpec(
            (pl.Element(TM), pl.Element(D_MODEL)),
            lambda i: (pl.multiple_of((i // tiles_per_b) * SEQ + jnp.maximum((i % tiles_per_b) * TM - BLOCK, 0),
                                      SUBLANES), 0))
        body, name = _front_embed_kernel, "front_embed_inproj"
        args = (x.reshape(BATCH * SEQ, D_MODEL), meta, g.reshape(1, D_MODEL), b.reshape(1, D_MODEL))
        in_specs = [x_spec, const(N_META, D_MODEL), vec, vec]
        scratch = []
    else:
        ya, yb, yc, wa, wb, wc, h32, g, b = prologue_args
        body, name = _front_outproj_kernel, "front_outproj_inproj"
        args = (ya, yb, yc, wa, wb, wc, h32, g.reshape(1, D_MODEL), b.reshape(1, D_MODEL))
        in_specs = [row(A_WIDTH), row(B_WIDTH), row(C_WIDTH), const(A_WIDTH, D_MODEL), const(B_WIDTH, D_MODEL),
                    const(C_WIDTH, D_MODEL), row(D_MODEL), vec, vec]
        scratch = []
    widths = (D_MODEL, N_ROPE_Q, N_ROPE_K, N_PLAIN, N_SILU, N_GATE, N_GATE)
    dtypes = (jnp.float32,) + (jnp.bfloat16,) * 4 + (jnp.float32,) * 2
    return pl.pallas_call(
        body,
        grid=(M_ROWS // TM,),
        in_specs=in_specs + [const(D_MODEL, N_IN), tab_spec, tab_spec, const(1, C_WIDTH)],
        out_specs=[row(n) for n in widths],
        out_shape=[jax.ShapeDtypeStruct((M_ROWS, n), dt) for n, dt in zip(widths, dtypes)],
        scratch_shapes=scratch,
        compiler_params=_cparams(("arbitrary",)),
        name=name,
    )(*args, w, cos_t, sin_t, lb)


def _col_chunks(n):
    return [(c, min(MXU_N, n - c)) for c in range(0, n, MXU_N)]


def _rope_cols(acc, cos, sin_signed, first_half):
    partner = jnp.where(first_half, pltpu.roll(acc, LANES - HEAD_DIM // 2, 1), pltpu.roll(acc, HEAD_DIM // 2, 1))
    return acc * cos + partner * sin_signed


def _inproj_body(x, rows, w_ref, cos_ref, sin_ref, lb_ref, outs):
    q_ref, k_ref, p_ref, s_ref, lf_ref, kk_ref = [o.at[rows, :] for o in outs]
    n_rows = rows.stop - rows.start
    cos = cos_ref[rows, :]
    sin = sin_ref[rows, :]
    lane = lax.broadcasted_iota(jnp.int32, (n_rows, LANES), 1)
    first_half = (lane % HEAD_DIM) < (HEAD_DIM // 2)
    row = ((pl.program_id(0) % (LP // TM)) * TM + rows.start
           + lax.broadcasted_iota(jnp.int32, (n_rows, LANES), 0))
    valid = row >= PAD
    pieces = ([("rope_q", q_ref, c) for c in range(0, N_ROPE_Q, LANES)]
              + [("rope_k", k_ref, c) for c in range(0, N_ROPE_K, LANES)]
              + [("plain", p_ref, c) for c in range(0, N_PLAIN, LANES)]
              + [("silu", s_ref, c) for c in range(0, N_SILU, LANES)]
              + [("gate", None, c) for c in range(0, N_GATE, LANES)])
    for c0, cw in _col_chunks(N_IN):
        acc = jnp.dot(x, w_ref[:, c0:c0 + cw], preferred_element_type=jnp.float32)
        for j in range(cw // LANES):
            kind, dst, dc = pieces[c0 // LANES + j]
            a = acc[:, j * LANES:(j + 1) * LANES]
            if kind == "rope_q":
                dst[:, dc:dc + LANES] = (_rope_cols(a, cos, sin, first_half) * Q_SCALE).astype(dst.dtype)
            elif kind == "rope_k":
                dst[:, dc:dc + LANES] = _rope_cols(a, cos, sin, first_half).astype(dst.dtype)
            elif kind == "plain":
                dst[:, dc:dc + LANES] = a.astype(dst.dtype)
            elif kind == "silu":
                dst[:, dc:dc + LANES] = (a * jax.nn.sigmoid(a)).astype(dst.dtype)
            else:
                lb = lb_ref[:, dc % C_WIDTH: dc % C_WIDTH + LANES]
                f = lb + (1.0 - lb) * jax.nn.sigmoid(a)
                lf_ref[:, dc:dc + LANES] = jnp.where(valid, jnp.log2(jnp.maximum(f, F_MIN)), 0.0)
                kk_ref[:, dc:dc + LANES] = jnp.where(valid, (1.0 - lb) * jax.nn.sigmoid(-a), 0.0)


def _mixer_a_kernel(sink_ref, q_ref, k_ref, v_ref, g_ref, o_ref):
    two = 2 * BLOCK
    row = lax.broadcasted_iota(jnp.int32, (two, 4 * BLOCK), 0) % BLOCK
    col = lax.broadcasted_iota(jnp.int32, (two, 4 * BLOCK), 1)
    is_meta = (col < BLOCK) & (col >= PAD)
    top = lax.broadcasted_iota(jnp.int32, (two, 1), 0) < BLOCK
    lane_hi = lax.broadcasted_iota(jnp.int32, (BLOCK, LANES), 1) >= HEAD_DIM
    masks = {}

    def band_start(n):
        return min(max((n - 1) * BLOCK, 0), LP - 3 * BLOCK)

    def scores(n):
        start = band_start(n)
        key = (n * BLOCK - start, start < BLOCK)
        if key not in masks:
            kpos = start + col - BLOCK
            band = (col >= BLOCK) & (jnp.abs(n * BLOCK + row - kpos) <= WINDOW)
            if start < BLOCK:
                band = band & (kpos >= BLOCK)
            masks[key] = is_meta | band
        kcat = jnp.concatenate([k_ref[0, 0:BLOCK, :], k_ref[0, start:start + 3 * BLOCK, :]], axis=0)
        qcols = [q_ref[0, n * BLOCK:(n + 1) * BLOCK, cv * LANES:(cv + 1) * LANES] for cv in range(2)]
        out = []
        for hf in range(2):
            sel = lane_hi if hf else jnp.logical_not(lane_hi)
            q2 = jnp.concatenate([jnp.where(sel, qc, jnp.zeros_like(qc)) for qc in qcols], axis=0)
            s = lax.dot_general(q2, kcat, (((1,), (1,)), ((), ())), preferred_element_type=jnp.float32)
            out.append(jnp.where(masks[key], s, NEG))
        return out

    def softmax_pv(n, s_halves):
        start = band_start(n)
        vcat = jnp.concatenate([v_ref[0, 0:BLOCK, :], v_ref[0, start:start + 3 * BLOCK, :]], axis=0)
        rows = slice(n * BLOCK, (n + 1) * BLOCK)
        outs = []
        for hf, s in enumerate(s_halves):
            sink = jnp.where(top, sink_ref[hf], sink_ref[2 + hf]) * LOG2E
            m = jnp.maximum(jnp.max(s, axis=-1, keepdims=True), sink)
            e = jnp.exp2(s - m)
            den = jnp.sum(e, axis=-1, keepdims=True) + jnp.exp2(sink - m)
            p = (e * (1.0 / den)).astype(jnp.bfloat16)
            outs.append(jnp.dot(p, vcat, preferred_element_type=jnp.float32))
        for cv in range(2):
            o = jnp.where(lane_hi, outs[1][cv * BLOCK:(cv + 1) * BLOCK], outs[0][cv * BLOCK:(cv + 1) * BLOCK])
            y = o * g_ref[0, rows, cv * LANES:(cv + 1) * LANES].astype(jnp.float32)
            o_ref[0, rows, cv * LANES:(cv + 1) * LANES] = y.astype(o_ref.dtype)

    ahead = 2
    pending = {}
    for n in range(NBLK + ahead):
        if n < NBLK:
            pending[n] = scores(n)
        if n >= ahead:
            softmax_pv(n - ahead, pending.pop(n - ahead))


def _mixer_a(sink, q3, k3, v3, g3):
    return pl.pallas_call(
        _mixer_a_kernel,
        grid=(BATCH,),
        in_specs=[
            pl.BlockSpec(memory_space=pltpu.SMEM),
            pl.BlockSpec((1, LP, A_WIDTH), lambda b: (b, 0, 0)),
            pl.BlockSpec((1, LP, LANES), lambda b: (b, 0, 0)),
            pl.BlockSpec((1, LP, LANES), lambda b: (b, 0, C_WIDTH // LANES)),
            pl.BlockSpec((1, LP, A_WIDTH), lambda b: (b, 0, 0)),
        ],
        out_specs=pl.BlockSpec((1, LP, A_WIDTH), lambda b: (b, 0, 0)),
        out_shape=jax.ShapeDtypeStruct((BATCH, LP, A_WIDTH), jnp.bfloat16),
        compiler_params=_cparams(("arbitrary",)),
        name="mixer_a_window_gqa",
    )(sink, q3, k3, v3, g3)


def _mixer_b_kernel(lam_init, lam_ref, *refs):
    hps = B_HEADS_PER_STEP
    q_refs, k_refs, v_refs, g_refs = (refs[i * hps:(i + 1) * hps] for i in range(4))
    w_ref, o_ref, s_a, s_b, a_a, a_b = refs[4 * hps:]
    lam = lam_ref[0]
    lane_hi = lax.broadcasted_iota(jnp.int32, (TQ_B, LANES), 1) >= HEAD_DIM
    key_ok = lax.broadcasted_iota(jnp.int32, (TQ_B, BLOCK), 1) >= PAD
    s_bufs, a_bufs = (s_a, s_b), (a_a, a_b)
    nt = (LP - Q0_B) // TQ_B
    nt_dims = (((1,), (1,)), ((), ()))
    tile_rows = lambda t: slice(Q0_B + t * TQ_B, Q0_B + (t + 1) * TQ_B)
    o_ref[0, 0:Q0_B, :] = jnp.zeros((Q0_B, hps * LANES), o_ref.dtype)
    stages = [(hh, t) for hh in range(hps) for t in range(nt)]

    def scores(i):
        hh, t = stages[i]
        q_ref, k_ref = q_refs[hh], k_refs[hh]
        s_buf = s_bufs[i % 2]
        q = q_ref[0, tile_rows(t), :]
        for c in range(2):
            qm = jnp.where(lane_hi if c else jnp.logical_not(lane_hi), q, jnp.zeros_like(q))
            s0 = lax.dot_general(qm, k_ref[0, 0:BLOCK, :], nt_dims, preferred_element_type=jnp.float32)
            s_buf[c, :, 0:BLOCK] = jnp.where(key_ok, s0, NEG)
            s_buf[c, :, BLOCK:LP] = lax.dot_general(qm, k_ref[0, BLOCK:LP, :], nt_dims,
                                                    preferred_element_type=jnp.float32)

    def softmax_pv(i):
        hh, t = stages[i]
        v_ref, g_ref = v_refs[hh], g_refs[hh]
        cols = slice(hh * LANES, (hh + 1) * LANES)
        w = w_ref[:, cols] * (1.0 - lam_init)
        s_buf, a_buf = s_bufs[i % 2], a_bufs[i % 2]
        inv_den0 = []
        for r in range(TQ_B // SM_ROWS):
            rows = slice(r * SM_ROWS, (r + 1) * SM_ROWS)
            es, dens = [], []
            for c in range(2):
                s = s_buf[c, rows, :]
                m = jnp.max(s, axis=-1, keepdims=True)
                e = jnp.exp2(s - m)
                es.append(e)
                dens.append(jnp.sum(e, axis=-1, keepdims=True))
            a_buf[rows, :] = (es[0] - es[1] * (dens[0] * lam / dens[1])).astype(jnp.bfloat16)
            inv_den0.append(1.0 / dens[0])
        o = jnp.dot(a_buf[...], v_ref[0], preferred_element_type=jnp.float32)
        o = o * jnp.concatenate(inv_den0, axis=0)
        ms = jnp.mean(o * o, axis=-1, keepdims=True)
        y = o * lax.rsqrt(ms + RMS_EPS) * w
        y = y * g_ref[0, tile_rows(t), :].astype(jnp.float32)
        o_ref[0, tile_rows(t), cols] = y.astype(o_ref.dtype)

    for i in range(len(stages) + 1):
        if i < len(stages):
            scores(i)
        if i >= 1:
            softmax_pv(i - 1)


def _mixer_b(lam_init, lam, q3, k3, v3, g3, subln_w):
    qoff = A_WIDTH // LANES
    koff = (A_KV_HEADS * HEAD_DIM) // LANES
    voff = (C_WIDTH + A_KV_HEADS * HEAD_DIM) // LANES
    goff = A_WIDTH // LANES
    hps = B_HEADS_PER_STEP
    head_blocks = lambda off: [pl.BlockSpec((1, LP, LANES), lambda b, hp, hh=hh: (b, 0, off + hp * hps + hh))
                               for hh in range(hps)]
    return pl.pallas_call(
        functools.partial(_mixer_b_kernel, lam_init),
        grid=(BATCH, B_HEADS // hps),
        in_specs=([pl.BlockSpec(memory_space=pltpu.SMEM)]
                  + head_blocks(qoff) + head_blocks(koff) + head_blocks(voff) + head_blocks(goff)
                  + [pl.BlockSpec((1, hps * B_V_DIM), lambda b, hp: (0, hp))]),
        out_specs=pl.BlockSpec((1, LP, hps * LANES), lambda b, hp: (b, 0, hp)),
        out_shape=jax.ShapeDtypeStruct((BATCH, LP, B_WIDTH), jnp.bfloat16),
        scratch_shapes=[pltpu.VMEM((2, TQ_B, LP), jnp.float32), pltpu.VMEM((2, TQ_B, LP), jnp.float32),
                        pltpu.VMEM((TQ_B, LP), jnp.bfloat16), pltpu.VMEM((TQ_B, LP), jnp.bfloat16)],
        compiler_params=_cparams(("arbitrary", "arbitrary")),
        name="mixer_b_diff_attn",
    )(lam, *([q3] * hps), *([k3] * hps), *([v3] * hps), *([g3] * hps), subln_w.reshape(1, B_WIDTH))


def _split2(x):
    hi = x.astype(jnp.bfloat16)
    lo = (x - hi.astype(jnp.float32)).astype(jnp.bfloat16)
    return hi, lo


def _hgrn_anchor(b, m, reverse):
    a_off = m if reverse else m - 1
    if 2 * m >= SUBLANES:
        b3 = b.reshape(BLOCK // (2 * m), 2 * m, C_WIDTH)
        return jnp.broadcast_to(b3[:, a_off:a_off + 1, :], b3.shape).reshape(BLOCK, C_WIDTH)
    b3 = b.reshape(BLOCK // SUBLANES, SUBLANES, C_WIDTH)
    sub = lax.broadcasted_iota(jnp.int32, b3.shape, 1)
    anchor = None
    for blk in range(SUBLANES // (2 * m)):
        r = blk * 2 * m + a_off
        cand = jnp.broadcast_to(b3[:, r:r + 1, :], b3.shape)
        anchor = cand if anchor is None else jnp.where(sub >= blk * 2 * m, cand, anchor)
    return anchor.reshape(BLOCK, C_WIDTH)


def _hgrn_chunks(chains, head16, half16, pair_key, same_head):
    nt_dims = (((1,), (1,)), ((), ()))
    f32 = jnp.float32
    for c in chains:
        hi, lo = _split2(c["lf"])
        c["b"] = (jnp.dot(c["tri16"], hi, preferred_element_type=f32)
                  + jnp.dot(c["tri16"], lo, preferred_element_type=f32))
    for c in chains:
        state = c["state"] if c["prev"] is None else chains[c["prev"]]["new_state"]
        edge = 0 if c["reverse"] else BLOCK - 1
        b, b_all = c["b"], c["b"][edge:edge + 1, :]
        q_in = (c["q"] * jnp.exp2(b)).astype(jnp.bfloat16)
        c["o_inter"] = lax.dot_general(q_in, state.astype(jnp.bfloat16), nt_dims, preferred_element_type=f32)
        k_out = (c["kk"] * jnp.exp2(b_all - b)).astype(jnp.bfloat16)
        u_t = lax.dot_general(c["v16"], k_out, (((0,), (0,)), ((), ())), preferred_element_type=f32)
        c["new_state"] = jnp.exp2(b_all) * state + jnp.where(same_head, u_t, 0.0)
        c["p"] = None
    for m in reversed(HGRN_LEVELS):
        for c in chains:
            if "q16" not in c:
                cols = [slice(col * LANES, (col + 1) * LANES) for col in range(C_WIDTH // LANES)]
                c["q16"] = [c["q"][:, cs].astype(jnp.bfloat16) for cs in cols]
                c["k16"] = [[c["kk"][:, cs].astype(jnp.bfloat16) * half16[hf] for hf in range(2)] for cs in cols]
            parts = []
            if m != 0:
                wgt = jnp.exp2(-jnp.abs(c["b"] - _hgrn_anchor(c["b"], m, c["reverse"])))
            for col in range(C_WIDTH // LANES):
                q_col, k_pair = c["q16"][col], c["k16"][col]
                if m != 0:
                    w16 = wgt[:, col * LANES:(col + 1) * LANES].astype(jnp.bfloat16)
                    q_col, k_pair = q_col * w16, [k * w16 for k in k_pair]
                k_two = jnp.concatenate(k_pair, axis=0)
                parts.append(lax.dot_general(q_col, k_two, nt_dims, preferred_element_type=f32))
            pm = jnp.concatenate(parts, axis=1)
            c["p"] = pm if c["p"] is None else jnp.where(pair_key < max(2 * m, 1), pm, c["p"])
    outs = []
    for c in chains:
        p = jnp.where(c["tri_ok"], c["p"], 0.0).astype(jnp.bfloat16)
        v_heads = jnp.concatenate([c["v16"] * head16[h] for h in range(C_HEADS)], axis=0)
        outs.append((jnp.dot(p, v_heads, preferred_element_type=f32) + c["o_inter"], c["new_state"]))
    return outs


def _mixer_c_kernel(q_ref, v_ref, lff_ref, lfb_ref, kkf_ref, kkb_ref, g_ref, w_ref, o_ref, accf_ref, accr_ref):
    lane_head = lax.broadcasted_iota(jnp.int32, (BLOCK, C_WIDTH), 1) // HEAD_DIM
    head16 = [(lane_head == h).astype(jnp.bfloat16) for h in range(C_HEADS)]
    lane_half = lax.broadcasted_iota(jnp.int32, (BLOCK, LANES), 1) // HEAD_DIM
    half16 = [(lane_half == hf).astype(jnp.bfloat16) for hf in range(2)]
    sr = lax.broadcasted_iota(jnp.int32, (C_WIDTH, C_WIDTH), 0) // HEAD_DIM
    sc = lax.broadcasted_iota(jnp.int32, (C_WIDTH, C_WIDTH), 1) // HEAD_DIM
    same_head = sr == sc
    head_ones16 = same_head.astype(jnp.bfloat16)
    t_i = lax.broadcasted_iota(jnp.int32, (BLOCK, C_HEADS * BLOCK), 0)
    s_i = lax.broadcasted_iota(jnp.int32, (BLOCK, C_HEADS * BLOCK), 1) % BLOCK
    pair_key = t_i ^ s_i
    r_i = lax.broadcasted_iota(jnp.int32, (BLOCK, BLOCK), 0)
    c_i = lax.broadcasted_iota(jnp.int32, (BLOCK, BLOCK), 1)
    tri_f, tri_r = (c_i <= r_i).astype(jnp.bfloat16), (c_i >= r_i).astype(jnp.bfloat16)
    ok_f, ok_r = s_i <= t_i, s_i >= t_i

    def run(steps, states):
        chains = []
        for si, step in enumerate(steps):
            for reverse in (False, True):
                n = (NBLK - 1 - step) if reverse else step
                r0 = n * BLOCK if isinstance(n, int) else pl.multiple_of(n * BLOCK, BLOCK)
                rows = pl.ds(r0, BLOCK)
                lf_ref, kk_ref = (lfb_ref, kkb_ref) if reverse else (lff_ref, kkf_ref)
                chains.append(dict(
                    reverse=reverse, rows=rows, q=q_ref[0, rows, :].astype(jnp.float32), v16=v_ref[0, rows, :],
                    kk=kk_ref[0, rows, :], lf=lf_ref[0, rows, :], tri16=tri_r if reverse else tri_f,
                    tri_ok=ok_r if reverse else ok_f, state=states[reverse], prev=None if si == 0 else len(chains) - 2))
        outs = _hgrn_chunks(chains, head16, half16, pair_key, same_head)
        for c, (o, _) in zip(chains, outs):
            (accr_ref if c["reverse"] else accf_ref)[c["rows"], :] = o
        return outs[-2][1], outs[-1][1]

    zero_state = jnp.zeros((C_WIDTH, C_WIDTH), jnp.float32)
    group = HGRN_STEPS_PER_ITER
    states = lax.fori_loop(0, NBLK // group, lambda j, st: run(tuple(group * j + u for u in range(group)), st),
                           (zero_state, zero_state))
    if NBLK % group:
        run(tuple(range(NBLK - NBLK % group, NBLK)), states)

    w = w_ref[...]

    def finish(n, carry):
        r0 = pl.multiple_of(n * BLOCK, BLOCK)
        y = accf_ref[pl.ds(r0, BLOCK), :] + accr_ref[pl.ds(r0, BLOCK), :]
        sq = y * y
        sq_hi = sq.astype(jnp.bfloat16)
        sq_lo = (sq - sq_hi.astype(jnp.float32)).astype(jnp.bfloat16)
        ms = (jnp.dot(sq_hi, head_ones16, preferred_element_type=jnp.float32)
              + jnp.dot(sq_lo, head_ones16, preferred_element_type=jnp.float32)) * (1.0 / HEAD_DIM)
        y = y * lax.rsqrt(ms + RMS_EPS) * w
        y = y * g_ref[0, pl.ds(r0, BLOCK), :].astype(jnp.float32)
        o_ref[0, pl.ds(r0, BLOCK), :] = y.astype(o_ref.dtype)
        return carry

    lax.fori_loop(0, NBLK, finish, 0)


def _mixer_c(q3, v3, lf3, kk3, g3, norm_w):
    blk = lambda j: pl.BlockSpec((1, LP, C_WIDTH), lambda b: (b, 0, j))
    return pl.pallas_call(
        _mixer_c_kernel,
        grid=(BATCH,),
        in_specs=[
            blk(4),
            blk(0),
            blk(0), blk(1),
            blk(0), blk(1),
            blk(3),
            pl.BlockSpec((1, C_WIDTH), lambda b: (0, 0)),
        ],
        out_specs=pl.BlockSpec((1, LP, C_WIDTH), lambda b: (b, 0, 0)),
        out_shape=jax.ShapeDtypeStruct((BATCH, LP, C_WIDTH), jnp.bfloat16),
        scratch_shapes=[pltpu.VMEM((LP, C_WIDTH), jnp.float32), pltpu.VMEM((LP, C_WIDTH), jnp.float32)],
        compiler_params=_cparams(("arbitrary",)),
        name="mixer_c_hgrn2",
    )(q3, v3, lf3, lf3, kk3, kk3, g3, norm_w.reshape(1, C_WIDTH))


def _outproj_last_kernel(ya_ref, yb_ref, yc_ref, wa_ref, wb_ref, wc_ref, h_ref, g_ref, b_ref, o_ref):
    n_rows = o_ref.shape[1]
    halves = (slice(0, n_rows // 2), slice(n_rows // 2, n_rows))
    accs = []
    for rows in halves:
        acc = jnp.dot(ya_ref[rows, :], wa_ref[...], preferred_element_type=jnp.float32)
        acc = acc + jnp.dot(yb_ref[rows, :], wb_ref[...], preferred_element_type=jnp.float32)
        accs.append(acc + jnp.dot(yc_ref[rows, :], wc_ref[...], preferred_element_type=jnp.float32))
    for rows, acc in zip(halves, accs):
        o_ref[0, rows, :] = _layer_norm(ALPHA * h_ref[rows, :] + acc, g_ref[...], b_ref[...])


def _outproj_last(ya, yb, yc, wa, wb, wc, h32, g, b):
    tl = SEQ // 2
    row = lambda n: pl.BlockSpec((pl.Element(tl), pl.Element(n)),
                                 lambda bi, j: (pl.multiple_of(bi * LP + BLOCK + j * tl, BLOCK), 0))
    full = lambda n: pl.BlockSpec((n, D_MODEL), lambda bi, j: (0, 0))
    vec = pl.BlockSpec((1, D_MODEL), lambda bi, j: (0, 0))
    return pl.pallas_call(
        _outproj_last_kernel,
        grid=(BATCH, SEQ // tl),
        in_specs=[row(A_WIDTH), row(B_WIDTH), row(C_WIDTH), full(A_WIDTH), full(B_WIDTH), full(C_WIDTH),
                  row(D_MODEL), vec, vec],
        out_specs=pl.BlockSpec((1, tl, D_MODEL), lambda bi, j: (bi, j, 0)),
        out_shape=jax.ShapeDtypeStruct((BATCH, SEQ, D_MODEL), jnp.float32),
        compiler_params=_cparams(("arbitrary", "arbitrary")),
        name="outproj_last_residual_ln",
    )(ya.reshape(M_ROWS, A_WIDTH), yb.reshape(M_ROWS, B_WIDTH), yc.reshape(M_ROWS, C_WIDTH), wa, wb, wc, h32,
      g.reshape(1, D_MODEL), b.reshape(1, D_MODEL))


def _regroup_w_in(w):
    def heads(lo, order):
        return [w[:, lo + h * HEAD_DIM: lo + (h + 1) * HEAD_DIM] for h in order]

    c = lambda lo, n: w[:, lo:lo + n]
    o_qa, o_ka, o_va, o_ga = 0, 256, 384, 512
    o_qb, o_kb, o_vb, o_gb = 768, 1280, 1792, 2304
    o_qc, o_ic, o_zf, o_zb, o_gc = 2816, 3072, 3328, 3584, 3840
    cat = lambda xs: jnp.concatenate(xs, axis=1).astype(jnp.bfloat16)
    w_q = cat(heads(o_qa, A_HEAD_ORDER) + [c(o_qb, 512)])
    w_k = cat([c(o_ka, 128), c(o_kb, 512)])
    w_p = cat([c(o_ic, 256), c(o_va, 128), c(o_vb, 512)])
    w_s = cat(heads(o_ga, A_HEAD_ORDER) + [c(o_gb, 512), c(o_gc, 256), c(o_qc, 256)])
    w_g = cat([c(o_zf, 256), c(o_zb, 256)])
    return jnp.concatenate([w_q, w_k, w_p, w_s, w_g], axis=1)


def _rope_tables():
    pos = np.arange(LP, dtype=np.float64) - PAD
    inv = ROPE_THETA ** (-np.arange(0, HEAD_DIM, 2, dtype=np.float64) / HEAD_DIM)
    ang = pos[:, None] * inv[None, :]
    cos, sin = np.cos(ang), np.sin(ang)
    cos_t = np.concatenate([cos, cos, cos, cos], axis=1)
    sin_t = np.concatenate([-sin, sin, -sin, sin], axis=1)
    return jnp.asarray(cos_t, jnp.float32), jnp.asarray(sin_t, jnp.float32)


def kernel(x, meta, emb_ln_g, emb_ln_b, w_in, w_out, a_sink, b_lam, b_subln_w, c_lb_logits, c_norm_w, ln_g, ln_b):
    cos_t, sin_t = _rope_tables()
    p_lb = jax.nn.softmax(c_lb_logits.astype(jnp.float32), axis=0)
    lb_all = jnp.cumsum(p_lb, axis=0) - p_lb[0:1]

    front_args = (x, meta, emb_ln_g, emb_ln_b)
    for l in range(DEPTH):
        lam_init = 0.8 - 0.6 * math.exp(-0.3 * l)
        lp32 = b_lam[l].astype(jnp.float32)
        lam = jnp.exp(jnp.sum(lp32[0] * lp32[1])) - jnp.exp(jnp.sum(lp32[2] * lp32[3])) + lam_init
        wo = w_out[l]
        wa = jnp.concatenate([wo[h * HEAD_DIM:(h + 1) * HEAD_DIM] for h in A_HEAD_ORDER], axis=0).astype(jnp.bfloat16)
        wb = wo[A_WIDTH:A_WIDTH + B_WIDTH].astype(jnp.bfloat16)
        wc = wo[A_WIDTH + B_WIDTH:].astype(jnp.bfloat16)
        sink = jnp.stack([a_sink[l, h] for h in A_HEAD_ORDER]).astype(jnp.float32)

        h32, *proj = _front(front_args, _regroup_w_in(w_in[l]), cos_t, sin_t, lb_all[l].reshape(1, C_WIDTH))
        q_all, k_all, p_all, s_all, lf_all, kk_all = [t.reshape(BATCH, LP, t.shape[-1]) for t in proj]

        ya = _mixer_a(sink, q_all, k_all, p_all, s_all)
        yb = _mixer_b(lam_init, lam.reshape(1), q_all, k_all, p_all, s_all, b_subln_w[l])
        yc = _mixer_c(s_all, p_all, lf_all, kk_all, s_all, c_norm_w[l])

        if l == DEPTH - 1:
            return _outproj_last(ya, yb, yc, wa, wb, wc, h32, ln_g[l], ln_b[l])
        front_args = (ya.reshape(M_ROWS, A_WIDTH), yb.reshape(M_ROWS, B_WIDTH), yc.reshape(M_ROWS, C_WIDTH),
                      wa, wb, wc, h32, ln_g[l], ln_b[l])
```

```python
import functools
import math

import jax
import jax.numpy as jnp
import numpy as np
from jax import lax
from jax.experimental import pallas as pl
from jax.experimental.pallas import tpu as pltpu

D_MODEL = 1024
BATCH = 8
SEQ = 2048
DEPTH = 2
N_META = 16
BLOCK = 128
PAD = BLOCK - N_META
WINDOW = 128
ROPE_THETA = 10000.0
HEAD_DIM = 64
A_HEADS = 4
A_KV_HEADS = 2
A_WIDTH = A_HEADS * HEAD_DIM
B_HEADS = 4
B_V_DIM = 2 * HEAD_DIM
B_WIDTH = B_HEADS * B_V_DIM
C_HEADS = 4
C_WIDTH = C_HEADS * HEAD_DIM
D_MIX = A_WIDTH + B_WIDTH + C_WIDTH
ALPHA = (2 * DEPTH) ** 0.25
LN_EPS = 1e-5
RMS_EPS = 1e-6
NEG = -1e30
F_MIN = 1e-30

LP = SEQ + N_META + PAD
NBLK = LP // BLOCK
M_ROWS = BATCH * LP
TM = LP // 4
LANES = 128
SUBLANES = 8
MXU_N = 256
LOG2E = math.log2(math.e)
Q_SCALE = HEAD_DIM ** -0.5 * LOG2E
Q0_B = 96
TQ_B = (LP - Q0_B) // 5
B_HEADS_PER_STEP = 1
SM_ROWS = 32
VMEM_LIMIT = 48 * 1024 * 1024

N_ROPE_Q = A_WIDTH + 2 * B_HEADS * HEAD_DIM
N_ROPE_K = A_KV_HEADS * HEAD_DIM + 2 * B_HEADS * HEAD_DIM
N_PLAIN = C_WIDTH + A_KV_HEADS * HEAD_DIM + B_WIDTH
N_SILU = A_WIDTH + B_WIDTH + C_WIDTH + C_WIDTH
N_GATE = 2 * C_WIDTH
N_IN = N_ROPE_Q + N_ROPE_K + N_PLAIN + N_SILU + N_GATE
A_HEAD_ORDER = (0, 2, 1, 3)

HGRN_LEVELS = (0, 1, 2, 4, 8, 16, 32, 64)
HGRN_STEPS_PER_ITER = 2


def _cparams(sem, flags=None):
    return pltpu.CompilerParams(dimension_semantics=sem, vmem_limit_bytes=VMEM_LIMIT, flags=flags)


def _layer_norm(x, g, b):
    mu = jnp.mean(x, axis=-1, keepdims=True)
    xc = x - mu
    var = jnp.mean(xc * xc, axis=-1, keepdims=True)
    return xc * lax.rsqrt(var + LN_EPS) * g + b


def _front_embed_kernel(x_ref, meta_ref, g_ref, b_ref, w_ref, cos_ref, sin_ref, lb_ref,
                        h32_ref, q_ref, k_ref, p_ref, s_ref, lf_ref, kk_ref):
    first = pl.program_id(0) % (LP // TM) == 0
    g, b = g_ref[...], b_ref[...]
    half = TM // 2
    pad_meta = jnp.concatenate([jnp.zeros((PAD, D_MODEL), jnp.float32), meta_ref[...]], axis=0)
    top = _layer_norm(jnp.where(first, pad_meta, x_ref[0:BLOCK, :]), g, b)
    pad_row = lax.broadcasted_iota(jnp.int32, (BLOCK, 1), 0) < PAD
    top = jnp.where(jnp.logical_and(first, pad_row), 0.0, top)
    off = pl.multiple_of(jnp.where(first, 0, BLOCK), SUBLANES)
    outs = (q_ref, k_ref, p_ref, s_ref, lf_ref, kk_ref)
    h_a = jnp.concatenate([top, _layer_norm(x_ref[pl.ds(off, half - BLOCK), :], g, b)], axis=0)
    h32_ref[0:half, :] = h_a
    _inproj_body(h_a.astype(jnp.bfloat16), slice(0, half), w_ref, cos_ref, sin_ref, lb_ref, outs)
    h_b = _layer_norm(x_ref[pl.ds(off + (half - BLOCK), TM - half), :], g, b)
    h32_ref[half:TM, :] = h_b
    _inproj_body(h_b.astype(jnp.bfloat16), slice(half, TM), w_ref, cos_ref, sin_ref, lb_ref, outs)


def _front_outproj_kernel(ya_ref, yb_ref, yc_ref, wa_ref, wb_ref, wc_ref, h_ref, g_ref, b_ref,
                          w_ref, cos_ref, sin_ref, lb_ref, h32_ref, q_ref, k_ref, p_ref, s_ref, lf_ref, kk_ref):
    halves = (slice(0, TM // 2), slice(TM // 2, TM))
    accs = []
    for rows in halves:
        acc = jnp.dot(ya_ref[rows, :], wa_ref[...], preferred_element_type=jnp.float32)
        acc = acc + jnp.dot(yb_ref[rows, :], wb_ref[...], preferred_element_type=jnp.float32)
        accs.append(acc + jnp.dot(yc_ref[rows, :], wc_ref[...], preferred_element_type=jnp.float32))
    for rows, acc in zip(halves, accs):
        h = _layer_norm(ALPHA * h_ref[rows, :] + acc, g_ref[...], b_ref[...])
        h32_ref[rows, :] = h
        _inproj_body(h.astype(jnp.bfloat16), rows, w_ref, cos_ref, sin_ref, lb_ref,
                     (q_ref, k_ref, p_ref, s_ref, lf_ref, kk_ref))


def _front(prologue_args, w, cos_t, sin_t, lb):
    row = lambda n: pl.BlockSpec((TM, n), lambda i: (i, 0))
    const = lambda r, n: pl.BlockSpec((r, n), lambda i: (0, 0))
    vec = const(1, D_MODEL)
    tab_spec = pl.BlockSpec((TM, LANES), lambda i: (i % (LP // TM), 0))
    tiles_per_b = LP // TM
    if len(prologue_args) == 4:
        x, meta, g, b = prologue_args
        x_spec = pl.BlockSpec(
            (pl.Element(TM), pl.Element(D_MODEL)),
            lambda i: (pl.multiple_of((i // tiles_per_b) * SEQ + jnp.maximum((i % tiles_per_b) * TM - BLOCK, 0),
                                      SUBLANES), 0))
        body, name = _front_embed_kernel, "front_embed_inproj"
        args = (x.reshape(BATCH * SEQ, D_MODEL), meta, g.reshape(1, D_MODEL), b.reshape(1, D_MODEL))
        in_specs = [x_spec, const(N_META, D_MODEL), vec, vec]
        scratch = []
    else:
        ya, yb, yc, wa, wb, wc, h32, g, b = prologue_args
        body, name = _front_outproj_kernel, "front_outproj_inproj"
        args = (ya, yb, yc, wa, wb, wc, h32, g.reshape(1, D_MODEL), b.reshape(1, D_MODEL))
        in_specs = [row(A_WIDTH), row(B_WIDTH), row(C_WIDTH), const(A_WIDTH, D_MODEL), const(B_WIDTH, D_MODEL),
                    const(C_WIDTH, D_MODEL), row(D_MODEL), vec, vec]
        scratch = []
    widths = (D_MODEL, N_ROPE_Q, N_ROPE_K, N_PLAIN, N_SILU, N_GATE, N_GATE)
    dtypes = (jnp.float32,) + (jnp.bfloat16,) * 4 + (jnp.float32,) * 2
    return pl.pallas_call(
        body,
        grid=(M_ROWS // TM,),
        in_specs=in_specs + [const(D_MODEL, N_IN), tab_spec, tab_spec, const(1, C_WIDTH)],
        out_specs=[row(n) for n in widths],
        out_shape=[jax.ShapeDtypeStruct((M_ROWS, n), dt) for n, dt in zip(widths, dtypes)],
        scratch_shapes=scratch,
        compiler_params=_cparams(("arbitrary",)),
        name=name,
    )(*args, w, cos_t, sin_t, lb)


def _col_chunks(n):
    return [(c, min(MXU_N, n - c)) for c in range(0, n, MXU_N)]


def _rope_cols(acc, cos, sin_signed, first_half):
    partner = jnp.where(first_half, pltpu.roll(acc, LANES - HEAD_DIM // 2, 1), pltpu.roll(acc, HEAD_DIM // 2, 1))
    return acc * cos + partner * sin_signed


def _inproj_body(x, rows, w_ref, cos_ref, sin_ref, lb_ref, outs):
    q_ref, k_ref, p_ref, s_ref, lf_ref, kk_ref = [o.at[rows, :] for o in outs]
    n_rows = rows.stop - rows.start
    cos = cos_ref[rows, :]
    sin = sin_ref[rows, :]
    lane = lax.broadcasted_iota(jnp.int32, (n_rows, LANES), 1)
    first_half = (lane % HEAD_DIM) < (HEAD_DIM // 2)
    row = ((pl.program_id(0) % (LP // TM)) * TM + rows.start
           + lax.broadcasted_iota(jnp.int32, (n_rows, LANES), 0))
    valid = row >= PAD
    pieces = ([("rope_q", q_ref, c) for c in range(0, N_ROPE_Q, LANES)]
              + [("rope_k", k_ref, c) for c in range(0, N_ROPE_K, LANES)]
              + [("plain", p_ref, c) for c in range(0, N_PLAIN, LANES)]
              + [("silu", s_ref, c) for c in range(0, N_SILU, LANES)]
              + [("gate", None, c) for c in range(0, N_GATE, LANES)])
    for c0, cw in _col_chunks(N_IN):
        acc = jnp.dot(x, w_ref[:, c0:c0 + cw], preferred_element_type=jnp.float32)
        for j in range(cw // LANES):
            kind, dst, dc = pieces[c0 // LANES + j]
            a = acc[:, j * LANES:(j + 1) * LANES]
            if kind == "rope_q":
                dst[:, dc:dc + LANES] = (_rope_cols(a, cos, sin, first_half) * Q_SCALE).astype(dst.dtype)
            elif kind == "rope_k":
                dst[:, dc:dc + LANES] = _rope_cols(a, cos, sin, first_half).astype(dst.dtype)
            elif kind == "plain":
                dst[:, dc:dc + LANES] = a.astype(dst.dtype)
            elif kind == "silu":
                dst[:, dc:dc + LANES] = (a * jax.nn.sigmoid(a)).astype(dst.dtype)
            else:
                lb = lb_ref[:, dc % C_WIDTH: dc % C_WIDTH + LANES]
                f = lb + (1.0 - lb) * jax.nn.sigmoid(a)
                lf_ref[:, dc:dc + LANES] = jnp.where(valid, jnp.log2(jnp.maximum(f, F_MIN)), 0.0)
                kk_ref[:, dc:dc + LANES] = jnp.where(valid, (1.0 - lb) * jax.nn.sigmoid(-a), 0.0)


def _mixer_a_kernel(sink_ref, q_ref, k_ref, v_ref, g_ref, o_ref):
    two = 2 * BLOCK
    row = lax.broadcasted_iota(jnp.int32, (two, 4 * BLOCK), 0) % BLOCK
    col = lax.broadcasted_iota(jnp.int32, (two, 4 * BLOCK), 1)
    is_meta = (col < BLOCK) & (col >= PAD)
    top = lax.broadcasted_iota(jnp.int32, (two, 1), 0) < BLOCK
    lane_hi = lax.broadcasted_iota(jnp.int32, (BLOCK, LANES), 1) >= HEAD_DIM
    masks = {}

    def band_start(n):
        return min(max((n - 1) * BLOCK, 0), LP - 3 * BLOCK)

    def scores(n):
        start = band_start(n)
        key = (n * BLOCK - start, start < BLOCK)
        if key not in masks:
            kpos = start + col - BLOCK
            band = (col >= BLOCK) & (jnp.abs(n * BLOCK + row - kpos) <= WINDOW)
            if start < BLOCK:
                band = band & (kpos >= BLOCK)
            masks[key] = is_meta | band
        kcat = jnp.concatenate([k_ref[0, 0:BLOCK, :], k_ref[0, start:start + 3 * BLOCK, :]], axis=0)
        qcols = [q_ref[0, n * BLOCK:(n + 1) * BLOCK, cv * LANES:(cv + 1) * LANES] for cv in range(2)]
        out = []
        for hf in range(2):
            sel = lane_hi if hf else jnp.logical_not(lane_hi)
            q2 = jnp.concatenate([jnp.where(sel, qc, jnp.zeros_like(qc)) for qc in qcols], axis=0)
            s = lax.dot_general(q2, kcat, (((1,), (1,)), ((), ())), preferred_element_type=jnp.float32)
            out.append(jnp.where(masks[key], s, NEG))
        return out

    def softmax_pv(n, s_halves):
        start = band_start(n)
        vcat = jnp.concatenate([v_ref[0, 0:BLOCK, :], v_ref[0, start:start + 3 * BLOCK, :]], axis=0)
        rows = slice(n * BLOCK, (n + 1) * BLOCK)
        outs = []
        for hf, s in enumerate(s_halves):
            sink = jnp.where(top, sink_ref[hf], sink_ref[2 + hf]) * LOG2E
            m = jnp.maximum(jnp.max(s, axis=-1, keepdims=True), sink)
            e = jnp.exp2(s - m)
            den = jnp.sum(e, axis=-1, keepdims=True) + jnp.exp2(sink - m)
            p = (e * (1.0 / den)).astype(jnp.bfloat16)
            outs.append(jnp.dot(p, vcat, preferred_element_type=jnp.float32))
        for cv in range(2):
            o = jnp.where(lane_hi, outs[1][cv * BLOCK:(cv + 1) * BLOCK], outs[0][cv * BLOCK:(cv + 1) * BLOCK])
            y = o * g_ref[0, rows, cv * LANES:(cv + 1) * LANES].astype(jnp.float32)
            o_ref[0, rows, cv * LANES:(cv + 1) * LANES] = y.astype(o_ref.dtype)

    ahead = 2
    pending = {}
    for n in range(NBLK + ahead):
        if n < NBLK:
            pending[n] = scores(n)
        if n >= ahead:
            softmax_pv(n - ahead, pending.pop(n - ahead))


def _mixer_a(sink, q3, k3, v3, g3):
    return pl.pallas_call(
        _mixer_a_kernel,
        grid=(BATCH,),
        in_specs=[
            pl.BlockSpec(memory_space=pltpu.SMEM),
            pl.BlockSpec((1, LP, A_WIDTH), lambda b: (b, 0, 0)),
            pl.BlockSpec((1, LP, LANES), lambda b: (b, 0, 0)),
            pl.BlockSpec((1, LP, LANES), lambda b: (b, 0, C_WIDTH // LANES)),
            pl.BlockSpec((1, LP, A_WIDTH), lambda b: (b, 0, 0)),
        ],
        out_specs=pl.BlockSpec((1, LP, A_WIDTH), lambda b: (b, 0, 0)),
        out_shape=jax.ShapeDtypeStruct((BATCH, LP, A_WIDTH), jnp.bfloat16),
        compiler_params=_cparams(("arbitrary",)),
        name="mixer_a_window_gqa",
    )(sink, q3, k3, v3, g3)


def _mixer_b_kernel(lam_init, lam_ref, *refs):
    hps = B_HEADS_PER_STEP
    q_refs, k_refs, v_refs, g_refs = (refs[i * hps:(i + 1) * hps] for i in range(4))
    w_ref, o_ref, s_a, s_b, a_a, a_b = refs[4 * hps:]
    lam = lam_ref[0]
    lane_hi = lax.broadcasted_iota(jnp.int32, (TQ_B, LANES), 1) >= HEAD_DIM
    key_ok = lax.broadcasted_iota(jnp.int32, (TQ_B, BLOCK), 1) >= PAD
    s_bufs, a_bufs = (s_a, s_b), (a_a, a_b)
    nt = (LP - Q0_B) // TQ_B
    nt_dims = (((1,), (1,)), ((), ()))
    tile_rows = lambda t: slice(Q0_B + t * TQ_B, Q0_B + (t + 1) * TQ_B)
    o_ref[0, 0:Q0_B, :] = jnp.zeros((Q0_B, hps * LANES), o_ref.dtype)
    stages = [(hh, t) for hh in range(hps) for t in range(nt)]

    def scores(i):
        hh, t = stages[i]
        q_ref, k_ref = q_refs[hh], k_refs[hh]
        s_buf = s_bufs[i % 2]
        q = q_ref[0, tile_rows(t), :]
        for c in range(2):
            qm = jnp.where(lane_hi if c else jnp.logical_not(lane_hi), q, jnp.zeros_like(q))
            s0 = lax.dot_general(qm, k_ref[0, 0:BLOCK, :], nt_dims, preferred_element_type=jnp.float32)
            s_buf[c, :, 0:BLOCK] = jnp.where(key_ok, s0, NEG)
            s_buf[c, :, BLOCK:LP] = lax.dot_general(qm, k_ref[0, BLOCK:LP, :], nt_dims,
                                                    preferred_element_type=jnp.float32)

    def softmax_pv(i):
        hh, t = stages[i]
        v_ref, g_ref = v_refs[hh], g_refs[hh]
        cols = slice(hh * LANES, (hh + 1) * LANES)
        w = w_ref[:, cols] * (1.0 - lam_init)
        s_buf, a_buf = s_bufs[i % 2], a_bufs[i % 2]
        inv_den0 = []
        for r in range(TQ_B // SM_ROWS):
            rows = slice(r * SM_ROWS, (r + 1) * SM_ROWS)
            es, dens = [], []
            for c in range(2):
                s = s_buf[c, rows, :]
                m = jnp.max(s, axis=-1, keepdims=True)
                e = jnp.exp2(s - m)
                es.append(e)
                dens.append(jnp.sum(e, axis=-1, keepdims=True))
            a_buf[rows, :] = (es[0] - es[1] * (dens[0] * lam / dens[1])).astype(jnp.bfloat16)
            inv_den0.append(1.0 / dens[0])
        o = jnp.dot(a_buf[...], v_ref[0], preferred_element_type=jnp.float32)
        o = o * jnp.concatenate(inv_den0, axis=0)
        ms = jnp.mean(o * o, axis=-1, keepdims=True)
        y = o * lax.rsqrt(ms + RMS_EPS) * w
        y = y * g_ref[0, tile_rows(t), :].astype(jnp.float32)
        o_ref[0, tile_rows(t), cols] = y.astype(o_ref.dtype)

    for i in range(len(stages) + 1):
        if i < len(stages):
            scores(i)
        if i >= 1:
            softmax_pv(i - 1)


def _mixer_b(lam_init, lam, q3, k3, v3, g3, subln_w):
    qoff = A_WIDTH // LANES
    koff = (A_KV_HEADS * HEAD_DIM) // LANES
    voff = (C_WIDTH + A_KV_HEADS * HEAD_DIM) // LANES
    goff = A_WIDTH // LANES
    hps = B_HEADS_PER_STEP
    head_blocks = lambda off: [pl.BlockSpec((1, LP, LANES), lambda b, hp, hh=hh: (b, 0, off + hp * hps + hh))
                               for hh in range(hps)]
    return pl.pallas_call(
        functools.partial(_mixer_b_kernel, lam_init),
        grid=(BATCH, B_HEADS // hps),
        in_specs=([pl.BlockSpec(memory_space=pltpu.SMEM)]
                  + head_blocks(qoff) + head_blocks(koff) + head_blocks(voff) + head_blocks(goff)
                  + [pl.BlockSpec((1, hps * B_V_DIM), lambda b, hp: (0, hp))]),
        out_specs=pl.BlockSpec((1, LP, hps * LANES), lambda b, hp: (b, 0, hp)),
        out_shape=jax.ShapeDtypeStruct((BATCH, LP, B_WIDTH), jnp.bfloat16),
        scratch_shapes=[pltpu.VMEM((2, TQ_B, LP), jnp.float32), pltpu.VMEM((2, TQ_B, LP), jnp.float32),
                        pltpu.VMEM((TQ_B, LP), jnp.bfloat16), pltpu.VMEM((TQ_B, LP), jnp.bfloat16)],
        compiler_params=_cparams(("arbitrary", "arbitrary")),
        name="mixer_b_diff_attn",
    )(lam, *([q3] * hps), *([k3] * hps), *([v3] * hps), *([g3] * hps), subln_w.reshape(1, B_WIDTH))


def _split2(x):
    hi = x.astype(jnp.bfloat16)
    lo = (x - hi.astype(jnp.float32)).astype(jnp.bfloat16)
    return hi, lo


def _hgrn_anchor(b, m, reverse):
    a_off = m if reverse else m - 1
    if 2 * m >= SUBLANES:
        b3 = b.reshape(BLOCK // (2 * m), 2 * m, C_WIDTH)
        return jnp.broadcast_to(b3[:, a_off:a_off + 1, :], b3.shape).reshape(BLOCK, C_WIDTH)
    b3 = b.reshape(BLOCK // SUBLANES, SUBLANES, C_WIDTH)
    sub = lax.broadcasted_iota(jnp.int32, b3.shape, 1)
    anchor = None
    for blk in range(SUBLANES // (2 * m)):
        r = blk * 2 * m + a_off
        cand = jnp.broadcast_to(b3[:, r:r + 1, :], b3.shape)
        anchor = cand if anchor is None else jnp.where(sub >= blk * 2 * m, cand, anchor)
    return anchor.reshape(BLOCK, C_WIDTH)


def _hgrn_chunks(chains, head16, half16, pair_key, same_head):
    nt_dims = (((1,), (1,)), ((), ()))
    f32 = jnp.float32
    for c in chains:
        hi, lo = _split2(c["lf"])
        c["b"] = (jnp.dot(c["tri16"], hi, preferred_element_type=f32)
                  + jnp.dot(c["tri16"], lo, preferred_element_type=f32))
    for c in chains:
        state = c["state"] if c["prev"] is None else chains[c["prev"]]["new_state"]
        edge = 0 if c["reverse"] else BLOCK - 1
        b, b_all = c["b"], c["b"][edge:edge + 1, :]
        q_in = (c["q"] * jnp.exp2(b)).astype(jnp.bfloat16)
        c["o_inter"] = lax.dot_general(q_in, state.astype(jnp.bfloat16), nt_dims, preferred_element_type=f32)
        k_out = (c["kk"] * jnp.exp2(b_all - b)).astype(jnp.bfloat16)
        u_t = lax.dot_general(c["v16"], k_out, (((0,), (0,)), ((), ())), preferred_element_type=f32)
        c["new_state"] = jnp.exp2(b_all) * state + jnp.where(same_head, u_t, 0.0)
        c["p"] = None
    for m in reversed(HGRN_LEVELS):
        for c in chains:
            if "q16" not in c:
                cols = [slice(col * LANES, (col + 1) * LANES) for col in range(C_WIDTH // LANES)]
                c["q16"] = [c["q"][:, cs].astype(jnp.bfloat16) for cs in cols]
                c["k16"] = [[c["kk"][:, cs].astype(jnp.bfloat16) * half16[hf] for hf in range(2)] for cs in cols]
            parts = []
            if m != 0:
                wgt = jnp.exp2(-jnp.abs(c["b"] - _hgrn_anchor(c["b"], m, c["reverse"])))
            for col in range(C_WIDTH // LANES):
                q_col, k_pair = c["q16"][col], c["k16"][col]
                if m != 0:
                    w16 = wgt[:, col * LANES:(col + 1) * LANES].astype(jnp.bfloat16)
                    q_col, k_pair = q_col * w16, [k * w16 for k in k_pair]
                k_two = jnp.concatenate(k_pair, axis=0)
                parts.append(lax.dot_general(q_col, k_two, nt_dims, preferred_element_type=f32))
            pm = jnp.concatenate(parts, axis=1)
            if c["p"] is None:
                c["p"] = pm
            elif m >= SUBLANES:
                shape3 = (BLOCK // (2 * m), 2 * m, C_HEADS * BLOCK)
                p3, pm3, key3 = c["p"].reshape(shape3), pm.reshape(shape3), pair_key.reshape(shape3)
                q_rows = slice(0, m) if c["reverse"] else slice(m, 2 * m)
                new = jnp.where(key3[:, q_rows] < 2 * m, pm3[:, q_rows], p3[:, q_rows])
                halves = [p3[:, 0:m], new] if not c["reverse"] else [new, p3[:, m:2 * m]]
                c["p"] = jnp.concatenate(halves, axis=1).reshape(BLOCK, C_HEADS * BLOCK)
            else:
                c["p"] = jnp.where(pair_key < max(2 * m, 1), pm, c["p"])
    outs = []
    for c in chains:
        p = jnp.where(c["tri_ok"], c["p"], 0.0).astype(jnp.bfloat16)
        v_heads = jnp.concatenate([c["v16"] * head16[h] for h in range(C_HEADS)], axis=0)
        outs.append((jnp.dot(p, v_heads, preferred_element_type=f32) + c["o_inter"], c["new_state"]))
    return outs


def _mixer_c_kernel(q_ref, v_ref, lff_ref, lfb_ref, kkf_ref, kkb_ref, g_ref, w_ref, o_ref, accf_ref, accr_ref):
    lane_head = lax.broadcasted_iota(jnp.int32, (BLOCK, C_WIDTH), 1) // HEAD_DIM
    head16 = [(lane_head == h).astype(jnp.bfloat16) for h in range(C_HEADS)]
    lane_half = lax.broadcasted_iota(jnp.int32, (BLOCK, LANES), 1) // HEAD_DIM
    half16 = [(lane_half == hf).astype(jnp.bfloat16) for hf in range(2)]
    sr = lax.broadcasted_iota(jnp.int32, (C_WIDTH, C_WIDTH), 0) // HEAD_DIM
    sc = lax.broadcasted_iota(jnp.int32, (C_WIDTH, C_WIDTH), 1) // HEAD_DIM
    same_head = sr == sc
    head_ones16 = same_head.astype(jnp.bfloat16)
    t_i = lax.broadcasted_iota(jnp.int32, (BLOCK, C_HEADS * BLOCK), 0)
    s_i = lax.broadcasted_iota(jnp.int32, (BLOCK, C_HEADS * BLOCK), 1) % BLOCK
    pair_key = t_i ^ s_i
    r_i = lax.broadcasted_iota(jnp.int32, (BLOCK, BLOCK), 0)
    c_i = lax.broadcasted_iota(jnp.int32, (BLOCK, BLOCK), 1)
    tri_f, tri_r = (c_i <= r_i).astype(jnp.bfloat16), (c_i >= r_i).astype(jnp.bfloat16)
    ok_f, ok_r = s_i <= t_i, s_i >= t_i

    def run(steps, states):
        chains = []
        for si, step in enumerate(steps):
            for reverse in (False, True):
                n = (NBLK - 1 - step) if reverse else step
                r0 = n * BLOCK if isinstance(n, int) else pl.multiple_of(n * BLOCK, BLOCK)
                rows = pl.ds(r0, BLOCK)
                lf_ref, kk_ref = (lfb_ref, kkb_ref) if reverse else (lff_ref, kkf_ref)
                chains.append(dict(
                    reverse=reverse, rows=rows, q=q_ref[0, rows, :].astype(jnp.float32), v16=v_ref[0, rows, :],
                    kk=kk_ref[0, rows, :], lf=lf_ref[0, rows, :], tri16=tri_r if reverse else tri_f,
                    tri_ok=ok_r if reverse else ok_f, state=states[reverse], prev=None if si == 0 else len(chains) - 2))
        outs = _hgrn_chunks(chains, head16, half16, pair_key, same_head)
        for c, (o, _) in zip(chains, outs):
            (accr_ref if c["reverse"] else accf_ref)[c["rows"], :] = o
        return outs[-2][1], outs[-1][1]

    zero_state = jnp.zeros((C_WIDTH, C_WIDTH), jnp.float32)
    group = HGRN_STEPS_PER_ITER
    states = lax.fori_loop(0, NBLK // group, lambda j, st: run(tuple(group * j + u for u in range(group)), st),
                           (zero_state, zero_state))
    if NBLK % group:
        run(tuple(range(NBLK - NBLK % group, NBLK)), states)

    w = w_ref[...]

    def finish(n, carry):
        r0 = pl.multiple_of(n * BLOCK, BLOCK)
        y = accf_ref[pl.ds(r0, BLOCK), :] + accr_ref[pl.ds(r0, BLOCK), :]
        sq = y * y
        sq_hi = sq.astype(jnp.bfloat16)
        sq_lo = (sq - sq_hi.astype(jnp.float32)).astype(jnp.bfloat16)
        ms = (jnp.dot(sq_hi, head_ones16, preferred_element_type=jnp.float32)
              + jnp.dot(sq_lo, head_ones16, preferred_element_type=jnp.float32)) * (1.0 / HEAD_DIM)
        y = y * lax.rsqrt(ms + RMS_EPS) * w
        y = y * g_ref[0, pl.ds(r0, BLOCK), :].astype(jnp.float32)
        o_ref[0, pl.ds(r0, BLOCK), :] = y.astype(o_ref.dtype)
        return carry

    lax.fori_loop(0, NBLK, finish, 0, unroll=True)


def _mixer_c(q3, v3, lf3, kk3, g3, norm_w):
    blk = lambda j: pl.BlockSpec((1, LP, C_WIDTH), lambda b: (b, 0, j))
    return pl.pallas_call(
        _mixer_c_kernel,
        grid=(BATCH,),
        in_specs=[
            blk(4),
            blk(0),
            blk(0), blk(1),
            blk(0), blk(1),
            blk(3),
            pl.BlockSpec((1, C_WIDTH), lambda b: (0, 0)),
        ],
        out_specs=pl.BlockSpec((1, LP, C_WIDTH), lambda b: (b, 0, 0)),
        out_shape=jax.ShapeDtypeStruct((BATCH, LP, C_WIDTH), jnp.bfloat16),
        scratch_shapes=[pltpu.VMEM((LP, C_WIDTH), jnp.float32), pltpu.VMEM((LP, C_WIDTH), jnp.float32)],
        compiler_params=_cparams(("arbitrary",)),
        name="mixer_c_hgrn2",
    )(q3, v3, lf3, lf3, kk3, kk3, g3, norm_w.reshape(1, C_WIDTH))


def _outproj_last_kernel(ya_ref, yb_ref, yc_ref, wa_ref, wb_ref, wc_ref, h_ref, g_ref, b_ref, o_ref):
    n_rows = o_ref.shape[1]
    halves = (slice(0, n_rows // 2), slice(n_rows // 2, n_rows))
    accs = []
    for rows in halves:
        acc = jnp.dot(ya_ref[rows, :], wa_ref[...], preferred_element_type=jnp.float32)
        acc = acc + jnp.dot(yb_ref[rows, :], wb_ref[...], preferred_element_type=jnp.float32)
        accs.append(acc + jnp.dot(yc_ref[rows, :], wc_ref[...], preferred_element_type=jnp.float32))
    for rows, acc in zip(halves, accs):
        o_ref[0, rows, :] = _layer_norm(ALPHA * h_ref[rows, :] + acc, g_ref[...], b_ref[...])


def _outproj_last(ya, yb, yc, wa, wb, wc, h32, g, b):
    tl = SEQ // 2
    row = lambda n: pl.BlockSpec((pl.Element(tl), pl.Element(n)),
                                 lambda bi, j: (pl.multiple_of(bi * LP + BLOCK + j * tl, BLOCK), 0))
    full = lambda n: pl.BlockSpec((n, D_MODEL), lambda bi, j: (0, 0))
    vec = pl.BlockSpec((1, D_MODEL), lambda bi, j: (0, 0))
    return pl.pallas_call(
        _outproj_last_kernel,
        grid=(BATCH, SEQ // tl),
        in_specs=[row(A_WIDTH), row(B_WIDTH), row(C_WIDTH), full(A_WIDTH), full(B_WIDTH), full(C_WIDTH),
                  row(D_MODEL), vec, vec],
        out_specs=pl.BlockSpec((1, tl, D_MODEL), lambda bi, j: (bi, j, 0)),
        out_shape=jax.ShapeDtypeStruct((BATCH, SEQ, D_MODEL), jnp.float32),
        compiler_params=_cparams(("arbitrary", "arbitrary")),
        name="outproj_last_residual_ln",
    )(ya.reshape(M_ROWS, A_WIDTH), yb.reshape(M_ROWS, B_WIDTH), yc.reshape(M_ROWS, C_WIDTH), wa, wb, wc, h32,
      g.reshape(1, D_MODEL), b.reshape(1, D_MODEL))


def _regroup_w_in(w):
    def heads(lo, order):
        return [w[:, lo + h * HEAD_DIM: lo + (h + 1) * HEAD_DIM] for h in order]

    c = lambda lo, n: w[:, lo:lo + n]
    o_qa, o_ka, o_va, o_ga = 0, 256, 384, 512
    o_qb, o_kb, o_vb, o_gb = 768, 1280, 1792, 2304
    o_qc, o_ic, o_zf, o_zb, o_gc = 2816, 3072, 3328, 3584, 3840
    cat = lambda xs: jnp.concatenate(xs, axis=1).astype(jnp.bfloat16)
    w_q = cat(heads(o_qa, A_HEAD_ORDER) + [c(o_qb, 512)])
    w_k = cat([c(o_ka, 128), c(o_kb, 512)])
    w_p = cat([c(o_ic, 256), c(o_va, 128), c(o_vb, 512)])
    w_s = cat(heads(o_ga, A_HEAD_ORDER) + [c(o_gb, 512), c(o_gc, 256), c(o_qc, 256)])
    w_g = cat([c(o_zf, 256), c(o_zb, 256)])
    return jnp.concatenate([w_q, w_k, w_p, w_s, w_g], axis=1)


def _rope_tables():
    pos = np.arange(LP, dtype=np.float64) - PAD
    inv = ROPE_THETA ** (-np.arange(0, HEAD_DIM, 2, dtype=np.float64) / HEAD_DIM)
    ang = pos[:, None] * inv[None, :]
    cos, sin = np.cos(ang), np.sin(ang)
    cos_t = np.concatenate([cos, cos, cos, cos], axis=1)
    sin_t = np.concatenate([-sin, sin, -sin, sin], axis=1)
    return jnp.asarray(cos_t, jnp.float32), jnp.asarray(sin_t, jnp.float32)


def kernel(x, meta, emb_ln_g, emb_ln_b, w_in, w_out, a_sink, b_lam, b_subln_w, c_lb_logits, c_norm_w, ln_g, ln_b):
    cos_t, sin_t = _rope_tables()
    p_lb = jax.nn.softmax(c_lb_logits.astype(jnp.float32), axis=0)
    lb_all = jnp.cumsum(p_lb, axis=0) - p_lb[0:1]

    front_args = (x, meta, emb_ln_g, emb_ln_b)
    for l in range(DEPTH):
        lam_init = 0.8 - 0.6 * math.exp(-0.3 * l)
        lp32 = b_lam[l].astype(jnp.float32)
        lam = jnp.exp(jnp.sum(lp32[0] * lp32[1])) - jnp.exp(jnp.sum(lp32[2] * lp32[3])) + lam_init
        wo = w_out[l]
        wa = jnp.concatenate([wo[h * HEAD_DIM:(h + 1) * HEAD_DIM] for h in A_HEAD_ORDER], axis=0).astype(jnp.bfloat16)
        wb = wo[A_WIDTH:A_WIDTH + B_WIDTH].astype(jnp.bfloat16)
        wc = wo[A_WIDTH + B_WIDTH:].astype(jnp.bfloat16)
        sink = jnp.stack([a_sink[l, h] for h in A_HEAD_ORDER]).astype(jnp.float32)

        h32, *proj = _front(front_args, _regroup_w_in(w_in[l]), cos_t, sin_t, lb_all[l].reshape(1, C_WIDTH))
        q_all, k_all, p_all, s_all, lf_all, kk_all = [t.reshape(BATCH, LP, t.shape[-1]) for t in proj]

        ya = _mixer_a(sink, q_all, k_all, p_all, s_all)
        yb = _mixer_b(lam_init, lam.reshape(1), q_all, k_all, p_all, s_all, b_subln_w[l])
        yc = _mixer_c(s_all, p_all, lf_all, kk_all, s_all, c_norm_w[l])

        if l == DEPTH - 1:
            return _outproj_last(ya, yb, yc, wa, wb, wc, h32, ln_g[l], ln_b[l])
        front_args = (ya.reshape(M_ROWS, A_WIDTH), yb.reshape(M_ROWS, B_WIDTH), yc.reshape(M_ROWS, C_WIDTH),
                      wa, wb, wc, h32, ln_g[l], ln_b[l])
```

```python
import functools
import math

import jax
import jax.numpy as jnp
import numpy as np
from jax import lax
from jax.experimental import pallas as pl
from jax.experimental.pallas import tpu as pltpu

D_MODEL = 1024
BATCH = 8
SEQ = 2048
DEPTH = 2
N_META = 16
BLOCK = 128
PAD = BLOCK - N_META
WINDOW = 128
ROPE_THETA = 10000.0
HEAD_DIM = 64
A_HEADS = 4
A_KV_HEADS = 2
A_WIDTH = A_HEADS * HEAD_DIM
B_HEADS = 4
B_V_DIM = 2 * HEAD_DIM
B_WIDTH = B_HEADS * B_V_DIM
C_HEADS = 4
C_WIDTH = C_HEADS * HEAD_DIM
D_MIX = A_WIDTH + B_WIDTH + C_WIDTH
ALPHA = (2 * DEPTH) ** 0.25
LN_EPS = 1e-5
RMS_EPS = 1e-6
NEG = -1e30
F_MIN = 1e-30

LP = SEQ + N_META + PAD
NBLK = LP // BLOCK
M_ROWS = BATCH * LP
TM = LP // 4
LANES = 128
SUBLANES = 8
MXU_N = 256
LOG2E = math.log2(math.e)
Q_SCALE = HEAD_DIM ** -0.5 * LOG2E
Q0_B = 96
TQ_B = (LP - Q0_B) // 5
B_HEADS_PER_STEP = 1
SM_ROWS = 32
VMEM_LIMIT = 48 * 1024 * 1024

N_ROPE_Q = A_WIDTH + 2 * B_HEADS * HEAD_DIM
N_ROPE_K = A_KV_HEADS * HEAD_DIM + 2 * B_HEADS * HEAD_DIM
N_PLAIN = C_WIDTH + A_KV_HEADS * HEAD_DIM + B_WIDTH
N_SILU = A_WIDTH + B_WIDTH + C_WIDTH + C_WIDTH
N_GATE = 2 * C_WIDTH
N_IN = N_ROPE_Q + N_ROPE_K + N_PLAIN + N_SILU + N_GATE
A_HEAD_ORDER = (0, 2, 1, 3)

HGRN_LEVELS = (0, 1, 2, 4, 8, 16, 32, 64)
HGRN_STEPS_PER_ITER = 2


def _cparams(sem, flags=None):
    return pltpu.CompilerParams(dimension_semantics=sem, vmem_limit_bytes=VMEM_LIMIT, flags=flags)


def _layer_norm(x, g, b):
    mu = jnp.mean(x, axis=-1, keepdims=True)
    xc = x - mu
    var = jnp.mean(xc * xc, axis=-1, keepdims=True)
    return xc * lax.rsqrt(var + LN_EPS) * g + b


def _front_embed_kernel(x_ref, meta_ref, g_ref, b_ref, w_ref, cos_ref, sin_ref, lb_ref,
                        h32_ref, q_ref, k_ref, p_ref, s_ref, lf_ref, kk_ref):
    first = pl.program_id(0) % (LP // TM) == 0
    g, b = g_ref[...], b_ref[...]
    half = TM // 2
    pad_meta = jnp.concatenate([jnp.zeros((PAD, D_MODEL), jnp.float32), meta_ref[...]], axis=0)
    top = _layer_norm(jnp.where(first, pad_meta, x_ref[0:BLOCK, :]), g, b)
    pad_row = lax.broadcasted_iota(jnp.int32, (BLOCK, 1), 0) < PAD
    top = jnp.where(jnp.logical_and(first, pad_row), 0.0, top)
    off = pl.multiple_of(jnp.where(first, 0, BLOCK), SUBLANES)
    outs = (q_ref, k_ref, p_ref, s_ref, lf_ref, kk_ref)
    h_a = jnp.concatenate([top, _layer_norm(x_ref[pl.ds(off, half - BLOCK), :], g, b)], axis=0)
    h32_ref[0:half, :] = h_a
    _inproj_body(h_a.astype(jnp.bfloat16), slice(0, half), w_ref, cos_ref, sin_ref, lb_ref, outs)
    h_b = _layer_norm(x_ref[pl.ds(off + (half - BLOCK), TM - half), :], g, b)
    h32_ref[half:TM, :] = h_b
    _inproj_body(h_b.astype(jnp.bfloat16), slice(half, TM), w_ref, cos_ref, sin_ref, lb_ref, outs)


def _front_outproj_kernel(ya_ref, yb_ref, yc_ref, wa_ref, wb_ref, wc_ref, h_ref, g_ref, b_ref,
                          w_ref, cos_ref, sin_ref, lb_ref, h32_ref, q_ref, k_ref, p_ref, s_ref, lf_ref, kk_ref):
    halves = (slice(0, TM // 2), slice(TM // 2, TM))
    accs = []
    for rows in halves:
        acc = jnp.dot(ya_ref[rows, :], wa_ref[...], preferred_element_type=jnp.float32)
        acc = acc + jnp.dot(yb_ref[rows, :], wb_ref[...], preferred_element_type=jnp.float32)
        accs.append(acc + jnp.dot(yc_ref[rows, :], wc_ref[...], preferred_element_type=jnp.float32))
    for rows, acc in zip(halves, accs):
        h = _layer_norm(ALPHA * h_ref[rows, :] + acc, g_ref[...], b_ref[...])
        h32_ref[rows, :] = h
        _inproj_body(h.astype(jnp.bfloat16), rows, w_ref, cos_ref, sin_ref, lb_ref,
                     (q_ref, k_ref, p_ref, s_ref, lf_ref, kk_ref))


def _front(prologue_args, w, cos_t, sin_t, lb):
    row = lambda n: pl.BlockSpec((TM, n), lambda i: (i, 0))
    const = lambda r, n: pl.BlockSpec((r, n), lambda i: (0, 0))
    vec = const(1, D_MODEL)
    tab_spec = pl.BlockSpec((TM, LANES), lambda i: (i % (LP // TM), 0))
    tiles_per_b = LP // TM
    if len(prologue_args) == 4:
        x, meta, g, b = prologue_args
        x_spec = pl.BlockSpec(
            (pl.Element(TM), pl.Element(D_MODEL)),
            lambda i: (pl.multiple_of((i // tiles_per_b) * SEQ + jnp.maximum((i % tiles_per_b) * TM - BLOCK, 0),
                                      SUBLANES), 0))
        body, name = _front_embed_kernel, "front_embed_inproj"
        args = (x.reshape(BATCH * SEQ, D_MODEL), meta, g.reshape(1, D_MODEL), b.reshape(1, D_MODEL))
        in_specs = [x_spec, const(N_META, D_MODEL), vec, vec]
        scratch = []
    else:
        ya, yb, yc, wa, wb, wc, h32, g, b = prologue_args
        body, name = _front_outproj_kernel, "front_outproj_inproj"
        args = (ya, yb, yc, wa, wb, wc, h32, g.reshape(1, D_MODEL), b.reshape(1, D_MODEL))
        in_specs = [row(A_WIDTH), row(B_WIDTH), row(C_WIDTH), const(A_WIDTH, D_MODEL), const(B_WIDTH, D_MODEL),
                    const(C_WIDTH, D_MODEL), row(D_MODEL), vec, vec]
        scratch = []
    widths = (D_MODEL, N_ROPE_Q, N_ROPE_K, N_PLAIN, N_SILU, N_GATE, N_GATE)
    dtypes = (jnp.float32,) + (jnp.bfloat16,) * 4 + (jnp.float32,) * 2
    return pl.pallas_call(
        body,
        grid=(M_ROWS // TM,),
        in_specs=in_specs + [const(D_MODEL, N_IN), tab_spec, tab_spec, const(1, C_WIDTH)],
        out_specs=[row(n) for n in widths],
        out_shape=[jax.ShapeDtypeStruct((M_ROWS, n), dt) for n, dt in zip(widths, dtypes)],
        scratch_shapes=scratch,
        compiler_params=_cparams(("arbitrary",)),
        name=name,
    )(*args, w, cos_t, sin_t, lb)


def _col_chunks(n):
    return [(c, min(MXU_N, n - c)) for c in range(0, n, MXU_N)]


def _rope_cols(acc, cos, sin_signed, first_half):
    partner = jnp.where(first_half, pltpu.roll(acc, LANES - HEAD_DIM // 2, 1), pltpu.roll(acc, HEAD_DIM // 2, 1))
    return acc * cos + partner * sin_signed


def _inproj_body(x, rows, w_ref, cos_ref, sin_ref, lb_ref, outs):
    q_ref, k_ref, p_ref, s_ref, lf_ref, kk_ref = [o.at[rows, :] for o in outs]
    n_rows = rows.stop - rows.start
    cos = cos_ref[rows, :]
    sin = sin_ref[rows, :]
    lane = lax.broadcasted_iota(jnp.int32, (n_rows, LANES), 1)
    first_half = (lane % HEAD_DIM) < (HEAD_DIM // 2)
    row = ((pl.program_id(0) % (LP // TM)) * TM + rows.start
           + lax.broadcasted_iota(jnp.int32, (n_rows, LANES), 0))
    valid = row >= PAD
    pieces = ([("rope_q", q_ref, c) for c in range(0, N_ROPE_Q, LANES)]
              + [("rope_k", k_ref, c) for c in range(0, N_ROPE_K, LANES)]
              + [("plain", p_ref, c) for c in range(0, N_PLAIN, LANES)]
              + [("silu", s_ref, c) for c in range(0, N_SILU, LANES)]
              + [("gate", None, c) for c in range(0, N_GATE, LANES)])
    for c0, cw in _col_chunks(N_IN):
        acc = jnp.dot(x, w_ref[:, c0:c0 + cw], preferred_element_type=jnp.float32)
        for j in range(cw // LANES):
            kind, dst, dc = pieces[c0 // LANES + j]
            a = acc[:, j * LANES:(j + 1) * LANES]
            if kind == "rope_q":
                dst[:, dc:dc + LANES] = (_rope_cols(a, cos, sin, first_half) * Q_SCALE).astype(dst.dtype)
            elif kind == "rope_k":
                dst[:, dc:dc + LANES] = _rope_cols(a, cos, sin, first_half).astype(dst.dtype)
            elif kind == "plain":
                dst[:, dc:dc + LANES] = a.astype(dst.dtype)
            elif kind == "silu":
                dst[:, dc:dc + LANES] = (a * jax.nn.sigmoid(a)).astype(dst.dtype)
            else:
                lb = lb_ref[:, dc % C_WIDTH: dc % C_WIDTH + LANES]
                f = lb + (1.0 - lb) * jax.nn.sigmoid(a)
                lf_ref[:, dc:dc + LANES] = jnp.where(valid, jnp.log2(jnp.maximum(f, F_MIN)), 0.0)
                kk_ref[:, dc:dc + LANES] = jnp.where(valid, (1.0 - lb) * jax.nn.sigmoid(-a), 0.0)


def _mixer_a_kernel(sink_ref, q_ref, k_ref, v_ref, g_ref, o_ref):
    two = 2 * BLOCK
    row = lax.broadcasted_iota(jnp.int32, (two, 4 * BLOCK), 0) % BLOCK
    col = lax.broadcasted_iota(jnp.int32, (two, 4 * BLOCK), 1)
    is_meta = (col < BLOCK) & (col >= PAD)
    top = lax.broadcasted_iota(jnp.int32, (two, 1), 0) < BLOCK
    lane_hi = lax.broadcasted_iota(jnp.int32, (BLOCK, LANES), 1) >= HEAD_DIM
    masks = {}

    def band_start(n):
        return min(max((n - 1) * BLOCK, 0), LP - 3 * BLOCK)

    def scores(n):
        start = band_start(n)
        key = (n * BLOCK - start, start < BLOCK)
        if key not in masks:
            kpos = start + col - BLOCK
            band = (col >= BLOCK) & (jnp.abs(n * BLOCK + row - kpos) <= WINDOW)
            if start < BLOCK:
                band = band & (kpos >= BLOCK)
            masks[key] = is_meta | band
        kcat = jnp.concatenate([k_ref[0, 0:BLOCK, :], k_ref[0, start:start + 3 * BLOCK, :]], axis=0)
        qcols = [q_ref[0, n * BLOCK:(n + 1) * BLOCK, cv * LANES:(cv + 1) * LANES] for cv in range(2)]
        out = []
        for hf in range(2):
            sel = lane_hi if hf else jnp.logical_not(lane_hi)
            q2 = jnp.concatenate([jnp.where(sel, qc, jnp.zeros_like(qc)) for qc in qcols], axis=0)
            s = lax.dot_general(q2, kcat, (((1,), (1,)), ((), ())), preferred_element_type=jnp.float32)
            out.append(jnp.where(masks[key], s, NEG))
        return out

    def softmax_pv(n, s_halves):
        start = band_start(n)
        vcat = jnp.concatenate([v_ref[0, 0:BLOCK, :], v_ref[0, start:start + 3 * BLOCK, :]], axis=0)
        rows = slice(n * BLOCK, (n + 1) * BLOCK)
        outs = []
        for hf, s in enumerate(s_halves):
            sink = jnp.where(top, sink_ref[hf], sink_ref[2 + hf]) * LOG2E
            m = jnp.maximum(jnp.max(s, axis=-1, keepdims=True), sink)
            e = jnp.exp2(s - m)
            den = jnp.sum(e, axis=-1, keepdims=True) + jnp.exp2(sink - m)
            p = (e * (1.0 / den)).astype(jnp.bfloat16)
            outs.append(jnp.dot(p, vcat, preferred_element_type=jnp.float32))
        for cv in range(2):
            o = jnp.where(lane_hi, outs[1][cv * BLOCK:(cv + 1) * BLOCK], outs[0][cv * BLOCK:(cv + 1) * BLOCK])
            y = o * g_ref[0, rows, cv * LANES:(cv + 1) * LANES].astype(jnp.float32)
            o_ref[0, rows, cv * LANES:(cv + 1) * LANES] = y.astype(o_ref.dtype)

    ahead = 2
    pending = {}
    for n in range(NBLK + ahead):
        if n < NBLK:
            pending[n] = scores(n)
        if n >= ahead:
            softmax_pv(n - ahead, pending.pop(n - ahead))


def _mixer_a(sink, q3, k3, v3, g3):
    return pl.pallas_call(
        _mixer_a_kernel,
        grid=(BATCH,),
        in_specs=[
            pl.BlockSpec(memory_space=pltpu.SMEM),
            pl.BlockSpec((1, LP, A_WIDTH), lambda b: (b, 0, 0)),
            pl.BlockSpec((1, LP, LANES), lambda b: (b, 0, 0)),
            pl.BlockSpec((1, LP, LANES), lambda b: (b, 0, C_WIDTH // LANES)),
            pl.BlockSpec((1, LP, A_WIDTH), lambda b: (b, 0, 0)),
        ],
        out_specs=pl.BlockSpec((1, LP, A_WIDTH), lambda b: (b, 0, 0)),
        out_shape=jax.ShapeDtypeStruct((BATCH, LP, A_WIDTH), jnp.bfloat16),
        compiler_params=_cparams(("arbitrary",)),
        name="mixer_a_window_gqa",
    )(sink, q3, k3, v3, g3)


def _mixer_b_kernel(lam_init, lam_ref, *refs):
    hps = B_HEADS_PER_STEP
    q_refs, k_refs, v_refs, g_refs = (refs[i * hps:(i + 1) * hps] for i in range(4))
    w_ref, o_ref, s_a, s_b, a_a, a_b = refs[4 * hps:]
    lam = lam_ref[0]
    lane_hi = lax.broadcasted_iota(jnp.int32, (TQ_B, LANES), 1) >= HEAD_DIM
    key_ok = lax.broadcasted_iota(jnp.int32, (TQ_B, BLOCK), 1) >= PAD
    s_bufs, a_bufs = (s_a, s_b), (a_a, a_b)
    nt = (LP - Q0_B) // TQ_B
    nt_dims = (((1,), (1,)), ((), ()))
    tile_rows = lambda t: slice(Q0_B + t * TQ_B, Q0_B + (t + 1) * TQ_B)
    o_ref[0, 0:Q0_B, :] = jnp.zeros((Q0_B, hps * LANES), o_ref.dtype)
    stages = [(hh, t) for hh in range(hps) for t in range(nt)]

    def scores(i):
        hh, t = stages[i]
        q_ref, k_ref = q_refs[hh], k_refs[hh]
        s_buf = s_bufs[i % 2]
        q = q_ref[0, tile_rows(t), :]
        for c in range(2):
            qm = jnp.where(lane_hi if c else jnp.logical_not(lane_hi), q, jnp.zeros_like(q))
            s0 = lax.dot_general(qm, k_ref[0, 0:BLOCK, :], nt_dims, preferred_element_type=jnp.float32)
            s_buf[c, :, 0:BLOCK] = jnp.where(key_ok, s0, NEG)
            s_buf[c, :, BLOCK:LP] = lax.dot_general(qm, k_ref[0, BLOCK:LP, :], nt_dims,
                                                    preferred_element_type=jnp.float32)

    def softmax_pv(i):
        hh, t = stages[i]
        v_ref, g_ref = v_refs[hh], g_refs[hh]
        cols = slice(hh * LANES, (hh + 1) * LANES)
        w = w_ref[:, cols] * (1.0 - lam_init)
        s_buf, a_buf = s_bufs[i % 2], a_bufs[i % 2]
        inv_den0 = []
        for r in range(TQ_B // SM_ROWS):
            rows = slice(r * SM_ROWS, (r + 1) * SM_ROWS)
            es, dens = [], []
            for c in range(2):
                s = s_buf[c, rows, :]
                m = jnp.max(s, axis=-1, keepdims=True)
                e = jnp.exp2(s - m)
                es.append(e)
                dens.append(jnp.sum(e, axis=-1, keepdims=True))
            a_buf[rows, :] = (es[0] - es[1] * (dens[0] * lam / dens[1])).astype(jnp.bfloat16)
            inv_den0.append(1.0 / dens[0])
        o = jnp.dot(a_buf[...], v_ref[0], preferred_element_type=jnp.float32)
        o = o * jnp.concatenate(inv_den0, axis=0)
        ms = jnp.mean(o * o, axis=-1, keepdims=True)
        y = o * lax.rsqrt(ms + RMS_EPS) * w
        y = y * g_ref[0, tile_rows(t), :].astype(jnp.float32)
        o_ref[0, tile_rows(t), cols] = y.astype(o_ref.dtype)

    for i in range(len(stages) + 1):
        if i < len(stages):
            scores(i)
        if i >= 1:
            softmax_pv(i - 1)


def _mixer_b(lam_init, lam, q3, k3, v3, g3, subln_w):
    qoff = A_WIDTH // LANES
    koff = (A_KV_HEADS * HEAD_DIM) // LANES
    voff = (C_WIDTH + A_KV_HEADS * HEAD_DIM) // LANES
    goff = A_WIDTH // LANES
    hps = B_HEADS_PER_STEP
    head_blocks = lambda off: [pl.BlockSpec((1, LP, LANES), lambda b, hp, hh=hh: (b, 0, off + hp * hps + hh))
                               for hh in range(hps)]
    return pl.pallas_call(
        functools.partial(_mixer_b_kernel, lam_init),
        grid=(BATCH, B_HEADS // hps),
        in_specs=([pl.BlockSpec(memory_space=pltpu.SMEM)]
                  + head_blocks(qoff) + head_blocks(koff) + head_blocks(voff) + head_blocks(goff)
                  + [pl.BlockSpec((1, hps * B_V_DIM), lambda b, hp: (0, hp))]),
        out_specs=pl.BlockSpec((1, LP, hps * LANES), lambda b, hp: (b, 0, hp)),
        out_shape=jax.ShapeDtypeStruct((BATCH, LP, B_WIDTH), jnp.bfloat16),
        scratch_shapes=[pltpu.VMEM((2, TQ_B, LP), jnp.float32), pltpu.VMEM((2, TQ_B, LP), jnp.float32),
                        pltpu.VMEM((TQ_B, LP), jnp.bfloat16), pltpu.VMEM((TQ_B, LP), jnp.bfloat16)],
        compiler_params=_cparams(("arbitrary", "arbitrary")),
        name="mixer_b_diff_attn",
    )(lam, *([q3] * hps), *([k3] * hps), *([v3] * hps), *([g3] * hps), subln_w.reshape(1, B_WIDTH))


def _split2(x):
    hi = x.astype(jnp.bfloat16)
    lo = (x - hi.astype(jnp.float32)).astype(jnp.bfloat16)
    return hi, lo


def _hgrn_anchor(b, m, reverse):
    a_off = m if reverse else m - 1
    if 2 * m >= SUBLANES:
        b3 = b.reshape(BLOCK // (2 * m), 2 * m, C_WIDTH)
        return jnp.broadcast_to(b3[:, a_off:a_off + 1, :], b3.shape).reshape(BLOCK, C_WIDTH)
    b3 = b.reshape(BLOCK // SUBLANES, SUBLANES, C_WIDTH)
    sub = lax.broadcasted_iota(jnp.int32, b3.shape, 1)
    anchor = None
    for blk in range(SUBLANES // (2 * m)):
        r = blk * 2 * m + a_off
        cand = jnp.broadcast_to(b3[:, r:r + 1, :], b3.shape)
        anchor = cand if anchor is None else jnp.where(sub >= blk * 2 * m, cand, anchor)
    return anchor.reshape(BLOCK, C_WIDTH)


def _hgrn_level_exponent(b, m, reverse):
    if m < SUBLANES:
        return -jnp.abs(b - _hgrn_anchor(b, m, reverse))
    a_off = m if reverse else m - 1
    b3 = b.reshape(BLOCK // (2 * m), 2 * m, C_WIDTH)
    anchor = b3[:, a_off:a_off + 1, :]
    left, right = b3[:, 0:m], b3[:, m:2 * m]
    halves = [left - anchor, anchor - right] if reverse else [anchor - left, right - anchor]
    return jnp.concatenate(halves, axis=1).reshape(BLOCK, C_WIDTH)


def _hgrn_chunks(chains, head16, half16, pair_key, same_head):
    nt_dims = (((1,), (1,)), ((), ()))
    f32 = jnp.float32
    for c in chains:
        hi, lo = _split2(c["lf"])
        c["b"] = (jnp.dot(c["tri16"], hi, preferred_element_type=f32)
                  + jnp.dot(c["tri16"], lo, preferred_element_type=f32))
    for c in chains:
        state = c["state"] if c["prev"] is None else chains[c["prev"]]["new_state"]
        edge = 0 if c["reverse"] else BLOCK - 1
        b, b_all = c["b"], c["b"][edge:edge + 1, :]
        q_in = (c["q"] * jnp.exp2(b)).astype(jnp.bfloat16)
        c["o_inter"] = lax.dot_general(q_in, state.astype(jnp.bfloat16), nt_dims, preferred_element_type=f32)
        k_out = (c["kk"] * jnp.exp2(b_all - b)).astype(jnp.bfloat16)
        u_t = lax.dot_general(c["v16"], k_out, (((0,), (0,)), ((), ())), preferred_element_type=f32)
        c["new_state"] = jnp.exp2(b_all) * state + jnp.where(same_head, u_t, 0.0)
        c["p"] = None
    for m in reversed(HGRN_LEVELS):
        for c in chains:
            if "q16" not in c:
                cols = [slice(col * LANES, (col + 1) * LANES) for col in range(C_WIDTH // LANES)]
                c["q16"] = [c["q"][:, cs].astype(jnp.bfloat16) for cs in cols]
                c["k16"] = [[c["kk"][:, cs].astype(jnp.bfloat16) * half16[hf] for hf in range(2)] for cs in cols]
            parts = []
            if m != 0:
                wgt = jnp.exp2(_hgrn_level_exponent(c["b"], m, c["reverse"]))
            for col in range(C_WIDTH // LANES):
                q_col, k_pair = c["q16"][col], c["k16"][col]
                if m != 0:
                    w16 = wgt[:, col * LANES:(col + 1) * LANES].astype(jnp.bfloat16)
                    q_col, k_pair = q_col * w16, [k * w16 for k in k_pair]
                k_two = jnp.concatenate(k_pair, axis=0)
                parts.append(lax.dot_general(q_col, k_two, nt_dims, preferred_element_type=f32))
            pm = jnp.concatenate(parts, axis=1)
            if c["p"] is None:
                c["p"] = pm
            elif m >= SUBLANES:
                shape3 = (BLOCK // (2 * m), 2 * m, C_HEADS * BLOCK)
                p3, pm3, key3 = c["p"].reshape(shape3), pm.reshape(shape3), pair_key.reshape(shape3)
                q_rows = slice(0, m) if c["reverse"] else slice(m, 2 * m)
                new = jnp.where(key3[:, q_rows] < 2 * m, pm3[:, q_rows], p3[:, q_rows])
                halves = [p3[:, 0:m], new] if not c["reverse"] else [new, p3[:, m:2 * m]]
                c["p"] = jnp.concatenate(halves, axis=1).reshape(BLOCK, C_HEADS * BLOCK)
            else:
                c["p"] = jnp.where(pair_key < max(2 * m, 1), pm, c["p"])
    outs = []
    for c in chains:
        p = jnp.where(c["tri_ok"], c["p"], 0.0).astype(jnp.bfloat16)
        v_heads = jnp.concatenate([c["v16"] * head16[h] for h in range(C_HEADS)], axis=0)
        outs.append((jnp.dot(p, v_heads, preferred_element_type=f32) + c["o_inter"], c["new_state"]))
    return outs


def _mixer_c_kernel(q_ref, v_ref, lff_ref, lfb_ref, kkf_ref, kkb_ref, g_ref, w_ref, o_ref, accf_ref, accr_ref):
    lane_head = lax.broadcasted_iota(jnp.int32, (BLOCK, C_WIDTH), 1) // HEAD_DIM
    head16 = [(lane_head == h).astype(jnp.bfloat16) for h in range(C_HEADS)]
    lane_half = lax.broadcasted_iota(jnp.int32, (BLOCK, LANES), 1) // HEAD_DIM
    half16 = [(lane_half == hf).astype(jnp.bfloat16) for hf in range(2)]
    sr = lax.broadcasted_iota(jnp.int32, (C_WIDTH, C_WIDTH), 0) // HEAD_DIM
    sc = lax.broadcasted_iota(jnp.int32, (C_WIDTH, C_WIDTH), 1) // HEAD_DIM
    same_head = sr == sc
    head_ones16 = same_head.astype(jnp.bfloat16)
    t_i = lax.broadcasted_iota(jnp.int32, (BLOCK, C_HEADS * BLOCK), 0)
    s_i = lax.broadcasted_iota(jnp.int32, (BLOCK, C_HEADS * BLOCK), 1) % BLOCK
    pair_key = t_i ^ s_i
    r_i = lax.broadcasted_iota(jnp.int32, (BLOCK, BLOCK), 0)
    c_i = lax.broadcasted_iota(jnp.int32, (BLOCK, BLOCK), 1)
    tri_f, tri_r = (c_i <= r_i).astype(jnp.bfloat16), (c_i >= r_i).astype(jnp.bfloat16)
    ok_f, ok_r = s_i <= t_i, s_i >= t_i

    def run(steps, states):
        chains = []
        for si, step in enumerate(steps):
            for reverse in (False, True):
                n = (NBLK - 1 - step) if reverse else step
                r0 = n * BLOCK if isinstance(n, int) else pl.multiple_of(n * BLOCK, BLOCK)
                rows = pl.ds(r0, BLOCK)
                lf_ref, kk_ref = (lfb_ref, kkb_ref) if reverse else (lff_ref, kkf_ref)
                chains.append(dict(
                    reverse=reverse, rows=rows, q=q_ref[0, rows, :].astype(jnp.float32), v16=v_ref[0, rows, :],
                    kk=kk_ref[0, rows, :], lf=lf_ref[0, rows, :], tri16=tri_r if reverse else tri_f,
                    tri_ok=ok_r if reverse else ok_f, state=states[reverse], prev=None if si == 0 else len(chains) - 2))
        outs = _hgrn_chunks(chains, head16, half16, pair_key, same_head)
        for c, (o, _) in zip(chains, outs):
            (accr_ref if c["reverse"] else accf_ref)[c["rows"], :] = o
        return outs[-2][1], outs[-1][1]

    zero_state = jnp.zeros((C_WIDTH, C_WIDTH), jnp.float32)
    group = HGRN_STEPS_PER_ITER
    states = lax.fori_loop(0, NBLK // group, lambda j, st: run(tuple(group * j + u for u in range(group)), st),
                           (zero_state, zero_state))
    if NBLK % group:
        run(tuple(range(NBLK - NBLK % group, NBLK)), states)

    w = w_ref[...]

    def finish(n, carry):
        r0 = pl.multiple_of(n * BLOCK, BLOCK)
        y = accf_ref[pl.ds(r0, BLOCK), :] + accr_ref[pl.ds(r0, BLOCK), :]
        sq = y * y
        sq_hi = sq.astype(jnp.bfloat16)
        sq_lo = (sq - sq_hi.astype(jnp.float32)).astype(jnp.bfloat16)
        ms = (jnp.dot(sq_hi, head_ones16, preferred_element_type=jnp.float32)
              + jnp.dot(sq_lo, head_ones16, preferred_element_type=jnp.float32)) * (1.0 / HEAD_DIM)
        y = y * lax.rsqrt(ms + RMS_EPS) * w
        y = y * g_ref[0, pl.ds(r0, BLOCK), :].astype(jnp.float32)
        o_ref[0, pl.ds(r0, BLOCK), :] = y.astype(o_ref.dtype)
        return carry

    lax.fori_loop(0, NBLK, finish, 0, unroll=True)


def _mixer_c(q3, v3, lf3, kk3, g3, norm_w):
    blk = lambda j: pl.BlockSpec((1, LP, C_WIDTH), lambda b: (b, 0, j))
    return pl.pallas_call(
        _mixer_c_kernel,
        grid=(BATCH,),
        in_specs=[
            blk(4),
            blk(0),
            blk(0), blk(1),
            blk(0), blk(1),
            blk(3),
            pl.BlockSpec((1, C_WIDTH), lambda b: (0, 0)),
        ],
        out_specs=pl.BlockSpec((1, LP, C_WIDTH), lambda b: (b, 0, 0)),
        out_shape=jax.ShapeDtypeStruct((BATCH, LP, C_WIDTH), jnp.bfloat16),
        scratch_shapes=[pltpu.VMEM((LP, C_WIDTH), jnp.float32), pltpu.VMEM((LP, C_WIDTH), jnp.float32)],
        compiler_params=_cparams(("arbitrary",)),
        name="mixer_c_hgrn2",
    )(q3, v3, lf3, lf3, kk3, kk3, g3, norm_w.reshape(1, C_WIDTH))


def _outproj_last_kernel(ya_ref, yb_ref, yc_ref, wa_ref, wb_ref, wc_ref, h_ref, g_ref, b_ref, o_ref):
    n_rows = o_ref.shape[1]
    halves = (slice(0, n_rows // 2), slice(n_rows // 2, n_rows))
    accs = []
    for rows in halves:
        acc = jnp.dot(ya_ref[rows, :], wa_ref[...], preferred_element_type=jnp.float32)
        acc = acc + jnp.dot(yb_ref[rows, :], wb_ref[...], preferred_element_type=jnp.float32)
        accs.append(acc + jnp.dot(yc_ref[rows, :], wc_ref[...], preferred_element_type=jnp.float32))
    for rows, acc in zip(halves, accs):
        o_ref[0, rows, :] = _layer_norm(ALPHA * h_ref[rows, :] + acc, g_ref[...], b_ref[...])


def _outproj_last(ya, yb, yc, wa, wb, wc, h32, g, b):
    tl = SEQ // 2
    row = lambda n: pl.BlockSpec((pl.Element(tl), pl.Element(n)),
                                 lambda bi, j: (pl.multiple_of(bi * LP + BLOCK + j * tl, BLOCK), 0))
    full = lambda n: pl.BlockSpec((n, D_MODEL), lambda bi, j: (0, 0))
    vec = pl.BlockSpec((1, D_MODEL), lambda bi, j: (0, 0))
    return pl.pallas_call(
        _outproj_last_kernel,
        grid=(BATCH, SEQ // tl),
        in_specs=[row(A_WIDTH), row(B_WIDTH), row(C_WIDTH), full(A_WIDTH), full(B_WIDTH), full(C_WIDTH),
                  row(D_MODEL), vec, vec],
        out_specs=pl.BlockSpec((1, tl, D_MODEL), lambda bi, j: (bi, j, 0)),
        out_shape=jax.ShapeDtypeStruct((BATCH, SEQ, D_MODEL), jnp.float32),
        compiler_params=_cparams(("arbitrary", "arbitrary")),
        name="outproj_last_residual_ln",
    )(ya.reshape(M_ROWS, A_WIDTH), yb.reshape(M_ROWS, B_WIDTH), yc.reshape(M_ROWS, C_WIDTH), wa, wb, wc, h32,
      g.reshape(1, D_MODEL), b.reshape(1, D_MODEL))


def _regroup_w_in(w):
    def heads(lo, order):
        return [w[:, lo + h * HEAD_DIM: lo + (h + 1) * HEAD_DIM] for h in order]

    c = lambda lo, n: w[:, lo:lo + n]
    o_qa, o_ka, o_va, o_ga = 0, 256, 384, 512
    o_qb, o_kb, o_vb, o_gb = 768, 1280, 1792, 2304
    o_qc, o_ic, o_zf, o_zb, o_gc = 2816, 3072, 3328, 3584, 3840
    cat = lambda xs: jnp.concatenate(xs, axis=1).astype(jnp.bfloat16)
    w_q = cat(heads(o_qa, A_HEAD_ORDER) + [c(o_qb, 512)])
    w_k = cat([c(o_ka, 128), c(o_kb, 512)])
    w_p = cat([c(o_ic, 256), c(o_va, 128), c(o_vb, 512)])
    w_s = cat(heads(o_ga, A_HEAD_ORDER) + [c(o_gb, 512), c(o_gc, 256), c(o_qc, 256)])
    w_g = cat([c(o_zf, 256), c(o_zb, 256)])
    return jnp.concatenate([w_q, w_k, w_p, w_s, w_g], axis=1)


def _rope_tables():
    pos = np.arange(LP, dtype=np.float64) - PAD
    inv = ROPE_THETA ** (-np.arange(0, HEAD_DIM, 2, dtype=np.float64) / HEAD_DIM)
    ang = pos[:, None] * inv[None, :]
    cos, sin = np.cos(ang), np.sin(ang)
    cos_t = np.concatenate([cos, cos, cos, cos], axis=1)
    sin_t = np.concatenate([-sin, sin, -sin, sin], axis=1)
    return jnp.asarray(cos_t, jnp.float32), jnp.asarray(sin_t, jnp.float32)


def kernel(x, meta, emb_ln_g, emb_ln_b, w_in, w_out, a_sink, b_lam, b_subln_w, c_lb_logits, c_norm_w, ln_g, ln_b):
    cos_t, sin_t = _rope_tables()
    p_lb = jax.nn.softmax(c_lb_logits.astype(jnp.float32), axis=0)
    lb_all = jnp.cumsum(p_lb, axis=0) - p_lb[0:1]

    front_args = (x, meta, emb_ln_g, emb_ln_b)
    for l in range(DEPTH):
        lam_init = 0.8 - 0.6 * math.exp(-0.3 * l)
        lp32 = b_lam[l].astype(jnp.float32)
        lam = jnp.exp(jnp.sum(lp32[0] * lp32[1])) - jnp.exp(jnp.sum(lp32[2] * lp32[3])) + lam_init
        wo = w_out[l]
        wa = jnp.concatenate([wo[h * HEAD_DIM:(h + 1) * HEAD_DIM] for h in A_HEAD_ORDER], axis=0).astype(jnp.bfloat16)
        wb = wo[A_WIDTH:A_WIDTH + B_WIDTH].astype(jnp.bfloat16)
        wc = wo[A_WIDTH + B_WIDTH:].astype(jnp.bfloat16)
        sink = jnp.stack([a_sink[l, h] for h in A_HEAD_ORDER]).astype(jnp.float32)

        h32, *proj = _front(front_args, _regroup_w_in(w_in[l]), cos_t, sin_t, lb_all[l].reshape(1, C_WIDTH))
        q_all, k_all, p_all, s_all, lf_all, kk_all = [t.reshape(BATCH, LP, t.shape[-1]) for t in proj]

        ya = _mixer_a(sink, q_all, k_all, p_all, s_all)
        yb = _mixer_b(lam_init, lam.reshape(1), q_all, k_all, p_all, s_all, b_subln_w[l])
        yc = _mixer_c(s_all, p_all, lf_all, kk_all, s_all, c_norm_w[l])

        if l == DEPTH - 1:
            return _outproj_last(ya, yb, yc, wa, wb, wc, h32, ln_g[l], ln_b[l])
        front_args = (ya.reshape(M_ROWS, A_WIDTH), yb.reshape(M_ROWS, B_WIDTH), yc.reshape(M_ROWS, C_WIDTH),
                      wa, wb, wc, h32, ln_g[l], ln_b[l])
```

```python
import functools
import math

import jax
import jax.numpy as jnp
import numpy as np
from jax import lax
from jax.experimental import pallas as pl
from jax.experimental.pallas import tpu as pltpu

D_MODEL = 1024
BATCH = 8
SEQ = 2048
DEPTH = 2
N_META = 16
BLOCK = 128
PAD = BLOCK - N_META
WINDOW = 128
ROPE_THETA = 10000.0
HEAD_DIM = 64
A_HEADS = 4
A_KV_HEADS = 2
A_WIDTH = A_HEADS * HEAD_DIM
B_HEADS = 4
B_V_DIM = 2 * HEAD_DIM
B_WIDTH = B_HEADS * B_V_DIM
C_HEADS = 4
C_WIDTH = C_HEADS * HEAD_DIM
D_MIX = A_WIDTH + B_WIDTH + C_WIDTH
ALPHA = (2 * DEPTH) ** 0.25
LN_EPS = 1e-5
RMS_EPS = 1e-6
NEG = -1e30
F_MIN = 1e-30

LP = SEQ + N_META + PAD
NBLK = LP // BLOCK
M_ROWS = BATCH * LP
TM = LP // 4
LANES = 128
SUBLANES = 8
MXU_N = 256
LOG2E = math.log2(math.e)
Q_SCALE = HEAD_DIM ** -0.5 * LOG2E
Q0_B = 96
TQ_B = (LP - Q0_B) // 5
B_HEADS_PER_STEP = 1
SM_ROWS = 32
VMEM_LIMIT = 48 * 1024 * 1024

N_ROPE_Q = A_WIDTH + 2 * B_HEADS * HEAD_DIM
N_ROPE_K = A_KV_HEADS * HEAD_DIM + 2 * B_HEADS * HEAD_DIM
N_PLAIN = C_WIDTH + A_KV_HEADS * HEAD_DIM + B_WIDTH
N_SILU = A_WIDTH + B_WIDTH + C_WIDTH + C_WIDTH
N_GATE = 2 * C_WIDTH
N_IN = N_ROPE_Q + N_ROPE_K + N_PLAIN + N_SILU + N_GATE
A_HEAD_ORDER = (0, 2, 1, 3)

HGRN_LEVELS = (0, 1, 2, 4, 8, 16, 32, 64)
HGRN_STEPS_PER_ITER = 4


def _cparams(sem):
    return pltpu.CompilerParams(dimension_semantics=sem, vmem_limit_bytes=VMEM_LIMIT)


def _layer_norm(x, g, b):
    mu = jnp.mean(x, axis=-1, keepdims=True)
    xc = x - mu
    var = jnp.mean(xc * xc, axis=-1, keepdims=True)
    return xc * lax.rsqrt(var + LN_EPS) * g + b


def _front_embed_kernel(x_ref, meta_ref, g_ref, b_ref, w_ref, cos_ref, sin_ref, lb_ref,
                        h32_ref, q_ref, k_ref, p_ref, s_ref, lf_ref, kk_ref):
    first = pl.program_id(0) % (LP // TM) == 0
    g, b = g_ref[...], b_ref[...]
    half = TM // 2
    pad_meta = jnp.concatenate([jnp.zeros((PAD, D_MODEL), jnp.float32), meta_ref[...]], axis=0)
    top = _layer_norm(jnp.where(first, pad_meta, x_ref[0:BLOCK, :]), g, b)
    pad_row = lax.broadcasted_iota(jnp.int32, (BLOCK, 1), 0) < PAD
    top = jnp.where(jnp.logical_and(first, pad_row), 0.0, top)
    off = pl.multiple_of(jnp.where(first, 0, BLOCK), SUBLANES)
    outs = (q_ref, k_ref, p_ref, s_ref, lf_ref, kk_ref)
    h_a = jnp.concatenate([top, _layer_norm(x_ref[pl.ds(off, half - BLOCK), :], g, b)], axis=0)
    h32_ref[0:half, :] = h_a
    _inproj_body(h_a.astype(jnp.bfloat16), slice(0, half), w_ref, cos_ref, sin_ref, lb_ref, outs)
    h_b = _layer_norm(x_ref[pl.ds(off + (half - BLOCK), TM - half), :], g, b)
    h32_ref[half:TM, :] = h_b
    _inproj_body(h_b.astype(jnp.bfloat16), slice(half, TM), w_ref, cos_ref, sin_ref, lb_ref, outs)


def _front_outproj_kernel(ya_ref, yb_ref, yc_ref, wa_ref, wb_ref, wc_ref, h_ref, g_ref, b_ref,
                          w_ref, cos_ref, sin_ref, lb_ref, h32_ref, q_ref, k_ref, p_ref, s_ref, lf_ref, kk_ref):
    halves = (slice(0, TM // 2), slice(TM // 2, TM))
    accs = []
    for rows in halves:
        acc = jnp.dot(ya_ref[rows, :], wa_ref[...], preferred_element_type=jnp.float32)
        acc = acc + jnp.dot(yb_ref[rows, :], wb_ref[...], preferred_element_type=jnp.float32)
        accs.append(acc + jnp.dot(yc_ref[rows, :], wc_ref[...], preferred_element_type=jnp.float32))
    for rows, acc in zip(halves, accs):
        h = _layer_norm(ALPHA * h_ref[rows, :] + acc, g_ref[...], b_ref[...])
        h32_ref[rows, :] = h
        _inproj_body(h.astype(jnp.bfloat16), rows, w_ref, cos_ref, sin_ref, lb_ref,
                     (q_ref, k_ref, p_ref, s_ref, lf_ref, kk_ref))


def _front(prologue_args, w, cos_t, sin_t, lb):
    row = lambda n: pl.BlockSpec((TM, n), lambda i: (i, 0))
    const = lambda r, n: pl.BlockSpec((r, n), lambda i: (0, 0))
    vec = const(1, D_MODEL)
    tab_spec = pl.BlockSpec((TM, LANES), lambda i: (i % (LP // TM), 0))
    tiles_per_b = LP // TM
    if len(prologue_args) == 4:
        x, meta, g, b = prologue_args
        x_spec = pl.BlockSpec(
            (pl.Element(TM), pl.Element(D_MODEL)),
            lambda i: (pl.multiple_of((i // tiles_per_b) * SEQ + jnp.maximum((i % tiles_per_b) * TM - BLOCK, 0),
                                      SUBLANES), 0))
        body, name = _front_embed_kernel, "front_embed_inproj"
        args = (x.reshape(BATCH * SEQ, D_MODEL), meta, g.reshape(1, D_MODEL), b.reshape(1, D_MODEL))
        in_specs = [x_spec, const(N_META, D_MODEL), vec, vec]
    else:
        ya, yb, yc, wa, wb, wc, h32, g, b = prologue_args
        body, name = _front_outproj_kernel, "front_outproj_inproj"
        args = (ya, yb, yc, wa, wb, wc, h32, g.reshape(1, D_MODEL), b.reshape(1, D_MODEL))
        in_specs = [row(A_WIDTH), row(B_WIDTH), row(C_WIDTH), const(A_WIDTH, D_MODEL), const(B_WIDTH, D_MODEL),
                    const(C_WIDTH, D_MODEL), row(D_MODEL), vec, vec]
    widths = (D_MODEL, N_ROPE_Q, N_ROPE_K, N_PLAIN, N_SILU, N_GATE, N_GATE)
    dtypes = (jnp.float32,) + (jnp.bfloat16,) * 4 + (jnp.float32,) * 2
    return pl.pallas_call(
        body,
        grid=(M_ROWS // TM,),
        in_specs=in_specs + [const(D_MODEL, N_IN), tab_spec, tab_spec, const(1, C_WIDTH)],
        out_specs=[row(n) for n in widths],
        out_shape=[jax.ShapeDtypeStruct((M_ROWS, n), dt) for n, dt in zip(widths, dtypes)],
        compiler_params=_cparams(("arbitrary",)),
        name=name,
    )(*args, w, cos_t, sin_t, lb)


def _col_chunks(n):
    return [(c, min(MXU_N, n - c)) for c in range(0, n, MXU_N)]


def _rope_cols(acc, cos, sin_signed, first_half):
    partner = jnp.where(first_half, pltpu.roll(acc, LANES - HEAD_DIM // 2, 1), pltpu.roll(acc, HEAD_DIM // 2, 1))
    return acc * cos + partner * sin_signed


def _inproj_body(x, rows, w_ref, cos_ref, sin_ref, lb_ref, outs):
    q_ref, k_ref, p_ref, s_ref, lf_ref, kk_ref = [o.at[rows, :] for o in outs]
    n_rows = rows.stop - rows.start
    cos = cos_ref[rows, :]
    sin = sin_ref[rows, :]
    lane = lax.broadcasted_iota(jnp.int32, (n_rows, LANES), 1)
    first_half = (lane % HEAD_DIM) < (HEAD_DIM // 2)
    row = ((pl.program_id(0) % (LP // TM)) * TM + rows.start
           + lax.broadcasted_iota(jnp.int32, (n_rows, LANES), 0))
    valid = row >= PAD
    pieces = ([("rope_q", q_ref, c) for c in range(0, N_ROPE_Q, LANES)]
              + [("rope_k", k_ref, c) for c in range(0, N_ROPE_K, LANES)]
              + [("plain", p_ref, c) for c in range(0, N_PLAIN, LANES)]
              + [("silu", s_ref, c) for c in range(0, N_SILU, LANES)]
              + [("gate", None, c) for c in range(0, N_GATE, LANES)])
    for c0, cw in _col_chunks(N_IN):
        acc = jnp.dot(x, w_ref[:, c0:c0 + cw], preferred_element_type=jnp.float32)
        for j in range(cw // LANES):
            kind, dst, dc = pieces[c0 // LANES + j]
            a = acc[:, j * LANES:(j + 1) * LANES]
            if kind == "rope_q":
                dst[:, dc:dc + LANES] = (_rope_cols(a, cos, sin, first_half) * Q_SCALE).astype(dst.dtype)
            elif kind == "rope_k":
                dst[:, dc:dc + LANES] = _rope_cols(a, cos, sin, first_half).astype(dst.dtype)
            elif kind == "plain":
                dst[:, dc:dc + LANES] = a.astype(dst.dtype)
            elif kind == "silu":
                dst[:, dc:dc + LANES] = (a * jax.nn.sigmoid(a)).astype(dst.dtype)
            else:
                lb = lb_ref[:, dc % C_WIDTH: dc % C_WIDTH + LANES]
                f = lb + (1.0 - lb) * jax.nn.sigmoid(a)
                lf_ref[:, dc:dc + LANES] = jnp.where(valid, jnp.log2(jnp.maximum(f, F_MIN)), 0.0)
                kk_ref[:, dc:dc + LANES] = jnp.where(valid, (1.0 - lb) * jax.nn.sigmoid(-a), 0.0)


def _mixer_a_kernel(sink_ref, q_ref, k_ref, v_ref, g_ref, o_ref):
    two = 2 * BLOCK
    row = lax.broadcasted_iota(jnp.int32, (two, 4 * BLOCK), 0) % BLOCK
    col = lax.broadcasted_iota(jnp.int32, (two, 4 * BLOCK), 1)
    is_meta = (col < BLOCK) & (col >= PAD)
    top = lax.broadcasted_iota(jnp.int32, (two, 1), 0) < BLOCK
    lane_hi = lax.broadcasted_iota(jnp.int32, (BLOCK, LANES), 1) >= HEAD_DIM
    masks = {}

    def band_start(n):
        return min(max((n - 1) * BLOCK, 0), LP - 3 * BLOCK)

    def scores(n):
        start = band_start(n)
        key = (n * BLOCK - start, start < BLOCK)
        if key not in masks:
            kpos = start + col - BLOCK
            band = (col >= BLOCK) & (jnp.abs(n * BLOCK + row - kpos) <= WINDOW)
            if start < BLOCK:
                band = band & (kpos >= BLOCK)
            masks[key] = is_meta | band
        kcat = jnp.concatenate([k_ref[0, 0:BLOCK, :], k_ref[0, start:start + 3 * BLOCK, :]], axis=0)
        qcols = [q_ref[0, n * BLOCK:(n + 1) * BLOCK, cv * LANES:(cv + 1) * LANES] for cv in range(2)]
        out = []
        for hf in range(2):
            sel = lane_hi if hf else jnp.logical_not(lane_hi)
            q2 = jnp.concatenate([jnp.where(sel, qc, jnp.zeros_like(qc)) for qc in qcols], axis=0)
            s = lax.dot_general(q2, kcat, (((1,), (1,)), ((), ())), preferred_element_type=jnp.float32)
            out.append(jnp.where(masks[key], s, NEG))
        return out

    def softmax_pv(n, s_halves):
        start = band_start(n)
        vcat = jnp.concatenate([v_ref[0, 0:BLOCK, :], v_ref[0, start:start + 3 * BLOCK, :]], axis=0)
        rows = slice(n * BLOCK, (n + 1) * BLOCK)
        outs = []
        for hf, s in enumerate(s_halves):
            sink = jnp.where(top, sink_ref[hf], sink_ref[2 + hf]) * LOG2E
            m = jnp.maximum(jnp.max(s, axis=-1, keepdims=True), sink)
            e = jnp.exp2(s - m)
            den = jnp.sum(e, axis=-1, keepdims=True) + jnp.exp2(sink - m)
            p = (e * (1.0 / den)).astype(jnp.bfloat16)
            outs.append(jnp.dot(p, vcat, preferred_element_type=jnp.float32))
        for cv in range(2):
            o = jnp.where(lane_hi, outs[1][cv * BLOCK:(cv + 1) * BLOCK], outs[0][cv * BLOCK:(cv + 1) * BLOCK])
            y = o * g_ref[0, rows, cv * LANES:(cv + 1) * LANES].astype(jnp.float32)
            o_ref[0, rows, cv * LANES:(cv + 1) * LANES] = y.astype(o_ref.dtype)

    ahead = 2
    pending = {}
    for n in range(NBLK + ahead):
        if n < NBLK:
            pending[n] = scores(n)
        if n >= ahead:
            softmax_pv(n - ahead, pending.pop(n - ahead))


def _mixer_a(sink, q3, k3, v3, g3):
    return pl.pallas_call(
        _mixer_a_kernel,
        grid=(BATCH,),
        in_specs=[
            pl.BlockSpec(memory_space=pltpu.SMEM),
            pl.BlockSpec((1, LP, A_WIDTH), lambda b: (b, 0, 0)),
            pl.BlockSpec((1, LP, LANES), lambda b: (b, 0, 0)),
            pl.BlockSpec((1, LP, LANES), lambda b: (b, 0, C_WIDTH // LANES)),
            pl.BlockSpec((1, LP, A_WIDTH), lambda b: (b, 0, 0)),
        ],
        out_specs=pl.BlockSpec((1, LP, A_WIDTH), lambda b: (b, 0, 0)),
        out_shape=jax.ShapeDtypeStruct((BATCH, LP, A_WIDTH), jnp.bfloat16),
        compiler_params=_cparams(("arbitrary",)),
        name="mixer_a_window_gqa",
    )(sink, q3, k3, v3, g3)


def _mixer_b_kernel(lam_init, lam_ref, *refs):
    hps = B_HEADS_PER_STEP
    q_refs, k_refs, v_refs, g_refs = (refs[i * hps:(i + 1) * hps] for i in range(4))
    w_ref, o_ref, s_a, s_b, a_a, a_b = refs[4 * hps:]
    lam = lam_ref[0]
    lane_hi = lax.broadcasted_iota(jnp.int32, (TQ_B, LANES), 1) >= HEAD_DIM
    key_ok = lax.broadcasted_iota(jnp.int32, (TQ_B, BLOCK), 1) >= PAD
    s_bufs, a_bufs = (s_a, s_b), (a_a, a_b)
    nt = (LP - Q0_B) // TQ_B
    nt_dims = (((1,), (1,)), ((), ()))
    tile_rows = lambda t: slice(Q0_B + t * TQ_B, Q0_B + (t + 1) * TQ_B)
    o_ref[0, 0:Q0_B, :] = jnp.zeros((Q0_B, hps * LANES), o_ref.dtype)
    stages = [(hh, t) for hh in range(hps) for t in range(nt)]

    def scores(i):
        hh, t = stages[i]
        q_ref, k_ref = q_refs[hh], k_refs[hh]
        s_buf = s_bufs[i % 2]
        q = q_ref[0, tile_rows(t), :]
        for c in range(2):
            qm = jnp.where(lane_hi if c else jnp.logical_not(lane_hi), q, jnp.zeros_like(q))
            s0 = lax.dot_general(qm, k_ref[0, 0:BLOCK, :], nt_dims, preferred_element_type=jnp.float32)
            s_buf[c, :, 0:BLOCK] = jnp.where(key_ok, s0, NEG)
            s_buf[c, :, BLOCK:LP] = lax.dot_general(qm, k_ref[0, BLOCK:LP, :], nt_dims,
                                                    preferred_element_type=jnp.float32)

    def softmax_pv(i):
        hh, t = stages[i]
        v_ref, g_ref = v_refs[hh], g_refs[hh]
        cols = slice(hh * LANES, (hh + 1) * LANES)
        w = w_ref[:, cols] * (1.0 - lam_init)
        s_buf, a_buf = s_bufs[i % 2], a_bufs[i % 2]
        inv_den0 = []
        for r in range(TQ_B // SM_ROWS):
            rows = slice(r * SM_ROWS, (r + 1) * SM_ROWS)
            es, dens = [], []
            for c in range(2):
                s = s_buf[c, rows, :]
                m = jnp.max(s, axis=-1, keepdims=True)
                e = jnp.exp2(s - m)
                es.append(e)
                dens.append(jnp.sum(e, axis=-1, keepdims=True))
            a_buf[rows, :] = (es[0] - es[1] * (dens[0] * lam / dens[1])).astype(jnp.bfloat16)
            inv_den0.append(1.0 / dens[0])
        o = jnp.dot(a_buf[...], v_ref[0], preferred_element_type=jnp.float32)
        o = o * jnp.concatenate(inv_den0, axis=0)
        ms = jnp.mean(o * o, axis=-1, keepdims=True)
        y = o * lax.rsqrt(ms + RMS_EPS) * w
        y = y * g_ref[0, tile_rows(t), :].astype(jnp.float32)
        o_ref[0, tile_rows(t), cols] = y.astype(o_ref.dtype)

    for i in range(len(stages) + 1):
        if i < len(stages):
            scores(i)
        if i >= 1:
            softmax_pv(i - 1)


def _mixer_b(lam_init, lam, q3, k3, v3, g3, subln_w):
    qoff = A_WIDTH // LANES
    koff = (A_KV_HEADS * HEAD_DIM) // LANES
    voff = (C_WIDTH + A_KV_HEADS * HEAD_DIM) // LANES
    goff = A_WIDTH // LANES
    hps = B_HEADS_PER_STEP
    head_blocks = lambda off: [pl.BlockSpec((1, LP, LANES), lambda b, hp, hh=hh: (b, 0, off + hp * hps + hh))
                               for hh in range(hps)]
    return pl.pallas_call(
        functools.partial(_mixer_b_kernel, lam_init),
        grid=(BATCH, B_HEADS // hps),
        in_specs=([pl.BlockSpec(memory_space=pltpu.SMEM)]
                  + head_blocks(qoff) + head_blocks(koff) + head_blocks(voff) + head_blocks(goff)
                  + [pl.BlockSpec((1, hps * B_V_DIM), lambda b, hp: (0, hp))]),
        out_specs=pl.BlockSpec((1, LP, hps * LANES), lambda b, hp: (b, 0, hp)),
        out_shape=jax.ShapeDtypeStruct((BATCH, LP, B_WIDTH), jnp.bfloat16),
        scratch_shapes=[pltpu.VMEM((2, TQ_B, LP), jnp.float32), pltpu.VMEM((2, TQ_B, LP), jnp.float32),
                        pltpu.VMEM((TQ_B, LP), jnp.bfloat16), pltpu.VMEM((TQ_B, LP), jnp.bfloat16)],
        compiler_params=_cparams(("arbitrary", "arbitrary")),
        name="mixer_b_diff_attn",
    )(lam, *([q3] * hps), *([k3] * hps), *([v3] * hps), *([g3] * hps), subln_w.reshape(1, B_WIDTH))


def _split2(x):
    hi = x.astype(jnp.bfloat16)
    lo = (x - hi.astype(jnp.float32)).astype(jnp.bfloat16)
    return hi, lo


def _hgrn_anchor(b, m, reverse):
    a_off = m if reverse else m - 1
    if 2 * m >= SUBLANES:
        b3 = b.reshape(BLOCK // (2 * m), 2 * m, C_WIDTH)
        return jnp.broadcast_to(b3[:, a_off:a_off + 1, :], b3.shape).reshape(BLOCK, C_WIDTH)
    b3 = b.reshape(BLOCK // SUBLANES, SUBLANES, C_WIDTH)
    sub = lax.broadcasted_iota(jnp.int32, b3.shape, 1)
    anchor = None
    for blk in range(SUBLANES // (2 * m)):
        r = blk * 2 * m + a_off
        cand = jnp.broadcast_to(b3[:, r:r + 1, :], b3.shape)
        anchor = cand if anchor is None else jnp.where(sub >= blk * 2 * m, cand, anchor)
    return anchor.reshape(BLOCK, C_WIDTH)


def _hgrn_level_exponent(b, m, reverse):
    if m < SUBLANES:
        return -jnp.abs(b - _hgrn_anchor(b, m, reverse))
    a_off = m if reverse else m - 1
    b3 = b.reshape(BLOCK // (2 * m), 2 * m, C_WIDTH)
    anchor = b3[:, a_off:a_off + 1, :]
    left, right = b3[:, 0:m], b3[:, m:2 * m]
    halves = [left - anchor, anchor - right] if reverse else [anchor - left, right - anchor]
    return jnp.concatenate(halves, axis=1).reshape(BLOCK, C_WIDTH)


def _hgrn_chunks(chains, head16, half16, pair_key, same_head):
    nt_dims = (((1,), (1,)), ((), ()))
    f32 = jnp.float32
    for c in chains:
        hi, lo = _split2(c["lf"])
        c["b"] = (jnp.dot(c["tri16"], hi, preferred_element_type=f32)
                  + jnp.dot(c["tri16"], lo, preferred_element_type=f32))
    for c in chains:
        state = c["state"] if c["prev"] is None else chains[c["prev"]]["new_state"]
        edge = 0 if c["reverse"] else BLOCK - 1
        b, b_all = c["b"], c["b"][edge:edge + 1, :]
        q_in = (c["q"] * jnp.exp2(b)).astype(jnp.bfloat16)
        c["o_inter"] = lax.dot_general(q_in, state.astype(jnp.bfloat16), nt_dims, preferred_element_type=f32)
        k_out = (c["kk"] * jnp.exp2(b_all - b)).astype(jnp.bfloat16)
        u_t = lax.dot_general(c["v16"], k_out, (((0,), (0,)), ((), ())), preferred_element_type=f32)
        c["new_state"] = jnp.exp2(b_all) * state + jnp.where(same_head, u_t, 0.0)
        c["p"] = None
    for m in reversed(HGRN_LEVELS):
        for c in chains:
            if "q16" not in c:
                cols = [slice(col * LANES, (col + 1) * LANES) for col in range(C_WIDTH // LANES)]
                c["q16"] = [c["q"][:, cs].astype(jnp.bfloat16) for cs in cols]
                c["k16"] = [[c["kk"][:, cs].astype(jnp.bfloat16) * half16[hf] for hf in range(2)] for cs in cols]
            parts = []
            if m != 0:
                wgt = jnp.exp2(_hgrn_level_exponent(c["b"], m, c["reverse"]))
            for col in range(C_WIDTH // LANES):
                q_col, k_pair = c["q16"][col], c["k16"][col]
                if m != 0:
                    w16 = wgt[:, col * LANES:(col + 1) * LANES].astype(jnp.bfloat16)
                    q_col, k_pair = q_col * w16, [k * w16 for k in k_pair]
                k_two = jnp.concatenate(k_pair, axis=0)
                parts.append(lax.dot_general(q_col, k_two, nt_dims, preferred_element_type=f32))
            pm = jnp.concatenate(parts, axis=1)
            if c["p"] is None:
                c["p"] = pm
            elif m >= SUBLANES:
                shape3 = (BLOCK // (2 * m), 2 * m, C_HEADS * BLOCK)
                p3, pm3, key3 = c["p"].reshape(shape3), pm.reshape(shape3), pair_key.reshape(shape3)
                q_rows = slice(0, m) if c["reverse"] else slice(m, 2 * m)
                new = jnp.where(key3[:, q_rows] < 2 * m, pm3[:, q_rows], p3[:, q_rows])
                halves = [p3[:, 0:m], new] if not c["reverse"] else [new, p3[:, m:2 * m]]
                c["p"] = jnp.concatenate(halves, axis=1).reshape(BLOCK, C_HEADS * BLOCK)
            else:
                c["p"] = jnp.where(pair_key < max(2 * m, 1), pm, c["p"])
    outs = []
    for c in chains:
        p = jnp.where(c["tri_ok"], c["p"], 0.0).astype(jnp.bfloat16)
        v_heads = jnp.concatenate([c["v16"] * head16[h] for h in range(C_HEADS)], axis=0)
        outs.append((jnp.dot(p, v_heads, preferred_element_type=f32) + c["o_inter"], c["new_state"]))
    return outs


def _mixer_c_kernel(q_ref, v_ref, lff_ref, lfb_ref, kkf_ref, kkb_ref, g_ref, w_ref, o_ref, accf_ref, accr_ref):
    lane_head = lax.broadcasted_iota(jnp.int32, (BLOCK, C_WIDTH), 1) // HEAD_DIM
    head16 = [(lane_head == h).astype(jnp.bfloat16) for h in range(C_HEADS)]
    lane_half = lax.broadcasted_iota(jnp.int32, (BLOCK, LANES), 1) // HEAD_DIM
    half16 = [(lane_half == hf).astype(jnp.bfloat16) for hf in range(2)]
    sr = lax.broadcasted_iota(jnp.int32, (C_WIDTH, C_WIDTH), 0) // HEAD_DIM
    sc = lax.broadcasted_iota(jnp.int32, (C_WIDTH, C_WIDTH), 1) // HEAD_DIM
    same_head = sr == sc
    head_ones16 = same_head.astype(jnp.bfloat16)
    t_i = lax.broadcasted_iota(jnp.int32, (BLOCK, C_HEADS * BLOCK), 0)
    s_i = lax.broadcasted_iota(jnp.int32, (BLOCK, C_HEADS * BLOCK), 1) % BLOCK
    pair_key = t_i ^ s_i
    r_i = lax.broadcasted_iota(jnp.int32, (BLOCK, BLOCK), 0)
    c_i = lax.broadcasted_iota(jnp.int32, (BLOCK, BLOCK), 1)
    tri_f, tri_r = (c_i <= r_i).astype(jnp.bfloat16), (c_i >= r_i).astype(jnp.bfloat16)
    ok_f, ok_r = s_i <= t_i, s_i >= t_i

    def run(steps, states):
        chains = []
        for si, step in enumerate(steps):
            for reverse in (False, True):
                n = (NBLK - 1 - step) if reverse else step
                r0 = n * BLOCK if isinstance(n, int) else pl.multiple_of(n * BLOCK, BLOCK)
                rows = pl.ds(r0, BLOCK)
                lf_ref, kk_ref = (lfb_ref, kkb_ref) if reverse else (lff_ref, kkf_ref)
                chains.append(dict(
                    reverse=reverse, rows=rows, q=q_ref[0, rows, :].astype(jnp.float32), v16=v_ref[0, rows, :],
                    kk=kk_ref[0, rows, :], lf=lf_ref[0, rows, :], tri16=tri_r if reverse else tri_f,
                    tri_ok=ok_r if reverse else ok_f, state=states[reverse], prev=None if si == 0 else len(chains) - 2))
        outs = _hgrn_chunks(chains, head16, half16, pair_key, same_head)
        for c, (o, _) in zip(chains, outs):
            (accr_ref if c["reverse"] else accf_ref)[c["rows"], :] = o
        return outs[-2][1], outs[-1][1]

    zero_state = jnp.zeros((C_WIDTH, C_WIDTH), jnp.float32)
    group = HGRN_STEPS_PER_ITER
    states = lax.fori_loop(0, NBLK // group, lambda j, st: run(tuple(group * j + u for u in range(group)), st),
                           (zero_state, zero_state))
    if NBLK % group:
        run(tuple(range(NBLK - NBLK % group, NBLK)), states)

    w = w_ref[...]

    def finish(n, carry):
        r0 = pl.multiple_of(n * BLOCK, BLOCK)
        y = accf_ref[pl.ds(r0, BLOCK), :] + accr_ref[pl.ds(r0, BLOCK), :]
        sq = y * y
        sq_hi = sq.astype(jnp.bfloat16)
        sq_lo = (sq - sq_hi.astype(jnp.float32)).astype(jnp.bfloat16)
        ms = (jnp.dot(sq_hi, head_ones16, preferred_element_type=jnp.float32)
              + jnp.dot(sq_lo, head_ones16, preferred_element_type=jnp.float32)) * (1.0 / HEAD_DIM)
        y = y * lax.rsqrt(ms + RMS_EPS) * w
        y = y * g_ref[0, pl.ds(r0, BLOCK), :].astype(jnp.float32)
        o_ref[0, pl.ds(r0, BLOCK), :] = y.astype(o_ref.dtype)
        return carry

    lax.fori_loop(0, NBLK, finish, 0, unroll=True)


def _mixer_c(q3, v3, lf3, kk3, g3, norm_w):
    blk = lambda j: pl.BlockSpec((1, LP, C_WIDTH), lambda b: (b, 0, j))
    return pl.pallas_call(
        _mixer_c_kernel,
        grid=(BATCH,),
        in_specs=[
            blk(4),
            blk(0),
            blk(0), blk(1),
            blk(0), blk(1),
            blk(3),
            pl.BlockSpec((1, C_WIDTH), lambda b: (0, 0)),
        ],
        out_specs=pl.BlockSpec((1, LP, C_WIDTH), lambda b: (b, 0, 0)),
        out_shape=jax.ShapeDtypeStruct((BATCH, LP, C_WIDTH), jnp.bfloat16),
        scratch_shapes=[pltpu.VMEM((LP, C_WIDTH), jnp.float32), pltpu.VMEM((LP, C_WIDTH), jnp.float32)],
        compiler_params=_cparams(("arbitrary",)),
        name="mixer_c_hgrn2",
    )(q3, v3, lf3, lf3, kk3, kk3, g3, norm_w.reshape(1, C_WIDTH))


def _outproj_last_kernel(ya_ref, yb_ref, yc_ref, wa_ref, wb_ref, wc_ref, h_ref, g_ref, b_ref, o_ref):
    n_rows = o_ref.shape[1]
    halves = (slice(0, n_rows // 2), slice(n_rows // 2, n_rows))
    accs = []
    for rows in halves:
        acc = jnp.dot(ya_ref[rows, :], wa_ref[...], preferred_element_type=jnp.float32)
        acc = acc + jnp.dot(yb_ref[rows, :], wb_ref[...], preferred_element_type=jnp.float32)
        accs.append(acc + jnp.dot(yc_ref[rows, :], wc_ref[...], preferred_element_type=jnp.float32))
    for rows, acc in zip(halves, accs):
        o_ref[0, rows, :] = _layer_norm(ALPHA * h_ref[rows, :] + acc, g_ref[...], b_ref[...])


def _outproj_last(ya, yb, yc, wa, wb, wc, h32, g, b):
    tl = SEQ // 2
    row = lambda n: pl.BlockSpec((pl.Element(tl), pl.Element(n)),
                                 lambda bi, j: (pl.multiple_of(bi * LP + BLOCK + j * tl, BLOCK), 0))
    full = lambda n: pl.BlockSpec((n, D_MODEL), lambda bi, j: (0, 0))
    vec = pl.BlockSpec((1, D_MODEL), lambda bi, j: (0, 0))
    return pl.pallas_call(
        _outproj_last_kernel,
        grid=(BATCH, SEQ // tl),
        in_specs=[row(A_WIDTH), row(B_WIDTH), row(C_WIDTH), full(A_WIDTH), full(B_WIDTH), full(C_WIDTH),
                  row(D_MODEL), vec, vec],
        out_specs=pl.BlockSpec((1, tl, D_MODEL), lambda bi, j: (bi, j, 0)),
        out_shape=jax.ShapeDtypeStruct((BATCH, SEQ, D_MODEL), jnp.float32),
        compiler_params=_cparams(("arbitrary", "arbitrary")),
        name="outproj_last_residual_ln",
    )(ya.reshape(M_ROWS, A_WIDTH), yb.reshape(M_ROWS, B_WIDTH), yc.reshape(M_ROWS, C_WIDTH), wa, wb, wc, h32,
      g.reshape(1, D_MODEL), b.reshape(1, D_MODEL))


def _regroup_w_in(w):
    def heads(lo, order):
        return [w[:, lo + h * HEAD_DIM: lo + (h + 1) * HEAD_DIM] for h in order]

    c = lambda lo, n: w[:, lo:lo + n]
    o_qa, o_ka, o_va, o_ga = 0, 256, 384, 512
    o_qb, o_kb, o_vb, o_gb = 768, 1280, 1792, 2304
    o_qc, o_ic, o_zf, o_zb, o_gc = 2816, 3072, 3328, 3584, 3840
    cat = lambda xs: jnp.concatenate(xs, axis=1).astype(jnp.bfloat16)
    w_q = cat(heads(o_qa, A_HEAD_ORDER) + [c(o_qb, 512)])
    w_k = cat([c(o_ka, 128), c(o_kb, 512)])
    w_p = cat([c(o_ic, 256), c(o_va, 128), c(o_vb, 512)])
    w_s = cat(heads(o_ga, A_HEAD_ORDER) + [c(o_gb, 512), c(o_gc, 256), c(o_qc, 256)])
    w_g = cat([c(o_zf, 256), c(o_zb, 256)])
    return jnp.concatenate([w_q, w_k, w_p, w_s, w_g], axis=1)


def _rope_tables():
    pos = np.arange(LP, dtype=np.float64) - PAD
    inv = ROPE_THETA ** (-np.arange(0, HEAD_DIM, 2, dtype=np.float64) / HEAD_DIM)
    ang = pos[:, None] * inv[None, :]
    cos, sin = np.cos(ang), np.sin(ang)
    cos_t = np.concatenate([cos, cos, cos, cos], axis=1)
    sin_t = np.concatenate([-sin, sin, -sin, sin], axis=1)
    return jnp.asarray(cos_t, jnp.float32), jnp.asarray(sin_t, jnp.float32)


def kernel(x, meta, emb_ln_g, emb_ln_b, w_in, w_out, a_sink, b_lam, b_subln_w, c_lb_logits, c_norm_w, ln_g, ln_b):
    cos_t, sin_t = _rope_tables()
    p_lb = jax.nn.softmax(c_lb_logits.astype(jnp.float32), axis=0)
    lb_all = jnp.cumsum(p_lb, axis=0) - p_lb[0:1]

    front_args = (x, meta, emb_ln_g, emb_ln_b)
    for l in range(DEPTH):
        lam_init = 0.8 - 0.6 * math.exp(-0.3 * l)
        lp32 = b_lam[l].astype(jnp.float32)
        lam = jnp.exp(jnp.sum(lp32[0] * lp32[1])) - jnp.exp(jnp.sum(lp32[2] * lp32[3])) + lam_init
        wo = w_out[l]
        wa = jnp.concatenate([wo[h * HEAD_DIM:(h + 1) * HEAD_DIM] for h in A_HEAD_ORDER], axis=0).astype(jnp.bfloat16)
        wb = wo[A_WIDTH:A_WIDTH + B_WIDTH].astype(jnp.bfloat16)
        wc = wo[A_WIDTH + B_WIDTH:].astype(jnp.bfloat16)
        sink = jnp.stack([a_sink[l, h] for h in A_HEAD_ORDER]).astype(jnp.float32)

        h32, *proj = _front(front_args, _regroup_w_in(w_in[l]), cos_t, sin_t, lb_all[l].reshape(1, C_WIDTH))
        q_all, k_all, p_all, s_all, lf_all, kk_all = [t.reshape(BATCH, LP, t.shape[-1]) for t in proj]

        ya = _mixer_a(sink, q_all, k_all, p_all, s_all)
        yb = _mixer_b(lam_init, lam.reshape(1), q_all, k_all, p_all, s_all, b_subln_w[l])
        yc = _mixer_c(s_all, p_all, lf_all, kk_all, s_all, c_norm_w[l])

        if l == DEPTH - 1:
            return _outproj_last(ya, yb, yc, wa, wb, wc, h32, ln_g[l], ln_b[l])
        front_args = (ya.reshape(M_ROWS, A_WIDTH), yb.reshape(M_ROWS, B_WIDTH), yc.reshape(M_ROWS, C_WIDTH),
                      wa, wb, wc, h32, ln_g[l], ln_b[l])
```

```python
import functools
import math

import jax
import jax.numpy as jnp
import numpy as np
from jax import lax
from jax.experimental import pallas as pl
from jax.experimental.pallas import tpu as pltpu

D_MODEL = 1024
BATCH = 8
SEQ = 2048
DEPTH = 2
N_META = 16
BLOCK = 128
PAD = BLOCK - N_META
WINDOW = 128
ROPE_THETA = 10000.0
HEAD_DIM = 64
A_HEADS = 4
A_KV_HEADS = 2
A_WIDTH = A_HEADS * HEAD_DIM
B_HEADS = 4
B_V_DIM = 2 * HEAD_DIM
B_WIDTH = B_HEADS * B_V_DIM
C_HEADS = 4
C_WIDTH = C_HEADS * HEAD_DIM
D_MIX = A_WIDTH + B_WIDTH + C_WIDTH
ALPHA = (2 * DEPTH) ** 0.25
LN_EPS = 1e-5
RMS_EPS = 1e-6
NEG = -1e30
F_MIN = 1e-30

LP = SEQ + N_META + PAD
NBLK = LP // BLOCK
M_ROWS = BATCH * LP
TM = LP // 4
LANES = 128
SUBLANES = 8
MXU_N = 256
LOG2E = math.log2(math.e)
Q_SCALE = HEAD_DIM ** -0.5 * LOG2E
Q0_B = 96
TQ_B = (LP - Q0_B) // 5
B_HEADS_PER_STEP = 1
SM_ROWS = 32
VMEM_LIMIT = 48 * 1024 * 1024

N_ROPE_Q = A_WIDTH + 2 * B_HEADS * HEAD_DIM
N_ROPE_K = A_KV_HEADS * HEAD_DIM + 2 * B_HEADS * HEAD_DIM
N_PLAIN = C_WIDTH + A_KV_HEADS * HEAD_DIM + B_WIDTH
N_SILU = A_WIDTH + B_WIDTH + C_WIDTH + C_WIDTH
N_GATE = 2 * C_WIDTH
N_IN = N_ROPE_Q + N_ROPE_K + N_PLAIN + N_SILU + N_GATE
A_HEAD_ORDER = (0, 2, 1, 3)

HGRN_LEVELS = (0, 1, 2, 4, 8, 16, 32, 64)
HGRN_STEPS_PER_ITER = 8


def _cparams(sem):
    return pltpu.CompilerParams(dimension_semantics=sem, vmem_limit_bytes=VMEM_LIMIT)


def _layer_norm(x, g, b):
    mu = jnp.mean(x, axis=-1, keepdims=True)
    xc = x - mu
    var = jnp.mean(xc * xc, axis=-1, keepdims=True)
    return xc * lax.rsqrt(var + LN_EPS) * g + b


def _front_embed_kernel(x_ref, meta_ref, g_ref, b_ref, w_ref, cos_ref, sin_ref, lb_ref,
                        h32_ref, q_ref, k_ref, p_ref, s_ref, lf_ref, kk_ref):
    first = pl.program_id(0) % (LP // TM) == 0
    g, b = g_ref[...], b_ref[...]
    half = TM // 2
    pad_meta = jnp.concatenate([jnp.zeros((PAD, D_MODEL), jnp.float32), meta_ref[...]], axis=0)
    top = _layer_norm(jnp.where(first, pad_meta, x_ref[0:BLOCK, :]), g, b)
    pad_row = lax.broadcasted_iota(jnp.int32, (BLOCK, 1), 0) < PAD
    top = jnp.where(jnp.logical_and(first, pad_row), 0.0, top)
    off = pl.multiple_of(jnp.where(first, 0, BLOCK), SUBLANES)
    outs = (q_ref, k_ref, p_ref, s_ref, lf_ref, kk_ref)
    h_a = jnp.concatenate([top, _layer_norm(x_ref[pl.ds(off, half - BLOCK), :], g, b)], axis=0)
    h32_ref[0:half, :] = h_a
    _inproj_body(h_a.astype(jnp.bfloat16), slice(0, half), w_ref, cos_ref, sin_ref, lb_ref, outs)
    h_b = _layer_norm(x_ref[pl.ds(off + (half - BLOCK), TM - half), :], g, b)
    h32_ref[half:TM, :] = h_b
    _inproj_body(h_b.astype(jnp.bfloat16), slice(half, TM), w_ref, cos_ref, sin_ref, lb_ref, outs)


def _front_outproj_kernel(ya_ref, yb_ref, yc_ref, wa_ref, wb_ref, wc_ref, h_ref, g_ref, b_ref,
                          w_ref, cos_ref, sin_ref, lb_ref, h32_ref, q_ref, k_ref, p_ref, s_ref, lf_ref, kk_ref):
    halves = (slice(0, TM // 2), slice(TM // 2, TM))
    accs = []
    for rows in halves:
        acc = jnp.dot(ya_ref[rows, :], wa_ref[...], preferred_element_type=jnp.float32)
        acc = acc + jnp.dot(yb_ref[rows, :], wb_ref[...], preferred_element_type=jnp.float32)
        accs.append(acc + jnp.dot(yc_ref[rows, :], wc_ref[...], preferred_element_type=jnp.float32))
    for rows, acc in zip(halves, accs):
        h = _layer_norm(ALPHA * h_ref[rows, :] + acc, g_ref[...], b_ref[...])
        h32_ref[rows, :] = h
        _inproj_body(h.astype(jnp.bfloat16), rows, w_ref, cos_ref, sin_ref, lb_ref,
                     (q_ref, k_ref, p_ref, s_ref, lf_ref, kk_ref))


def _front(prologue_args, w, cos_t, sin_t, lb):
    row = lambda n: pl.BlockSpec((TM, n), lambda i: (i, 0))
    const = lambda r, n: pl.BlockSpec((r, n), lambda i: (0, 0))
    vec = const(1, D_MODEL)
    tab_spec = pl.BlockSpec((TM, LANES), lambda i: (i % (LP // TM), 0))
    tiles_per_b = LP // TM
    if len(prologue_args) == 4:
        x, meta, g, b = prologue_args
        x_spec = pl.BlockSpec(
            (pl.Element(TM), pl.Element(D_MODEL)),
            lambda i: (pl.multiple_of((i // tiles_per_b) * SEQ + jnp.maximum((i % tiles_per_b) * TM - BLOCK, 0),
                                      SUBLANES), 0))
        body, name = _front_embed_kernel, "front_embed_inproj"
        args = (x.reshape(BATCH * SEQ, D_MODEL), meta, g.reshape(1, D_MODEL), b.reshape(1, D_MODEL))
        in_specs = [x_spec, const(N_META, D_MODEL), vec, vec]
    else:
        ya, yb, yc, wa, wb, wc, h32, g, b = prologue_args
        body, name = _front_outproj_kernel, "front_outproj_inproj"
        args = (ya, yb, yc, wa, wb, wc, h32, g.reshape(1, D_MODEL), b.reshape(1, D_MODEL))
        in_specs = [row(A_WIDTH), row(B_WIDTH), row(C_WIDTH), const(A_WIDTH, D_MODEL), const(B_WIDTH, D_MODEL),
                    const(C_WIDTH, D_MODEL), row(D_MODEL), vec, vec]
    widths = (D_MODEL, N_ROPE_Q, N_ROPE_K, N_PLAIN, N_SILU, N_GATE, N_GATE)
    dtypes = (jnp.float32,) + (jnp.bfloat16,) * 4 + (jnp.float32,) * 2
    return pl.pallas_call(
        body,
        grid=(M_ROWS // TM,),
        in_specs=in_specs + [const(D_MODEL, N_IN), tab_spec, tab_spec, const(1, C_WIDTH)],
        out_specs=[row(n) for n in widths],
        out_shape=[jax.ShapeDtypeStruct((M_ROWS, n), dt) for n, dt in zip(widths, dtypes)],
        compiler_params=_cparams(("arbitrary",)),
        name=name,
    )(*args, w, cos_t, sin_t, lb)


def _col_chunks(n):
    return [(c, min(MXU_N, n - c)) for c in range(0, n, MXU_N)]


def _rope_cols(acc, cos, sin_signed, first_half):
    partner = jnp.where(first_half, pltpu.roll(acc, LANES - HEAD_DIM // 2, 1), pltpu.roll(acc, HEAD_DIM // 2, 1))
    return acc * cos + partner * sin_signed


def _inproj_body(x, rows, w_ref, cos_ref, sin_ref, lb_ref, outs):
    q_ref, k_ref, p_ref, s_ref, lf_ref, kk_ref = [o.at[rows, :] for o in outs]
    n_rows = rows.stop - rows.start
    cos = cos_ref[rows, :]
    sin = sin_ref[rows, :]
    lane = lax.broadcasted_iota(jnp.int32, (n_rows, LANES), 1)
    first_half = (lane % HEAD_DIM) < (HEAD_DIM // 2)
    row = ((pl.program_id(0) % (LP // TM)) * TM + rows.start
           + lax.broadcasted_iota(jnp.int32, (n_rows, LANES), 0))
    valid = row >= PAD
    pieces = ([("rope_q", q_ref, c) for c in range(0, N_ROPE_Q, LANES)]
              + [("rope_k", k_ref, c) for c in range(0, N_ROPE_K, LANES)]
              + [("plain", p_ref, c) for c in range(0, N_PLAIN, LANES)]
              + [("silu", s_ref, c) for c in range(0, N_SILU, LANES)]
              + [("gate", None, c) for c in range(0, N_GATE, LANES)])
    for c0, cw in _col_chunks(N_IN):
        acc = jnp.dot(x, w_ref[:, c0:c0 + cw], preferred_element_type=jnp.float32)
        for j in range(cw // LANES):
            kind, dst, dc = pieces[c0 // LANES + j]
            a = acc[:, j * LANES:(j + 1) * LANES]
            if kind == "rope_q":
                dst[:, dc:dc + LANES] = (_rope_cols(a, cos, sin, first_half) * Q_SCALE).astype(dst.dtype)
            elif kind == "rope_k":
                dst[:, dc:dc + LANES] = _rope_cols(a, cos, sin, first_half).astype(dst.dtype)
            elif kind == "plain":
                dst[:, dc:dc + LANES] = a.astype(dst.dtype)
            elif kind == "silu":
                dst[:, dc:dc + LANES] = (a * jax.nn.sigmoid(a)).astype(dst.dtype)
            else:
                lb = lb_ref[:, dc % C_WIDTH: dc % C_WIDTH + LANES]
                f = lb + (1.0 - lb) * jax.nn.sigmoid(a)
                lf_ref[:, dc:dc + LANES] = jnp.where(valid, jnp.log2(jnp.maximum(f, F_MIN)), 0.0)
                kk_ref[:, dc:dc + LANES] = jnp.where(valid, (1.0 - lb) * jax.nn.sigmoid(-a), 0.0)


def _mixer_a_kernel(sink_ref, q_ref, k_ref, v_ref, g_ref, o_ref):
    two = 2 * BLOCK
    row = lax.broadcasted_iota(jnp.int32, (two, 4 * BLOCK), 0) % BLOCK
    col = lax.broadcasted_iota(jnp.int32, (two, 4 * BLOCK), 1)
    is_meta = (col < BLOCK) & (col >= PAD)
    top = lax.broadcasted_iota(jnp.int32, (two, 1), 0) < BLOCK
    lane_hi = lax.broadcasted_iota(jnp.int32, (BLOCK, LANES), 1) >= HEAD_DIM
    masks = {}

    def band_start(n):
        return min(max((n - 1) * BLOCK, 0), LP - 3 * BLOCK)

    def scores(n):
        start = band_start(n)
        key = (n * BLOCK - start, start < BLOCK)
        if key not in masks:
            kpos = start + col - BLOCK
            band = (col >= BLOCK) & (jnp.abs(n * BLOCK + row - kpos) <= WINDOW)
            if start < BLOCK:
                band = band & (kpos >= BLOCK)
            ok = is_meta | band
            masks[key] = (jnp.where(ok, 1.0, 0.0), jnp.where(ok, 0.0, NEG))
        keep, fill = masks[key]
        kcat = jnp.concatenate([k_ref[0, 0:BLOCK, :], k_ref[0, start:start + 3 * BLOCK, :]], axis=0)
        qcols = [q_ref[0, n * BLOCK:(n + 1) * BLOCK, cv * LANES:(cv + 1) * LANES] for cv in range(2)]
        out = []
        for hf in range(2):
            sel = lane_hi if hf else jnp.logical_not(lane_hi)
            q2 = jnp.concatenate([jnp.where(sel, qc, jnp.zeros_like(qc)) for qc in qcols], axis=0)
            s = lax.dot_general(q2, kcat, (((1,), (1,)), ((), ())), preferred_element_type=jnp.float32)
            out.append(s * keep + fill)
        return out

    def softmax_pv(n, s_halves):
        start = band_start(n)
        vcat = jnp.concatenate([v_ref[0, 0:BLOCK, :], v_ref[0, start:start + 3 * BLOCK, :]], axis=0)
        rows = slice(n * BLOCK, (n + 1) * BLOCK)
        outs = []
        for hf, s in enumerate(s_halves):
            sink = jnp.where(top, sink_ref[hf], sink_ref[2 + hf]) * LOG2E
            m = jnp.maximum(jnp.max(s, axis=-1, keepdims=True), sink)
            e = jnp.exp2(s - m)
            den = jnp.sum(e, axis=-1, keepdims=True) + jnp.exp2(sink - m)
            p = (e * (1.0 / den)).astype(jnp.bfloat16)
            outs.append(jnp.dot(p, vcat, preferred_element_type=jnp.float32))
        for cv in range(2):
            o = jnp.where(lane_hi, outs[1][cv * BLOCK:(cv + 1) * BLOCK], outs[0][cv * BLOCK:(cv + 1) * BLOCK])
            y = o * g_ref[0, rows, cv * LANES:(cv + 1) * LANES].astype(jnp.float32)
            o_ref[0, rows, cv * LANES:(cv + 1) * LANES] = y.astype(o_ref.dtype)

    ahead = 2
    pending = {}
    for n in range(NBLK + ahead):
        if n < NBLK:
            pending[n] = scores(n)
        if n >= ahead:
            softmax_pv(n - ahead, pending.pop(n - ahead))


def _mixer_a(sink, q3, k3, v3, g3):
    return pl.pallas_call(
        _mixer_a_kernel,
        grid=(BATCH,),
        in_specs=[
            pl.BlockSpec(memory_space=pltpu.SMEM),
            pl.BlockSpec((1, LP, A_WIDTH), lambda b: (b, 0, 0)),
            pl.BlockSpec((1, LP, LANES), lambda b: (b, 0, 0)),
            pl.BlockSpec((1, LP, LANES), lambda b: (b, 0, C_WIDTH // LANES)),
            pl.BlockSpec((1, LP, A_WIDTH), lambda b: (b, 0, 0)),
        ],
        out_specs=pl.BlockSpec((1, LP, A_WIDTH), lambda b: (b, 0, 0)),
        out_shape=jax.ShapeDtypeStruct((BATCH, LP, A_WIDTH), jnp.bfloat16),
        compiler_params=_cparams(("arbitrary",)),
        name="mixer_a_window_gqa",
    )(sink, q3, k3, v3, g3)


def _mixer_b_kernel(lam_init, lam_ref, *refs):
    hps = B_HEADS_PER_STEP
    q_refs, k_refs, v_refs, g_refs = (refs[i * hps:(i + 1) * hps] for i in range(4))
    w_ref, o_ref, s_a, s_b, a_a, a_b = refs[4 * hps:]
    lam = lam_ref[0]
    lane_hi = lax.broadcasted_iota(jnp.int32, (TQ_B, LANES), 1) >= HEAD_DIM
    key_ok = lax.broadcasted_iota(jnp.int32, (TQ_B, BLOCK), 1) >= PAD
    s_bufs, a_bufs = (s_a, s_b), (a_a, a_b)
    nt = (LP - Q0_B) // TQ_B
    nt_dims = (((1,), (1,)), ((), ()))
    tile_rows = lambda t: slice(Q0_B + t * TQ_B, Q0_B + (t + 1) * TQ_B)
    o_ref[0, 0:Q0_B, :] = jnp.zeros((Q0_B, hps * LANES), o_ref.dtype)
    stages = [(hh, t) for hh in range(hps) for t in range(nt)]

    def scores(i):
        hh, t = stages[i]
        q_ref, k_ref = q_refs[hh], k_refs[hh]
        s_buf = s_bufs[i % 2]
        q = q_ref[0, tile_rows(t), :]
        for c in range(2):
            qm = jnp.where(lane_hi if c else jnp.logical_not(lane_hi), q, jnp.zeros_like(q))
            s0 = lax.dot_general(qm, k_ref[0, 0:BLOCK, :], nt_dims, preferred_element_type=jnp.float32)
            s_buf[c, :, 0:BLOCK] = jnp.where(key_ok, s0, NEG)
            s_buf[c, :, BLOCK:LP] = lax.dot_general(qm, k_ref[0, BLOCK:LP, :], nt_dims,
                                                    preferred_element_type=jnp.float32)

    def softmax_pv(i):
        hh, t = stages[i]
        v_ref, g_ref = v_refs[hh], g_refs[hh]
        cols = slice(hh * LANES, (hh + 1) * LANES)
        w = w_ref[:, cols] * (1.0 - lam_init)
        s_buf, a_buf = s_bufs[i % 2], a_bufs[i % 2]
        inv_den0 = []
        for r in range(TQ_B // SM_ROWS):
            rows = slice(r * SM_ROWS, (r + 1) * SM_ROWS)
            es, dens = [], []
            for c in range(2):
                s = s_buf[c, rows, :]
                m = jnp.max(s, axis=-1, keepdims=True)
                e = jnp.exp2(s - m)
                es.append(e)
                dens.append(jnp.sum(e, axis=-1, keepdims=True))
            a_buf[rows, :] = (es[0] - es[1] * (dens[0] * lam / dens[1])).astype(jnp.bfloat16)
            inv_den0.append(1.0 / dens[0])
        o = jnp.dot(a_buf[...], v_ref[0], preferred_element_type=jnp.float32)
        o = o * jnp.concatenate(inv_den0, axis=0)
        ms = jnp.mean(o * o, axis=-1, keepdims=True)
        y = o * lax.rsqrt(ms + RMS_EPS) * w
        y = y * g_ref[0, tile_rows(t), :].astype(jnp.float32)
        o_ref[0, tile_rows(t), cols] = y.astype(o_ref.dtype)

    for i in range(len(stages) + 1):
        if i < len(stages):
            scores(i)
        if i >= 1:
            softmax_pv(i - 1)


def _mixer_b(lam_init, lam, q3, k3, v3, g3, subln_w):
    qoff = A_WIDTH // LANES
    koff = (A_KV_HEADS * HEAD_DIM) // LANES
    voff = (C_WIDTH + A_KV_HEADS * HEAD_DIM) // LANES
    goff = A_WIDTH // LANES
    hps = B_HEADS_PER_STEP
    head_blocks = lambda off: [pl.BlockSpec((1, LP, LANES), lambda b, hp, hh=hh: (b, 0, off + hp * hps + hh))
                               for hh in range(hps)]
    return pl.pallas_call(
        functools.partial(_mixer_b_kernel, lam_init),
        grid=(BATCH, B_HEADS // hps),
        in_specs=([pl.BlockSpec(memory_space=pltpu.SMEM)]
                  + head_blocks(qoff) + head_blocks(koff) + head_blocks(voff) + head_blocks(goff)
                  + [pl.BlockSpec((1, hps * B_V_DIM), lambda b, hp: (0, hp))]),
        out_specs=pl.BlockSpec((1, LP, hps * LANES), lambda b, hp: (b, 0, hp)),
        out_shape=jax.ShapeDtypeStruct((BATCH, LP, B_WIDTH), jnp.bfloat16),
        scratch_shapes=[pltpu.VMEM((2, TQ_B, LP), jnp.float32), pltpu.VMEM((2, TQ_B, LP), jnp.float32),
                        pltpu.VMEM((TQ_B, LP), jnp.bfloat16), pltpu.VMEM((TQ_B, LP), jnp.bfloat16)],
        compiler_params=_cparams(("arbitrary", "arbitrary")),
        name="mixer_b_diff_attn",
    )(lam, *([q3] * hps), *([k3] * hps), *([v3] * hps), *([g3] * hps), subln_w.reshape(1, B_WIDTH))


def _split2(x):
    hi = x.astype(jnp.bfloat16)
    lo = (x - hi.astype(jnp.float32)).astype(jnp.bfloat16)
    return hi, lo


def _hgrn_anchor(b, m, reverse):
    a_off = m if reverse else m - 1
    if 2 * m >= SUBLANES:
        b3 = b.reshape(BLOCK // (2 * m), 2 * m, C_WIDTH)
        return jnp.broadcast_to(b3[:, a_off:a_off + 1, :], b3.shape).reshape(BLOCK, C_WIDTH)
    b3 = b.reshape(BLOCK // SUBLANES, SUBLANES, C_WIDTH)
    sub = lax.broadcasted_iota(jnp.int32, b3.shape, 1)
    anchor = None
    for blk in range(SUBLANES // (2 * m)):
        r = blk * 2 * m + a_off
        cand = jnp.broadcast_to(b3[:, r:r + 1, :], b3.shape)
        anchor = cand if anchor is None else jnp.where(sub >= blk * 2 * m, cand, anchor)
    return anchor.reshape(BLOCK, C_WIDTH)


def _hgrn_level_exponent(b, m, reverse):
    if m < SUBLANES:
        return -jnp.abs(b - _hgrn_anchor(b, m, reverse))
    a_off = m if reverse else m - 1
    b3 = b.reshape(BLOCK // (2 * m), 2 * m, C_WIDTH)
    anchor = b3[:, a_off:a_off + 1, :]
    left, right = b3[:, 0:m], b3[:, m:2 * m]
    halves = [left - anchor, anchor - right] if reverse else [anchor - left, right - anchor]
    return jnp.concatenate(halves, axis=1).reshape(BLOCK, C_WIDTH)


def _hgrn_chunks(chains, head16, half16, pair_key, same_head):
    nt_dims = (((1,), (1,)), ((), ()))
    f32 = jnp.float32
    for c in chains:
        hi, lo = _split2(c["lf"])
        c["b"] = (jnp.dot(c["tri16"], hi, preferred_element_type=f32)
                  + jnp.dot(c["tri16"], lo, preferred_element_type=f32))
    for c in chains:
        state = c["state"] if c["prev"] is None else chains[c["prev"]]["new_state"]
        edge = 0 if c["reverse"] else BLOCK - 1
        b, b_all = c["b"], c["b"][edge:edge + 1, :]
        q_in = (c["q"] * jnp.exp2(b)).astype(jnp.bfloat16)
        c["o_inter"] = lax.dot_general(q_in, state.astype(jnp.bfloat16), nt_dims, preferred_element_type=f32)
        k_out = (c["kk"] * jnp.exp2(b_all - b)).astype(jnp.bfloat16)
        u_t = lax.dot_general(c["v16"], k_out, (((0,), (0,)), ((), ())), preferred_element_type=f32)
        c["new_state"] = jnp.exp2(b_all) * state + jnp.where(same_head, u_t, 0.0)
        c["p"] = None
    for m in reversed(HGRN_LEVELS):
        for c in chains:
            if "q16" not in c:
                cols = [slice(col * LANES, (col + 1) * LANES) for col in range(C_WIDTH // LANES)]
                c["q16"] = [c["q"][:, cs].astype(jnp.bfloat16) for cs in cols]
                c["k16"] = [[c["kk"][:, cs].astype(jnp.bfloat16) * half16[hf] for hf in range(2)] for cs in cols]
            parts = []
            if m != 0:
                wgt = jnp.exp2(_hgrn_level_exponent(c["b"], m, c["reverse"]))
            for col in range(C_WIDTH // LANES):
                q_col, k_pair = c["q16"][col], c["k16"][col]
                if m != 0:
                    w16 = wgt[:, col * LANES:(col + 1) * LANES].astype(jnp.bfloat16)
                    q_col, k_pair = q_col * w16, [k * w16 for k in k_pair]
                k_two = jnp.concatenate(k_pair, axis=0)
                parts.append(lax.dot_general(q_col, k_two, nt_dims, preferred_element_type=f32))
            pm = jnp.concatenate(parts, axis=1)
            if c["p"] is None:
                c["p"] = pm
            elif m >= SUBLANES:
                shape3 = (BLOCK // (2 * m), 2 * m, C_HEADS * BLOCK)
                p3, pm3, key3 = c["p"].reshape(shape3), pm.reshape(shape3), pair_key.reshape(shape3)
                q_rows = slice(0, m) if c["reverse"] else slice(m, 2 * m)
                new = jnp.where(key3[:, q_rows] < 2 * m, pm3[:, q_rows], p3[:, q_rows])
                halves = [p3[:, 0:m], new] if not c["reverse"] else [new, p3[:, m:2 * m]]
                c["p"] = jnp.concatenate(halves, axis=1).reshape(BLOCK, C_HEADS * BLOCK)
            else:
                c["p"] = jnp.where(pair_key < max(2 * m, 1), pm, c["p"])
    outs = []
    for c in chains:
        p = jnp.where(c["tri_ok"], c["p"], 0.0).astype(jnp.bfloat16)
        v_heads = jnp.concatenate([c["v16"] * head16[h] for h in range(C_HEADS)], axis=0)
        outs.append((jnp.dot(p, v_heads, preferred_element_type=f32) + c["o_inter"], c["new_state"]))
    return outs


def _mixer_c_kernel(q_ref, v_ref, lff_ref, lfb_ref, kkf_ref, kkb_ref, g_ref, w_ref, o_ref, accf_ref, accr_ref):
    lane_head = lax.broadcasted_iota(jnp.int32, (BLOCK, C_WIDTH), 1) // HEAD_DIM
    head16 = [(lane_head == h).astype(jnp.bfloat16) for h in range(C_HEADS)]
    lane_half = lax.broadcasted_iota(jnp.int32, (BLOCK, LANES), 1) // HEAD_DIM
    half16 = [(lane_half == hf).astype(jnp.bfloat16) for hf in range(2)]
    sr = lax.broadcasted_iota(jnp.int32, (C_WIDTH, C_WIDTH), 0) // HEAD_DIM
    sc = lax.broadcasted_iota(jnp.int32, (C_WIDTH, C_WIDTH), 1) // HEAD_DIM
    same_head = sr == sc
    head_ones16 = same_head.astype(jnp.bfloat16)
    t_i = lax.broadcasted_iota(jnp.int32, (BLOCK, C_HEADS * BLOCK), 0)
    s_i = lax.broadcasted_iota(jnp.int32, (BLOCK, C_HEADS * BLOCK), 1) % BLOCK
    pair_key = t_i ^ s_i
    r_i = lax.broadcasted_iota(jnp.int32, (BLOCK, BLOCK), 0)
    c_i = lax.broadcasted_iota(jnp.int32, (BLOCK, BLOCK), 1)
    tri_f, tri_r = (c_i <= r_i).astype(jnp.bfloat16), (c_i >= r_i).astype(jnp.bfloat16)
    ok_f, ok_r = s_i <= t_i, s_i >= t_i

    def run(steps, states):
        chains = []
        for si, step in enumerate(steps):
            for reverse in (False, True):
                n = (NBLK - 1 - step) if reverse else step
                r0 = n * BLOCK if isinstance(n, int) else pl.multiple_of(n * BLOCK, BLOCK)
                rows = pl.ds(r0, BLOCK)
                lf_ref, kk_ref = (lfb_ref, kkb_ref) if reverse else (lff_ref, kkf_ref)
                chains.append(dict(
                    reverse=reverse, rows=rows, q=q_ref[0, rows, :].astype(jnp.float32), v16=v_ref[0, rows, :],
                    kk=kk_ref[0, rows, :], lf=lf_ref[0, rows, :], tri16=tri_r if reverse else tri_f,
                    tri_ok=ok_r if reverse else ok_f, state=states[reverse], prev=None if si == 0 else len(chains) - 2))
        outs = _hgrn_chunks(chains, head16, half16, pair_key, same_head)
        for c, (o, _) in zip(chains, outs):
            (accr_ref if c["reverse"] else accf_ref)[c["rows"], :] = o
        return outs[-2][1], outs[-1][1]

    zero_state = jnp.zeros((C_WIDTH, C_WIDTH), jnp.float32)
    group = HGRN_STEPS_PER_ITER
    states = lax.fori_loop(0, NBLK // group, lambda j, st: run(tuple(group * j + u for u in range(group)), st),
                           (zero_state, zero_state))
    if NBLK % group:
        run(tuple(range(NBLK - NBLK % group, NBLK)), states)

    w = w_ref[...]

    def finish(n, carry):
        r0 = pl.multiple_of(n * BLOCK, BLOCK)
        y = accf_ref[pl.ds(r0, BLOCK), :] + accr_ref[pl.ds(r0, BLOCK), :]
        sq = y * y
        sq_hi = sq.astype(jnp.bfloat16)
        sq_lo = (sq - sq_hi.astype(jnp.float32)).astype(jnp.bfloat16)
        ms = (jnp.dot(sq_hi, head_ones16, preferred_element_type=jnp.float32)
              + jnp.dot(sq_lo, head_ones16, preferred_element_type=jnp.float32)) * (1.0 / HEAD_DIM)
        y = y * lax.rsqrt(ms + RMS_EPS) * w
        y = y * g_ref[0, pl.ds(r0, BLOCK), :].astype(jnp.float32)
        o_ref[0, pl.ds(r0, BLOCK), :] = y.astype(o_ref.dtype)
        return carry

    lax.fori_loop(0, NBLK, finish, 0, unroll=True)


def _mixer_c(q3, v3, lf3, kk3, g3, norm_w):
    blk = lambda j: pl.BlockSpec((1, LP, C_WIDTH), lambda b: (b, 0, j))
    return pl.pallas_call(
        _mixer_c_kernel,
        grid=(BATCH,),
        in_specs=[
            blk(4),
            blk(0),
            blk(0), blk(1),
            blk(0), blk(1),
            blk(3),
            pl.BlockSpec((1, C_WIDTH), lambda b: (0, 0)),
        ],
        out_specs=pl.BlockSpec((1, LP, C_WIDTH), lambda b: (b, 0, 0)),
        out_shape=jax.ShapeDtypeStruct((BATCH, LP, C_WIDTH), jnp.bfloat16),
        scratch_shapes=[pltpu.VMEM((LP, C_WIDTH), jnp.float32), pltpu.VMEM((LP, C_WIDTH), jnp.float32)],
        compiler_params=_cparams(("arbitrary",)),
        name="mixer_c_hgrn2",
    )(q3, v3, lf3, lf3, kk3, kk3, g3, norm_w.reshape(1, C_WIDTH))


def _outproj_last_kernel(ya_ref, yb_ref, yc_ref, wa_ref, wb_ref, wc_ref, h_ref, g_ref, b_ref, o_ref):
    n_rows = o_ref.shape[1]
    halves = (slice(0, n_rows // 2), slice(n_rows // 2, n_rows))
    accs = []
    for rows in halves:
        acc = jnp.dot(ya_ref[rows, :], wa_ref[...], preferred_element_type=jnp.float32)
        acc = acc + jnp.dot(yb_ref[rows, :], wb_ref[...], preferred_element_type=jnp.float32)
        accs.append(acc + jnp.dot(yc_ref[rows, :], wc_ref[...], preferred_element_type=jnp.float32))
    for rows, acc in zip(halves, accs):
        o_ref[0, rows, :] = _layer_norm(ALPHA * h_ref[rows, :] + acc, g_ref[...], b_ref[...])


def _outproj_last(ya, yb, yc, wa, wb, wc, h32, g, b):
    tl = SEQ // 2
    row = lambda n: pl.BlockSpec((pl.Element(tl), pl.Element(n)),
                                 lambda bi, j: (pl.multiple_of(bi * LP + BLOCK + j * tl, BLOCK), 0))
    full = lambda n: pl.BlockSpec((n, D_MODEL), lambda bi, j: (0, 0))
    vec = pl.BlockSpec((1, D_MODEL), lambda bi, j: (0, 0))
    return pl.pallas_call(
        _outproj_last_kernel,
        grid=(BATCH, SEQ // tl),
        in_specs=[row(A_WIDTH), row(B_WIDTH), row(C_WIDTH), full(A_WIDTH), full(B_WIDTH), full(C_WIDTH),
                  row(D_MODEL), vec, vec],
        out_specs=pl.BlockSpec((1, tl, D_MODEL), lambda bi, j: (bi, j, 0)),
        out_shape=jax.ShapeDtypeStruct((BATCH, SEQ, D_MODEL), jnp.float32),
        compiler_params=_cparams(("arbitrary", "arbitrary")),
        name="outproj_last_residual_ln",
    )(ya.reshape(M_ROWS, A_WIDTH), yb.reshape(M_ROWS, B_WIDTH), yc.reshape(M_ROWS, C_WIDTH), wa, wb, wc, h32,
      g.reshape(1, D_MODEL), b.reshape(1, D_MODEL))


def _regroup_w_in(w):
    def heads(lo, order):
        return [w[:, lo + h * HEAD_DIM: lo + (h + 1) * HEAD_DIM] for h in order]

    c = lambda lo, n: w[:, lo:lo + n]
    o_qa, o_ka, o_va, o_ga = 0, 256, 384, 512
    o_qb, o_kb, o_vb, o_gb = 768, 1280, 1792, 2304
    o_qc, o_ic, o_zf, o_zb, o_gc = 2816, 3072, 3328, 3584, 3840
    cat = lambda xs: jnp.concatenate(xs, axis=1).astype(jnp.bfloat16)
    w_q = cat(heads(o_qa, A_HEAD_ORDER) + [c(o_qb, 512)])
    w_k = cat([c(o_ka, 128), c(o_kb, 512)])
    w_p = cat([c(o_ic, 256), c(o_va, 128), c(o_vb, 512)])
    w_s = cat(heads(o_ga, A_HEAD_ORDER) + [c(o_gb, 512), c(o_gc, 256), c(o_qc, 256)])
    w_g = cat([c(o_zf, 256), c(o_zb, 256)])
    return jnp.concatenate([w_q, w_k, w_p, w_s, w_g], axis=1)


def _rope_tables():
    pos = np.arange(LP, dtype=np.float64) - PAD
    inv = ROPE_THETA ** (-np.arange(0, HEAD_DIM, 2, dtype=np.float64) / HEAD_DIM)
    ang = pos[:, None] * inv[None, :]
    cos, sin = np.cos(ang), np.sin(ang)
    cos_t = np.concatenate([cos, cos, cos, cos], axis=1)
    sin_t = np.concatenate([-sin, sin, -sin, sin], axis=1)
    return jnp.asarray(cos_t, jnp.float32), jnp.asarray(sin_t, jnp.float32)


def kernel(x, meta, emb_ln_g, emb_ln_b, w_in, w_out, a_sink, b_lam, b_subln_w, c_lb_logits, c_norm_w, ln_g, ln_b):
    cos_t, sin_t = _rope_tables()
    p_lb = jax.nn.softmax(c_lb_logits.astype(jnp.float32), axis=0)
    lb_all = jnp.cumsum(p_lb, axis=0) - p_lb[0:1]

    front_args = (x, meta, emb_ln_g, emb_ln_b)
    for l in range(DEPTH):
        lam_init = 0.8 - 0.6 * math.exp(-0.3 * l)
        lp32 = b_lam[l].astype(jnp.float32)
        lam = jnp.exp(jnp.sum(lp32[0] * lp32[1])) - jnp.exp(jnp.sum(lp32[2] * lp32[3])) + lam_init
        wo = w_out[l]
        wa = jnp.concatenate([wo[h * HEAD_DIM:(h + 1) * HEAD_DIM] for h in A_HEAD_ORDER], axis=0).astype(jnp.bfloat16)
        wb = wo[A_WIDTH:A_WIDTH + B_WIDTH].astype(jnp.bfloat16)
        wc = wo[A_WIDTH + B_WIDTH:].astype(jnp.bfloat16)
        sink = jnp.stack([a_sink[l, h] for h in A_HEAD_ORDER]).astype(jnp.float32)

        h32, *proj = _front(front_args, _regroup_w_in(w_in[l]), cos_t, sin_t, lb_all[l].reshape(1, C_WIDTH))
        q_all, k_all, p_all, s_all, lf_all, kk_all = [t.reshape(BATCH, LP, t.shape[-1]) for t in proj]

        ya = _mixer_a(sink, q_all, k_all, p_all, s_all)
        yb = _mixer_b(lam_init, lam.reshape(1), q_all, k_all, p_all, s_all, b_subln_w[l])
        yc = _mixer_c(s_all, p_all, lf_all, kk_all, s_all, c_norm_w[l])

        if l == DEPTH - 1:
            return _outproj_last(ya, yb, yc, wa, wb, wc, h32, ln_g[l], ln_b[l])
        front_args = (ya.reshape(M_ROWS, A_WIDTH), yb.reshape(M_ROWS, B_WIDTH), yc.reshape(M_ROWS, C_WIDTH),
                      wa, wb, wc, h32, ln_g[l], ln_b[l])
```

```python
import functools
import math

import jax
import jax.numpy as jnp
import numpy as np
from jax import lax
from jax.experimental import pallas as pl
from jax.experimental.pallas import tpu as pltpu

D_MODEL = 1024
BATCH = 8
SEQ = 2048
DEPTH = 2
N_META = 16
BLOCK = 128
PAD = BLOCK - N_META
WINDOW = 128
ROPE_THETA = 10000.0
HEAD_DIM = 64
A_HEADS = 4
A_KV_HEADS = 2
A_WIDTH = A_HEADS * HEAD_DIM
B_HEADS = 4
B_V_DIM = 2 * HEAD_DIM
B_WIDTH = B_HEADS * B_V_DIM
C_HEADS = 4
C_WIDTH = C_HEADS * HEAD_DIM
D_MIX = A_WIDTH + B_WIDTH + C_WIDTH
ALPHA = (2 * DEPTH) ** 0.25
LN_EPS = 1e-5
RMS_EPS = 1e-6
NEG = -1e30
F_MIN = 1e-30

LP = SEQ + N_META + PAD
NBLK = LP // BLOCK
M_ROWS = BATCH * LP
TM = LP // 4
LANES = 128
SUBLANES = 8
MXU_N = 256
LOG2E = math.log2(math.e)
Q_SCALE = HEAD_DIM ** -0.5 * LOG2E
Q0_B = 96
TQ_B = (LP - Q0_B) // 5
B_HEADS_PER_STEP = 1
SM_ROWS = 32
VMEM_LIMIT = 48 * 1024 * 1024

N_ROPE_Q = A_WIDTH + 2 * B_HEADS * HEAD_DIM
N_ROPE_K = A_KV_HEADS * HEAD_DIM + 2 * B_HEADS * HEAD_DIM
N_PLAIN = C_WIDTH + A_KV_HEADS * HEAD_DIM + B_WIDTH
N_SILU = A_WIDTH + B_WIDTH + C_WIDTH + C_WIDTH
N_GATE = 2 * C_WIDTH
N_IN = N_ROPE_Q + N_ROPE_K + N_PLAIN + N_SILU + N_GATE
A_HEAD_ORDER = (0, 2, 1, 3)

HGRN_LEVELS = (0, 1, 2, 4, 8, 16, 32, 64)
HGRN_STEPS_PER_ITER = 8


def _cparams(sem):
    return pltpu.CompilerParams(dimension_semantics=sem, vmem_limit_bytes=VMEM_LIMIT)


def _layer_norm(x, g, b):
    mu = jnp.mean(x, axis=-1, keepdims=True)
    xc = x - mu
    var = jnp.mean(xc * xc, axis=-1, keepdims=True)
    return xc * lax.rsqrt(var + LN_EPS) * g + b


def _front_embed_kernel(x_ref, meta_ref, g_ref, b_ref, w_ref, cos_ref, sin_ref, lb_ref,
                        h32_ref, q_ref, k_ref, p_ref, s_ref, lf_ref, kk_ref):
    first = pl.program_id(0) % (LP // TM) == 0
    g, b = g_ref[...], b_ref[...]
    half = TM // 2
    pad_meta = jnp.concatenate([jnp.zeros((PAD, D_MODEL), jnp.float32), meta_ref[...]], axis=0)
    top = _layer_norm(jnp.where(first, pad_meta, x_ref[0:BLOCK, :]), g, b)
    pad_row = lax.broadcasted_iota(jnp.int32, (BLOCK, 1), 0) < PAD
    top = jnp.where(jnp.logical_and(first, pad_row), 0.0, top)
    off = pl.multiple_of(jnp.where(first, 0, BLOCK), SUBLANES)
    outs = (q_ref, k_ref, p_ref, s_ref, lf_ref, kk_ref)
    h_a = jnp.concatenate([top, _layer_norm(x_ref[pl.ds(off, half - BLOCK), :], g, b)], axis=0)
    h32_ref[0:half, :] = h_a
    _inproj_body(h_a.astype(jnp.bfloat16), slice(0, half), w_ref, cos_ref, sin_ref, lb_ref, outs)
    h_b = _layer_norm(x_ref[pl.ds(off + (half - BLOCK), TM - half), :], g, b)
    h32_ref[half:TM, :] = h_b
    _inproj_body(h_b.astype(jnp.bfloat16), slice(half, TM), w_ref, cos_ref, sin_ref, lb_ref, outs)


def _front_outproj_kernel(ya_ref, yb_ref, yc_ref, wa_ref, wb_ref, wc_ref, h_ref, g_ref, b_ref,
                          w_ref, cos_ref, sin_ref, lb_ref, h32_ref, q_ref, k_ref, p_ref, s_ref, lf_ref, kk_ref):
    halves = (slice(0, TM // 2), slice(TM // 2, TM))
    accs = []
    for rows in halves:
        acc = jnp.dot(ya_ref[rows, :], wa_ref[...], preferred_element_type=jnp.float32)
        acc = acc + jnp.dot(yb_ref[rows, :], wb_ref[...], preferred_element_type=jnp.float32)
        accs.append(acc + jnp.dot(yc_ref[rows, :], wc_ref[...], preferred_element_type=jnp.float32))
    for rows, acc in zip(halves, accs):
        h = _layer_norm(ALPHA * h_ref[rows, :] + acc, g_ref[...], b_ref[...])
        h32_ref[rows, :] = h
        _inproj_body(h.astype(jnp.bfloat16), rows, w_ref, cos_ref, sin_ref, lb_ref,
                     (q_ref, k_ref, p_ref, s_ref, lf_ref, kk_ref))


def _front(prologue_args, w, cos_t, sin_t, lb):
    row = lambda n: pl.BlockSpec((TM, n), lambda i: (i, 0))
    const = lambda r, n: pl.BlockSpec((r, n), lambda i: (0, 0))
    vec = const(1, D_MODEL)
    tab_spec = pl.BlockSpec((TM, LANES), lambda i: (i % (LP // TM), 0))
    tiles_per_b = LP // TM
    if len(prologue_args) == 4:
        x, meta, g, b = prologue_args
        x_spec = pl.BlockSpec(
            (pl.Element(TM), pl.Element(D_MODEL)),
            lambda i: (pl.multiple_of((i // tiles_per_b) * SEQ + jnp.maximum((i % tiles_per_b) * TM - BLOCK, 0),
                                      SUBLANES), 0))
        body, name = _front_embed_kernel, "front_embed_inproj"
        args = (x.reshape(BATCH * SEQ, D_MODEL), meta, g.reshape(1, D_MODEL), b.reshape(1, D_MODEL))
        in_specs = [x_spec, const(N_META, D_MODEL), vec, vec]
    else:
        ya, yb, yc, wa, wb, wc, h32, g, b = prologue_args
        body, name = _front_outproj_kernel, "front_outproj_inproj"
        args = (ya, yb, yc, wa, wb, wc, h32, g.reshape(1, D_MODEL), b.reshape(1, D_MODEL))
        in_specs = [row(A_WIDTH), row(B_WIDTH), row(C_WIDTH), const(A_WIDTH, D_MODEL), const(B_WIDTH, D_MODEL),
                    const(C_WIDTH, D_MODEL), row(D_MODEL), vec, vec]
    widths = (D_MODEL, N_ROPE_Q, N_ROPE_K, N_PLAIN, N_SILU, N_GATE, N_GATE)
    dtypes = (jnp.float32,) + (jnp.bfloat16,) * 4 + (jnp.float32,) * 2
    return pl.pallas_call(
        body,
        grid=(M_ROWS // TM,),
        in_specs=in_specs + [const(D_MODEL, N_IN), tab_spec, tab_spec, const(1, C_WIDTH)],
        out_specs=[row(n) for n in widths],
        out_shape=[jax.ShapeDtypeStruct((M_ROWS, n), dt) for n, dt in zip(widths, dtypes)],
        compiler_params=_cparams(("arbitrary",)),
        name=name,
    )(*args, w, cos_t, sin_t, lb)


def _col_chunks(n):
    return [(c, min(MXU_N, n - c)) for c in range(0, n, MXU_N)]


def _rope_cols(acc, cos, sin_signed, first_half):
    partner = jnp.where(first_half, pltpu.roll(acc, LANES - HEAD_DIM // 2, 1), pltpu.roll(acc, HEAD_DIM // 2, 1))
    return acc * cos + partner * sin_signed


def _inproj_body(x, rows, w_ref, cos_ref, sin_ref, lb_ref, outs):
    q_ref, k_ref, p_ref, s_ref, lf_ref, kk_ref = [o.at[rows, :] for o in outs]
    n_rows = rows.stop - rows.start
    cos = cos_ref[rows, :]
    sin = sin_ref[rows, :]
    lane = lax.broadcasted_iota(jnp.int32, (n_rows, LANES), 1)
    first_half = (lane % HEAD_DIM) < (HEAD_DIM // 2)
    row = ((pl.program_id(0) % (LP // TM)) * TM + rows.start
           + lax.broadcasted_iota(jnp.int32, (n_rows, LANES), 0))
    valid = row >= PAD
    pieces = ([("rope_q", q_ref, c) for c in range(0, N_ROPE_Q, LANES)]
              + [("rope_k", k_ref, c) for c in range(0, N_ROPE_K, LANES)]
              + [("plain", p_ref, c) for c in range(0, N_PLAIN, LANES)]
              + [("silu", s_ref, c) for c in range(0, N_SILU, LANES)]
              + [("gate", None, c) for c in range(0, N_GATE, LANES)])
    for c0, cw in _col_chunks(N_IN):
        acc = jnp.dot(x, w_ref[:, c0:c0 + cw], preferred_element_type=jnp.float32)
        for j in range(cw // LANES):
            kind, dst, dc = pieces[c0 // LANES + j]
            a = acc[:, j * LANES:(j + 1) * LANES]
            if kind == "rope_q":
                dst[:, dc:dc + LANES] = (_rope_cols(a, cos, sin, first_half) * Q_SCALE).astype(dst.dtype)
            elif kind == "rope_k":
                dst[:, dc:dc + LANES] = _rope_cols(a, cos, sin, first_half).astype(dst.dtype)
            elif kind == "plain":
                dst[:, dc:dc + LANES] = a.astype(dst.dtype)
            elif kind == "silu":
                dst[:, dc:dc + LANES] = (a * jax.nn.sigmoid(a)).astype(dst.dtype)
            else:
                lb = lb_ref[:, dc % C_WIDTH: dc % C_WIDTH + LANES]
                f = lb + (1.0 - lb) * jax.nn.sigmoid(a)
                lf_ref[:, dc:dc + LANES] = jnp.where(valid, jnp.log2(jnp.maximum(f, F_MIN)), 0.0)
                kk_ref[:, dc:dc + LANES] = jnp.where(valid, (1.0 - lb) * jax.nn.sigmoid(-a), 0.0)


def _mixer_a_kernel(sink_ref, q_ref, k_ref, v_ref, g_ref, o_ref):
    two = 2 * BLOCK
    row = lax.broadcasted_iota(jnp.int32, (two, 4 * BLOCK), 0) % BLOCK
    col = lax.broadcasted_iota(jnp.int32, (two, 4 * BLOCK), 1)
    is_meta = (col < BLOCK) & (col >= PAD)
    top = lax.broadcasted_iota(jnp.int32, (two, 1), 0) < BLOCK
    lane_hi = lax.broadcasted_iota(jnp.int32, (BLOCK, LANES), 1) >= HEAD_DIM
    masks = {}

    def band_start(n):
        return min(max((n - 1) * BLOCK, 0), LP - 3 * BLOCK)

    def scores(n):
        start = band_start(n)
        key = (n * BLOCK - start, start < BLOCK)
        if key not in masks:
            kpos = start + col - BLOCK
            band = (col >= BLOCK) & (jnp.abs(n * BLOCK + row - kpos) <= WINDOW)
            if start < BLOCK:
                band = band & (kpos >= BLOCK)
            masks[key] = is_meta | band
        kcat = jnp.concatenate([k_ref[0, 0:BLOCK, :], k_ref[0, start:start + 3 * BLOCK, :]], axis=0)
        qcols = [q_ref[0, n * BLOCK:(n + 1) * BLOCK, cv * LANES:(cv + 1) * LANES] for cv in range(2)]
        out = []
        for hf in range(2):
            sel = lane_hi if hf else jnp.logical_not(lane_hi)
            q2 = jnp.concatenate([jnp.where(sel, qc, jnp.zeros_like(qc)) for qc in qcols], axis=0)
            s = lax.dot_general(q2, kcat, (((1,), (1,)), ((), ())), preferred_element_type=jnp.float32)
            out.append(jnp.where(masks[key], s, NEG))
        return out

    def softmax_pv(n, s_halves):
        start = band_start(n)
        vcat = jnp.concatenate([v_ref[0, 0:BLOCK, :], v_ref[0, start:start + 3 * BLOCK, :]], axis=0)
        rows = slice(n * BLOCK, (n + 1) * BLOCK)
        outs = []
        for hf, s in enumerate(s_halves):
            sink = jnp.where(top, sink_ref[hf], sink_ref[2 + hf]) * LOG2E
            m = jnp.maximum(jnp.max(s, axis=-1, keepdims=True), sink)
            e = jnp.exp2(s - m)
            den = jnp.sum(e, axis=-1, keepdims=True) + jnp.exp2(sink - m)
            p = (e * (1.0 / den)).astype(jnp.bfloat16)
            outs.append(jnp.dot(p, vcat, preferred_element_type=jnp.float32))
        for cv in range(2):
            o = jnp.where(lane_hi, outs[1][cv * BLOCK:(cv + 1) * BLOCK], outs[0][cv * BLOCK:(cv + 1) * BLOCK])
            y = o * g_ref[0, rows, cv * LANES:(cv + 1) * LANES].astype(jnp.float32)
            o_ref[0, rows, cv * LANES:(cv + 1) * LANES] = y.astype(o_ref.dtype)

    ahead = 2
    pending = {}
    for n in range(NBLK + ahead):
        if n < NBLK:
            pending[n] = scores(n)
        if n >= ahead:
            softmax_pv(n - ahead, pending.pop(n - ahead))


def _mixer_a(sink, q3, k3, v3, g3):
    return pl.pallas_call(
        _mixer_a_kernel,
        grid=(BATCH,),
        in_specs=[
            pl.BlockSpec(memory_space=pltpu.SMEM),
            pl.BlockSpec((1, LP, A_WIDTH), lambda b: (b, 0, 0)),
            pl.BlockSpec((1, LP, LANES), lambda b: (b, 0, 0)),
            pl.BlockSpec((1, LP, LANES), lambda b: (b, 0, C_WIDTH // LANES)),
            pl.BlockSpec((1, LP, A_WIDTH), lambda b: (b, 0, 0)),
        ],
        out_specs=pl.BlockSpec((1, LP, A_WIDTH), lambda b: (b, 0, 0)),
        out_shape=jax.ShapeDtypeStruct((BATCH, LP, A_WIDTH), jnp.bfloat16),
        compiler_params=_cparams(("arbitrary",)),
        name="mixer_a_window_gqa",
    )(sink, q3, k3, v3, g3)


def _mixer_b_kernel(lam_init, lam_ref, *refs):
    hps = B_HEADS_PER_STEP
    q_refs, k_refs, v_refs, g_refs = (refs[i * hps:(i + 1) * hps] for i in range(4))
    w_ref, o_ref, s_a, s_b, a_a, a_b = refs[4 * hps:]
    lam = lam_ref[0]
    lane_hi = lax.broadcasted_iota(jnp.int32, (TQ_B, LANES), 1) >= HEAD_DIM
    key_ok = lax.broadcasted_iota(jnp.int32, (TQ_B, BLOCK), 1) >= PAD
    s_bufs, a_bufs = (s_a, s_b), (a_a, a_b)
    nt = (LP - Q0_B) // TQ_B
    nt_dims = (((1,), (1,)), ((), ()))
    tile_rows = lambda t: slice(Q0_B + t * TQ_B, Q0_B + (t + 1) * TQ_B)
    o_ref[0, 0:Q0_B, :] = jnp.zeros((Q0_B, hps * LANES), o_ref.dtype)
    stages = [(hh, t) for hh in range(hps) for t in range(nt)]

    def scores(i):
        hh, t = stages[i]
        q_ref, k_ref = q_refs[hh], k_refs[hh]
        s_buf = s_bufs[i % 2]
        q = q_ref[0, tile_rows(t), :]
        for c in range(2):
            qm = jnp.where(lane_hi if c else jnp.logical_not(lane_hi), q, jnp.zeros_like(q))
            s0 = lax.dot_general(qm, k_ref[0, 0:BLOCK, :], nt_dims, preferred_element_type=jnp.float32)
            s_buf[c, :, 0:BLOCK] = jnp.where(key_ok, s0, NEG)
            s_buf[c, :, BLOCK:LP] = lax.dot_general(qm, k_ref[0, BLOCK:LP, :], nt_dims,
                                                    preferred_element_type=jnp.float32)

    def softmax_pv(i):
        hh, t = stages[i]
        v_ref, g_ref = v_refs[hh], g_refs[hh]
        cols = slice(hh * LANES, (hh + 1) * LANES)
        w = w_ref[:, cols] * (1.0 - lam_init)
        s_buf, a_buf = s_bufs[i % 2], a_bufs[i % 2]
        inv_den0 = []
        for r in range(TQ_B // SM_ROWS):
            rows = slice(r * SM_ROWS, (r + 1) * SM_ROWS)
            es, dens = [], []
            for c in range(2):
                s = s_buf[c, rows, :]
                m = jnp.max(s, axis=-1, keepdims=True)
                e = jnp.exp2(s - m)
                es.append(e)
                dens.append(jnp.sum(e, axis=-1, keepdims=True))
            a_buf[rows, :] = (es[0] - es[1] * (dens[0] * lam / dens[1])).astype(jnp.bfloat16)
            inv_den0.append(1.0 / dens[0])
        o = jnp.dot(a_buf[...], v_ref[0], preferred_element_type=jnp.float32)
        o = o * jnp.concatenate(inv_den0, axis=0)
        ms = jnp.mean(o * o, axis=-1, keepdims=True)
        y = o * lax.rsqrt(ms + RMS_EPS) * w
        y = y * g_ref[0, tile_rows(t), :].astype(jnp.float32)
        o_ref[0, tile_rows(t), cols] = y.astype(o_ref.dtype)

    for i in range(len(stages) + 1):
        if i < len(stages):
            scores(i)
        if i >= 1:
            softmax_pv(i - 1)


def _mixer_b(lam_init, lam, q3, k3, v3, g3, subln_w):
    qoff = A_WIDTH // LANES
    koff = (A_KV_HEADS * HEAD_DIM) // LANES
    voff = (C_WIDTH + A_KV_HEADS * HEAD_DIM) // LANES
    goff = A_WIDTH // LANES
    hps = B_HEADS_PER_STEP
    head_blocks = lambda off: [pl.BlockSpec((1, LP, LANES), lambda b, hp, hh=hh: (b, 0, off + hp * hps + hh))
                               for hh in range(hps)]
    return pl.pallas_call(
        functools.partial(_mixer_b_kernel, lam_init),
        grid=(BATCH, B_HEADS // hps),
        in_specs=([pl.BlockSpec(memory_space=pltpu.SMEM)]
                  + head_blocks(qoff) + head_blocks(koff) + head_blocks(voff) + head_blocks(goff)
                  + [pl.BlockSpec((1, hps * B_V_DIM), lambda b, hp: (0, hp))]),
        out_specs=pl.BlockSpec((1, LP, hps * LANES), lambda b, hp: (b, 0, hp)),
        out_shape=jax.ShapeDtypeStruct((BATCH, LP, B_WIDTH), jnp.bfloat16),
        scratch_shapes=[pltpu.VMEM((2, TQ_B, LP), jnp.float32), pltpu.VMEM((2, TQ_B, LP), jnp.float32),
                        pltpu.VMEM((TQ_B, LP), jnp.bfloat16), pltpu.VMEM((TQ_B, LP), jnp.bfloat16)],
        compiler_params=_cparams(("arbitrary", "arbitrary")),
        name="mixer_b_diff_attn",
    )(lam, *([q3] * hps), *([k3] * hps), *([v3] * hps), *([g3] * hps), subln_w.reshape(1, B_WIDTH))


def _split2(x):
    hi = x.astype(jnp.bfloat16)
    lo = (x - hi.astype(jnp.float32)).astype(jnp.bfloat16)
    return hi, lo


def _hgrn_anchor(b, m, reverse):
    a_off = m if reverse else m - 1
    if 2 * m >= SUBLANES:
        b3 = b.reshape(BLOCK // (2 * m), 2 * m, C_WIDTH)
        return jnp.broadcast_to(b3[:, a_off:a_off + 1, :], b3.shape).reshape(BLOCK, C_WIDTH)
    b3 = b.reshape(BLOCK // SUBLANES, SUBLANES, C_WIDTH)
    sub = lax.broadcasted_iota(jnp.int32, b3.shape, 1)
    anchor = None
    for blk in range(SUBLANES // (2 * m)):
        r = blk * 2 * m + a_off
        cand = jnp.broadcast_to(b3[:, r:r + 1, :], b3.shape)
        anchor = cand if anchor is None else jnp.where(sub >= blk * 2 * m, cand, anchor)
    return anchor.reshape(BLOCK, C_WIDTH)


def _hgrn_level_exponent(b, m, reverse):
    if m < SUBLANES:
        return -jnp.abs(b - _hgrn_anchor(b, m, reverse))
    a_off = m if reverse else m - 1
    b3 = b.reshape(BLOCK // (2 * m), 2 * m, C_WIDTH)
    anchor = b3[:, a_off:a_off + 1, :]
    left, right = b3[:, 0:m], b3[:, m:2 * m]
    halves = [left - anchor, anchor - right] if reverse else [anchor - left, right - anchor]
    return jnp.concatenate(halves, axis=1).reshape(BLOCK, C_WIDTH)


def _hgrn_chunks(chains, head16, half16, pair_key, same_head):
    nt_dims = (((1,), (1,)), ((), ()))
    f32 = jnp.float32
    for c in chains:
        hi, lo = _split2(c["lf"])
        c["b"] = (jnp.dot(c["tri16"], hi, preferred_element_type=f32)
                  + jnp.dot(c["tri16"], lo, preferred_element_type=f32))
    for c in chains:
        state = c["state"] if c["prev"] is None else chains[c["prev"]]["new_state"]
        edge = 0 if c["reverse"] else BLOCK - 1
        b, b_all = c["b"], c["b"][edge:edge + 1, :]
        q_in = (c["q"] * jnp.exp2(b)).astype(jnp.bfloat16)
        c["o_inter"] = lax.dot_general(q_in, state.astype(jnp.bfloat16), nt_dims, preferred_element_type=f32)
        k_out = (c["kk"] * jnp.exp2(b_all - b)).astype(jnp.bfloat16)
        u_t = lax.dot_general(c["v16"], k_out, (((0,), (0,)), ((), ())), preferred_element_type=f32)
        c["new_state"] = jnp.exp2(b_all) * state + jnp.where(same_head, u_t, 0.0)
        c["p"] = None
    for m in reversed(HGRN_LEVELS):
        for c in chains:
            if "q16" not in c:
                cols = [slice(col * LANES, (col + 1) * LANES) for col in range(C_WIDTH // LANES)]
                c["q16"] = [c["q"][:, cs].astype(jnp.bfloat16) for cs in cols]
                c["k16"] = [[c["kk"][:, cs].astype(jnp.bfloat16) * half16[hf] for hf in range(2)] for cs in cols]
            parts = []
            if m != 0:
                wgt = jnp.exp2(_hgrn_level_exponent(c["b"], m, c["reverse"]))
            for col in range(C_WIDTH // LANES):
                q_col, k_pair = c["q16"][col], c["k16"][col]
                if m != 0:
                    w16 = wgt[:, col * LANES:(col + 1) * LANES].astype(jnp.bfloat16)
                    q_col, k_pair = q_col * w16, [k * w16 for k in k_pair]
                k_two = jnp.concatenate(k_pair, axis=0)
                parts.append(lax.dot_general(q_col, k_two, nt_dims, preferred_element_type=f32))
            pm = jnp.concatenate(parts, axis=1)
            if c["p"] is None:
                c["p"] = pm
            elif m >= SUBLANES:
                shape3 = (BLOCK // (2 * m), 2 * m, C_HEADS * BLOCK)
                p3, pm3, key3 = c["p"].reshape(shape3), pm.reshape(shape3), pair_key.reshape(shape3)
                q_rows = slice(0, m) if c["reverse"] else slice(m, 2 * m)
                new = jnp.where(key3[:, q_rows] < 2 * m, pm3[:, q_rows], p3[:, q_rows])
                halves = [p3[:, 0:m], new] if not c["reverse"] else [new, p3[:, m:2 * m]]
                c["p"] = jnp.concatenate(halves, axis=1).reshape(BLOCK, C_HEADS * BLOCK)
            else:
                c["p"] = jnp.where(pair_key < max(2 * m, 1), pm, c["p"])
    outs = []
    for c in chains:
        p = jnp.where(c["tri_ok"], c["p"], 0.0).astype(jnp.bfloat16)
        v_heads = jnp.concatenate([c["v16"] * head16[h] for h in range(C_HEADS)], axis=0)
        outs.append((jnp.dot(p, v_heads, preferred_element_type=f32) + c["o_inter"], c["new_state"]))
    return outs


def _mixer_c_kernel(q_ref, v_ref, lff_ref, lfb_ref, kkf_ref, kkb_ref, g_ref, w_ref, o_ref, accf_ref, accr_ref):
    lane_head = lax.broadcasted_iota(jnp.int32, (BLOCK, C_WIDTH), 1) // HEAD_DIM
    head16 = [(lane_head == h).astype(jnp.bfloat16) for h in range(C_HEADS)]
    lane_half = lax.broadcasted_iota(jnp.int32, (BLOCK, LANES), 1) // HEAD_DIM
    half16 = [(lane_half == hf).astype(jnp.bfloat16) for hf in range(2)]
    sr = lax.broadcasted_iota(jnp.int32, (C_WIDTH, C_WIDTH), 0) // HEAD_DIM
    sc = lax.broadcasted_iota(jnp.int32, (C_WIDTH, C_WIDTH), 1) // HEAD_DIM
    same_head = sr == sc
    head_ones16 = same_head.astype(jnp.bfloat16)
    t_i = lax.broadcasted_iota(jnp.int32, (BLOCK, C_HEADS * BLOCK), 0)
    s_i = lax.broadcasted_iota(jnp.int32, (BLOCK, C_HEADS * BLOCK), 1) % BLOCK
    pair_key = t_i ^ s_i
    r_i = lax.broadcasted_iota(jnp.int32, (BLOCK, BLOCK), 0)
    c_i = lax.broadcasted_iota(jnp.int32, (BLOCK, BLOCK), 1)
    tri_f, tri_r = (c_i <= r_i).astype(jnp.bfloat16), (c_i >= r_i).astype(jnp.bfloat16)
    ok_f, ok_r = s_i <= t_i, s_i >= t_i

    def run(steps, states):
        chains = []
        for si, step in enumerate(steps):
            for reverse in (False, True):
                n = (NBLK - 1 - step) if reverse else step
                r0 = n * BLOCK if isinstance(n, int) else pl.multiple_of(n * BLOCK, BLOCK)
                rows = pl.ds(r0, BLOCK)
                lf_ref, kk_ref = (lfb_ref, kkb_ref) if reverse else (lff_ref, kkf_ref)
                chains.append(dict(
                    reverse=reverse, rows=rows, q=q_ref[0, rows, :].astype(jnp.float32), v16=v_ref[0, rows, :],
                    kk=kk_ref[0, rows, :], lf=lf_ref[0, rows, :], tri16=tri_r if reverse else tri_f,
                    tri_ok=ok_r if reverse else ok_f, state=states[reverse], prev=None if si == 0 else len(chains) - 2))
        outs = _hgrn_chunks(chains, head16, half16, pair_key, same_head)
        for c, (o, _) in zip(chains, outs):
            (accr_ref if c["reverse"] else accf_ref)[c["rows"], :] = o
        return outs[-2][1], outs[-1][1]

    zero_state = jnp.zeros((C_WIDTH, C_WIDTH), jnp.float32)
    group = HGRN_STEPS_PER_ITER
    states = lax.fori_loop(0, NBLK // group, lambda j, st: run(tuple(group * j + u for u in range(group)), st),
                           (zero_state, zero_state))
    if NBLK % group:
        run(tuple(range(NBLK - NBLK % group, NBLK)), states)

    w = w_ref[...]

    def finish(n, carry):
        r0 = pl.multiple_of(n * BLOCK, BLOCK)
        y = accf_ref[pl.ds(r0, BLOCK), :] + accr_ref[pl.ds(r0, BLOCK), :]
        sq = y * y
        sq_hi = sq.astype(jnp.bfloat16)
        sq_lo = (sq - sq_hi.astype(jnp.float32)).astype(jnp.bfloat16)
        ms = (jnp.dot(sq_hi, head_ones16, preferred_element_type=jnp.float32)
              + jnp.dot(sq_lo, head_ones16, preferred_element_type=jnp.float32)) * (1.0 / HEAD_DIM)
        y = y * lax.rsqrt(ms + RMS_EPS) * w
        y = y * g_ref[0, pl.ds(r0, BLOCK), :].astype(jnp.float32)
        o_ref[0, pl.ds(r0, BLOCK), :] = y.astype(o_ref.dtype)
        return carry

    lax.fori_loop(0, NBLK, finish, 0, unroll=True)


def _mixer_c(q3, v3, lf3, kk3, g3, norm_w):
    blk = lambda j: pl.BlockSpec((1, LP, C_WIDTH), lambda b: (b, 0, j))
    return pl.pallas_call(
        _mixer_c_kernel,
        grid=(BATCH,),
        in_specs=[
            blk(4),
            blk(0),
            blk(0), blk(1),
            blk(0), blk(1),
            blk(3),
            pl.BlockSpec((1, C_WIDTH), lambda b: (0, 0)),
        ],
        out_specs=pl.BlockSpec((1, LP, C_WIDTH), lambda b: (b, 0, 0)),
        out_shape=jax.ShapeDtypeStruct((BATCH, LP, C_WIDTH), jnp.bfloat16),
        scratch_shapes=[pltpu.VMEM((LP, C_WIDTH), jnp.float32), pltpu.VMEM((LP, C_WIDTH), jnp.float32)],
        compiler_params=_cparams(("arbitrary",)),
        name="mixer_c_hgrn2",
    )(q3, v3, lf3, lf3, kk3, kk3, g3, norm_w.reshape(1, C_WIDTH))


def _outproj_last_kernel(ya_ref, yb_ref, yc_ref, wa_ref, wb_ref, wc_ref, h_ref, g_ref, b_ref, o_ref):
    n_rows = o_ref.shape[1]
    halves = (slice(0, n_rows // 2), slice(n_rows // 2, n_rows))
    accs = []
    for rows in halves:
        acc = jnp.dot(ya_ref[rows, :], wa_ref[...], preferred_element_type=jnp.float32)
        acc = acc + jnp.dot(yb_ref[rows, :], wb_ref[...], preferred_element_type=jnp.float32)
        accs.append(acc + jnp.dot(yc_ref[rows, :], wc_ref[...], preferred_element_type=jnp.float32))
    for rows, acc in zip(halves, accs):
        o_ref[0, rows, :] = _layer_norm(ALPHA * h_ref[rows, :] + acc, g_ref[...], b_ref[...])


def _outproj_last(ya, yb, yc, wa, wb, wc, h32, g, b):
    tl = SEQ // 2
    row = lambda n: pl.BlockSpec((pl.Element(tl), pl.Element(n)),
                                 lambda bi, j: (pl.multiple_of(bi * LP + BLOCK + j * tl, BLOCK), 0))
    full = lambda n: pl.BlockSpec((n, D_MODEL), lambda bi, j: (0, 0))
    vec = pl.BlockSpec((1, D_MODEL), lambda bi, j: (0, 0))
    return pl.pallas_call(
        _outproj_last_kernel,
        grid=(BATCH, SEQ // tl),
        in_specs=[row(A_WIDTH), row(B_WIDTH), row(C_WIDTH), full(A_WIDTH), full(B_WIDTH), full(C_WIDTH),
                  row(D_MODEL), vec, vec],
        out_specs=pl.BlockSpec((1, tl, D_MODEL), lambda bi, j: (bi, j, 0)),
        out_shape=jax.ShapeDtypeStruct((BATCH, SEQ, D_MODEL), jnp.float32),
        compiler_params=_cparams(("arbitrary", "arbitrary")),
        name="outproj_last_residual_ln",
    )(ya.reshape(M_ROWS, A_WIDTH), yb.reshape(M_ROWS, B_WIDTH), yc.reshape(M_ROWS, C_WIDTH), wa, wb, wc, h32,
      g.reshape(1, D_MODEL), b.reshape(1, D_MODEL))


def _regroup_w_in(w):
    def heads(lo, order):
        return [w[:, lo + h * HEAD_DIM: lo + (h + 1) * HEAD_DIM] for h in order]

    c = lambda lo, n: w[:, lo:lo + n]
    o_qa, o_ka, o_va, o_ga = 0, 256, 384, 512
    o_qb, o_kb, o_vb, o_gb = 768, 1280, 1792, 2304
    o_qc, o_ic, o_zf, o_zb, o_gc = 2816, 3072, 3328, 3584, 3840
    cat = lambda xs: jnp.concatenate(xs, axis=1).astype(jnp.bfloat16)
    w_q = cat(heads(o_qa, A_HEAD_ORDER) + [c(o_qb, 512)])
    w_k = cat([c(o_ka, 128), c(o_kb, 512)])
    w_p = cat([c(o_ic, 256), c(o_va, 128), c(o_vb, 512)])
    w_s = cat(heads(o_ga, A_HEAD_ORDER) + [c(o_gb, 512), c(o_gc, 256), c(o_qc, 256)])
    w_g = cat([c(o_zf, 256), c(o_zb, 256)])
    return jnp.concatenate([w_q, w_k, w_p, w_s, w_g], axis=1)


def _rope_tables():
    pos = np.arange(LP, dtype=np.float64) - PAD
    inv = ROPE_THETA ** (-np.arange(0, HEAD_DIM, 2, dtype=np.float64) / HEAD_DIM)
    ang = pos[:, None] * inv[None, :]
    cos, sin = np.cos(ang), np.sin(ang)
    cos_t = np.concatenate([cos, cos, cos, cos], axis=1)
    sin_t = np.concatenate([-sin, sin, -sin, sin], axis=1)
    return jnp.asarray(cos_t, jnp.float32), jnp.asarray(sin_t, jnp.float32)


def kernel(x, meta, emb_ln_g, emb_ln_b, w_in, w_out, a_sink, b_lam, b_subln_w, c_lb_logits, c_norm_w, ln_g, ln_b):
    cos_t, sin_t = _rope_tables()
    p_lb = jax.nn.softmax(c_lb_logits.astype(jnp.float32), axis=0)
    lb_all = jnp.cumsum(p_lb, axis=0) - p_lb[0:1]

    front_args = (x, meta, emb_ln_g, emb_ln_b)
    for l in range(DEPTH):
        lam_init = 0.8 - 0.6 * math.exp(-0.3 * l)
        lp32 = b_lam[l].astype(jnp.float32)
        lam = jnp.exp(jnp.sum(lp32[0] * lp32[1])) - jnp.exp(jnp.sum(lp32[2] * lp32[3])) + lam_init
        wo = w_out[l]
        wa = jnp.concatenate([wo[h * HEAD_DIM:(h + 1) * HEAD_DIM] for h in A_HEAD_ORDER], axis=0).astype(jnp.bfloat16)
        wb = wo[A_WIDTH:A_WIDTH + B_WIDTH].astype(jnp.bfloat16)
        wc = wo[A_WIDTH + B_WIDTH:].astype(jnp.bfloat16)
        sink = jnp.stack([a_sink[l, h] for h in A_HEAD_ORDER]).astype(jnp.float32)

        h32, *proj = _front(front_args, _regroup_w_in(w_in[l]), cos_t, sin_t, lb_all[l].reshape(1, C_WIDTH))
        q_all, k_all, p_all, s_all, lf_all, kk_all = [t.reshape(BATCH, LP, t.shape[-1]) for t in proj]

        ya = _mixer_a(sink, q_all, k_all, p_all, s_all)
        yb = _mixer_b(lam_init, lam.reshape(1), q_all, k_all, p_all, s_all, b_subln_w[l])
        yc = _mixer_c(s_all, p_all, lf_all, kk_all, s_all, c_norm_w[l])

        if l == DEPTH - 1:
            return _outproj_last(ya, yb, yc, wa, wb, wc, h32, ln_g[l], ln_b[l])
        front_args = (ya.reshape(M_ROWS, A_WIDTH), yb.reshape(M_ROWS, B_WIDTH), yc.reshape(M_ROWS, C_WIDTH),
                      wa, wb, wc, h32, ln_g[l], ln_b[l])
```

```python
import functools
import math

import jax
import jax.numpy as jnp
import numpy as np
from jax import lax
from jax.experimental import pallas as pl
from jax.experimental.pallas import tpu as pltpu

D_MODEL = 1024
BATCH = 8
SEQ = 2048
DEPTH = 2
N_META = 16
BLOCK = 128
PAD = BLOCK - N_META
WINDOW = 128
ROPE_THETA = 10000.0
HEAD_DIM = 64
A_HEADS = 4
A_KV_HEADS = 2
A_WIDTH = A_HEADS * HEAD_DIM
B_HEADS = 4
B_V_DIM = 2 * HEAD_DIM
B_WIDTH = B_HEADS * B_V_DIM
C_HEADS = 4
C_WIDTH = C_HEADS * HEAD_DIM
D_MIX = A_WIDTH + B_WIDTH + C_WIDTH
ALPHA = (2 * DEPTH) ** 0.25
LN_EPS = 1e-5
RMS_EPS = 1e-6
NEG = -1e30
F_MIN = 1e-30

LP = SEQ + N_META + PAD
NBLK = LP // BLOCK
M_ROWS = BATCH * LP
TM = LP // 4
LANES = 128
SUBLANES = 8
MXU_N = 256
LOG2E = math.log2(math.e)
Q_SCALE = HEAD_DIM ** -0.5 * LOG2E
Q0_B = 96
TQ_B = (LP - Q0_B) // 5
B_HEADS_PER_STEP = 1
SM_ROWS = 32
VMEM_LIMIT = 48 * 1024 * 1024

N_ROPE_Q = A_WIDTH + 2 * B_HEADS * HEAD_DIM
N_ROPE_K = A_KV_HEADS * HEAD_DIM + 2 * B_HEADS * HEAD_DIM
N_PLAIN = C_WIDTH + A_KV_HEADS * HEAD_DIM + B_WIDTH
N_SILU = A_WIDTH + B_WIDTH + C_WIDTH + C_WIDTH
N_GATE = 2 * C_WIDTH
N_IN = N_ROPE_Q + N_ROPE_K + N_PLAIN + N_SILU + N_GATE
A_HEAD_ORDER = (0, 2, 1, 3)

HGRN_LEVELS = (0, 1, 2, 4, 8, 16, 32, 64)
HGRN_STEPS_PER_ITER = 8


def _cparams(sem):
    return pltpu.CompilerParams(dimension_semantics=sem, vmem_limit_bytes=VMEM_LIMIT)


def _layer_norm(x, g, b):
    mu = jnp.mean(x, axis=-1, keepdims=True)
    xc = x - mu
    var = jnp.mean(xc * xc, axis=-1, keepdims=True)
    return xc * lax.rsqrt(var + LN_EPS) * g + b


def _front_embed_kernel(x_ref, meta_ref, g_ref, b_ref, w_ref, cos_ref, sin_ref, lb_ref,
                        h32_ref, q_ref, k_ref, p_ref, s_ref, lf_ref, kk_ref):
    first = pl.program_id(0) % (LP // TM) == 0
    g, b = g_ref[...], b_ref[...]
    half = TM // 2
    pad_meta = jnp.concatenate([jnp.zeros((PAD, D_MODEL), jnp.float32), meta_ref[...]], axis=0)
    top = _layer_norm(jnp.where(first, pad_meta, x_ref[0:BLOCK, :]), g, b)
    pad_row = lax.broadcasted_iota(jnp.int32, (BLOCK, 1), 0) < PAD
    top = jnp.where(jnp.logical_and(first, pad_row), 0.0, top)
    off = pl.multiple_of(jnp.where(first, 0, BLOCK), SUBLANES)
    outs = (q_ref, k_ref, p_ref, s_ref, lf_ref, kk_ref)
    h_a = jnp.concatenate([top, _layer_norm(x_ref[pl.ds(off, half - BLOCK), :], g, b)], axis=0)
    h32_ref[0:half, :] = h_a
    _inproj_body(h_a.astype(jnp.bfloat16), slice(0, half), w_ref, cos_ref, sin_ref, lb_ref, outs)
    h_b = _layer_norm(x_ref[pl.ds(off + (half - BLOCK), TM - half), :], g, b)
    h32_ref[half:TM, :] = h_b
    _inproj_body(h_b.astype(jnp.bfloat16), slice(half, TM), w_ref, cos_ref, sin_ref, lb_ref, outs)


def _front_outproj_kernel(ya_ref, yb_ref, yc_ref, wa_ref, wb_ref, wc_ref, h_ref, g_ref, b_ref,
                          w_ref, cos_ref, sin_ref, lb_ref, h32_ref, q_ref, k_ref, p_ref, s_ref, lf_ref, kk_ref):
    halves = (slice(0, TM // 2), slice(TM // 2, TM))
    accs = []
    for rows in halves:
        acc = jnp.dot(ya_ref[rows, :], wa_ref[...], preferred_element_type=jnp.float32)
        acc = acc + jnp.dot(yb_ref[rows, :], wb_ref[...], preferred_element_type=jnp.float32)
        accs.append(acc + jnp.dot(yc_ref[rows, :], wc_ref[...], preferred_element_type=jnp.float32))
    for rows, acc in zip(halves, accs):
        h = _layer_norm(ALPHA * h_ref[rows, :] + acc, g_ref[...], b_ref[...])
        h32_ref[rows, :] = h
        _inproj_body(h.astype(jnp.bfloat16), rows, w_ref, cos_ref, sin_ref, lb_ref,
                     (q_ref, k_ref, p_ref, s_ref, lf_ref, kk_ref))


def _front(prologue_args, w, cos_t, sin_t, lb):
    row = lambda n: pl.BlockSpec((TM, n), lambda i: (i, 0))
    const = lambda r, n: pl.BlockSpec((r, n), lambda i: (0, 0))
    vec = const(1, D_MODEL)
    tab_spec = pl.BlockSpec((TM, LANES), lambda i: (i % (LP // TM), 0))
    tiles_per_b = LP // TM
    if len(prologue_args) == 4:
        x, meta, g, b = prologue_args
        x_spec = pl.BlockSpec(
            (pl.Element(TM), pl.Element(D_MODEL)),
            lambda i: (pl.multiple_of((i // tiles_per_b) * SEQ + jnp.maximum((i % tiles_per_b) * TM - BLOCK, 0),
                                      SUBLANES), 0))
        body, name = _front_embed_kernel, "front_embed_inproj"
        args = (x.reshape(BATCH * SEQ, D_MODEL), meta, g.reshape(1, D_MODEL), b.reshape(1, D_MODEL))
        in_specs = [x_spec, const(N_META, D_MODEL), vec, vec]
    else:
        ya, yb, yc, wa, wb, wc, h32, g, b = prologue_args
        body, name = _front_outproj_kernel, "front_outproj_inproj"
        args = (ya, yb, yc, wa, wb, wc, h32, g.reshape(1, D_MODEL), b.reshape(1, D_MODEL))
        in_specs = [row(A_WIDTH), row(B_WIDTH), row(C_WIDTH), const(A_WIDTH, D_MODEL), const(B_WIDTH, D_MODEL),
                    const(C_WIDTH, D_MODEL), row(D_MODEL), vec, vec]
    widths = (D_MODEL, N_ROPE_Q, N_ROPE_K, N_PLAIN, N_SILU, N_GATE, N_GATE)
    dtypes = (jnp.float32,) + (jnp.bfloat16,) * 4 + (jnp.float32,) * 2
    return pl.pallas_call(
        body,
        grid=(M_ROWS // TM,),
        in_specs=in_specs + [const(D_MODEL, N_IN), tab_spec, tab_spec, const(1, C_WIDTH)],
        out_specs=[row(n) for n in widths],
        out_shape=[jax.ShapeDtypeStruct((M_ROWS, n), dt) for n, dt in zip(widths, dtypes)],
        compiler_params=_cparams(("arbitrary",)),
        name=name,
    )(*args, w, cos_t, sin_t, lb)


def _col_chunks(n):
    return [(c, min(MXU_N, n - c)) for c in range(0, n, MXU_N)]


def _rope_cols(acc, cos, sin_signed, first_half):
    partner = jnp.where(first_half, pltpu.roll(acc, LANES - HEAD_DIM // 2, 1), pltpu.roll(acc, HEAD_DIM // 2, 1))
    return acc * cos + partner * sin_signed


def _inproj_body(x, rows, w_ref, cos_ref, sin_ref, lb_ref, outs):
    q_ref, k_ref, p_ref, s_ref, lf_ref, kk_ref = [o.at[rows, :] for o in outs]
    n_rows = rows.stop - rows.start
    cos = cos_ref[rows, :]
    sin = sin_ref[rows, :]
    lane = lax.broadcasted_iota(jnp.int32, (n_rows, LANES), 1)
    first_half = (lane % HEAD_DIM) < (HEAD_DIM // 2)
    row = ((pl.program_id(0) % (LP // TM)) * TM + rows.start
           + lax.broadcasted_iota(jnp.int32, (n_rows, LANES), 0))
    valid = row >= PAD
    pieces = ([("rope_q", q_ref, c) for c in range(0, N_ROPE_Q, LANES)]
              + [("rope_k", k_ref, c) for c in range(0, N_ROPE_K, LANES)]
              + [("plain", p_ref, c) for c in range(0, N_PLAIN, LANES)]
              + [("silu", s_ref, c) for c in range(0, N_SILU, LANES)]
              + [("gate", None, c) for c in range(0, N_GATE, LANES)])
    for c0, cw in _col_chunks(N_IN):
        acc = jnp.dot(x, w_ref[:, c0:c0 + cw], preferred_element_type=jnp.float32)
        for j in range(cw // LANES):
            kind, dst, dc = pieces[c0 // LANES + j]
            a = acc[:, j * LANES:(j + 1) * LANES]
            if kind == "rope_q":
                dst[:, dc:dc + LANES] = (_rope_cols(a, cos, sin, first_half) * Q_SCALE).astype(dst.dtype)
            elif kind == "rope_k":
                dst[:, dc:dc + LANES] = _rope_cols(a, cos, sin, first_half).astype(dst.dtype)
            elif kind == "plain":
                dst[:, dc:dc + LANES] = a.astype(dst.dtype)
            elif kind == "silu":
                dst[:, dc:dc + LANES] = (a * jax.nn.sigmoid(a)).astype(dst.dtype)
            else:
                lb = lb_ref[:, dc % C_WIDTH: dc % C_WIDTH + LANES]
                f = lb + (1.0 - lb) * jax.nn.sigmoid(a)
                lf_ref[:, dc:dc + LANES] = jnp.where(valid, jnp.log2(jnp.maximum(f, F_MIN)), 0.0)
                kk_ref[:, dc:dc + LANES] = jnp.where(valid, (1.0 - lb) * jax.nn.sigmoid(-a), 0.0)


def _mixer_a_kernel(sink_ref, q_ref, k_ref, v_ref, g_ref, o_ref):
    two = 2 * BLOCK
    row = lax.broadcasted_iota(jnp.int32, (two, 4 * BLOCK), 0) % BLOCK
    col = lax.broadcasted_iota(jnp.int32, (two, 4 * BLOCK), 1)
    is_meta = (col < BLOCK) & (col >= PAD)
    top = lax.broadcasted_iota(jnp.int32, (two, 1), 0) < BLOCK
    lane_hi = lax.broadcasted_iota(jnp.int32, (BLOCK, LANES), 1) >= HEAD_DIM
    masks = {}

    def band_start(n):
        return min(max((n - 1) * BLOCK, 0), LP - 3 * BLOCK)

    def scores(n):
        start = band_start(n)
        key = (n * BLOCK - start, start < BLOCK)
        if key not in masks:
            kpos = start + col - BLOCK
            band = (col >= BLOCK) & (jnp.abs(n * BLOCK + row - kpos) <= WINDOW)
            if start < BLOCK:
                band = band & (kpos >= BLOCK)
            masks[key] = is_meta | band
        kcat = jnp.concatenate([k_ref[0, 0:BLOCK, :], k_ref[0, start:start + 3 * BLOCK, :]], axis=0)
        qcols = [q_ref[0, n * BLOCK:(n + 1) * BLOCK, cv * LANES:(cv + 1) * LANES] for cv in range(2)]
        out = []
        for hf in range(2):
            sel = lane_hi if hf else jnp.logical_not(lane_hi)
            q2 = jnp.concatenate([jnp.where(sel, qc, jnp.zeros_like(qc)) for qc in qcols], axis=0)
            s = lax.dot_general(q2, kcat, (((1,), (1,)), ((), ())), preferred_element_type=jnp.float32)
            out.append(jnp.where(masks[key], s, NEG))
        return out

    def softmax_pv(n, s_halves):
        start = band_start(n)
        vcat = jnp.concatenate([v_ref[0, 0:BLOCK, :], v_ref[0, start:start + 3 * BLOCK, :]], axis=0)
        rows = slice(n * BLOCK, (n + 1) * BLOCK)
        outs = []
        for hf, s in enumerate(s_halves):
            sink = jnp.where(top, sink_ref[hf], sink_ref[2 + hf]) * LOG2E
            m = jnp.maximum(jnp.max(s, axis=-1, keepdims=True), sink)
            e = jnp.exp2(s - m)
            den = jnp.sum(e, axis=-1, keepdims=True) + jnp.exp2(sink - m)
            p = (e * (1.0 / den)).astype(jnp.bfloat16)
            outs.append(jnp.dot(p, vcat, preferred_element_type=jnp.float32))
        for cv in range(2):
            o = jnp.where(lane_hi, outs[1][cv * BLOCK:(cv + 1) * BLOCK], outs[0][cv * BLOCK:(cv + 1) * BLOCK])
            y = o * g_ref[0, rows, cv * LANES:(cv + 1) * LANES].astype(jnp.float32)
            o_ref[0, rows, cv * LANES:(cv + 1) * LANES] = y.astype(o_ref.dtype)

    ahead = 1
    pending = {}
    for n in range(NBLK + ahead):
        if n < NBLK:
            pending[n] = scores(n)
        if n >= ahead:
            softmax_pv(n - ahead, pending.pop(n - ahead))


def _mixer_a(sink, q3, k3, v3, g3):
    return pl.pallas_call(
        _mixer_a_kernel,
        grid=(BATCH,),
        in_specs=[
            pl.BlockSpec(memory_space=pltpu.SMEM),
            pl.BlockSpec((1, LP, A_WIDTH), lambda b: (b, 0, 0)),
            pl.BlockSpec((1, LP, LANES), lambda b: (b, 0, 0)),
            pl.BlockSpec((1, LP, LANES), lambda b: (b, 0, C_WIDTH // LANES)),
            pl.BlockSpec((1, LP, A_WIDTH), lambda b: (b, 0, 0)),
        ],
        out_specs=pl.BlockSpec((1, LP, A_WIDTH), lambda b: (b, 0, 0)),
        out_shape=jax.ShapeDtypeStruct((BATCH, LP, A_WIDTH), jnp.bfloat16),
        compiler_params=_cparams(("arbitrary",)),
        name="mixer_a_window_gqa",
    )(sink, q3, k3, v3, g3)


def _mixer_b_kernel(lam_init, lam_ref, *refs):
    hps = B_HEADS_PER_STEP
    q_refs, k_refs, v_refs, g_refs = (refs[i * hps:(i + 1) * hps] for i in range(4))
    w_ref, o_ref, s_a, s_b, a_a, a_b = refs[4 * hps:]
    lam = lam_ref[0]
    lane_hi = lax.broadcasted_iota(jnp.int32, (TQ_B, LANES), 1) >= HEAD_DIM
    key_ok = lax.broadcasted_iota(jnp.int32, (TQ_B, BLOCK), 1) >= PAD
    s_bufs, a_bufs = (s_a, s_b), (a_a, a_b)
    nt = (LP - Q0_B) // TQ_B
    nt_dims = (((1,), (1,)), ((), ()))
    tile_rows = lambda t: slice(Q0_B + t * TQ_B, Q0_B + (t + 1) * TQ_B)
    o_ref[0, 0:Q0_B, :] = jnp.zeros((Q0_B, hps * LANES), o_ref.dtype)
    stages = [(hh, t) for hh in range(hps) for t in range(nt)]

    def scores(i):
        hh, t = stages[i]
        q_ref, k_ref = q_refs[hh], k_refs[hh]
        s_buf = s_bufs[i % 2]
        q = q_ref[0, tile_rows(t), :]
        for c in range(2):
            qm = jnp.where(lane_hi if c else jnp.logical_not(lane_hi), q, jnp.zeros_like(q))
            s0 = lax.dot_general(qm, k_ref[0, 0:BLOCK, :], nt_dims, preferred_element_type=jnp.float32)
            s_buf[c, :, 0:BLOCK] = jnp.where(key_ok, s0, NEG)
            s_buf[c, :, BLOCK:LP] = lax.dot_general(qm, k_ref[0, BLOCK:LP, :], nt_dims,
                                                    preferred_element_type=jnp.float32)

    def softmax_pv(i):
        hh, t = stages[i]
        v_ref, g_ref = v_refs[hh], g_refs[hh]
        cols = slice(hh * LANES, (hh + 1) * LANES)
        w = w_ref[:, cols] * (1.0 - lam_init)
        s_buf, a_buf = s_bufs[i % 2], a_bufs[i % 2]
        inv_den0 = []
        for r in range(TQ_B // SM_ROWS):
            rows = slice(r * SM_ROWS, (r + 1) * SM_ROWS)
            es, dens = [], []
            for c in range(2):
                s = s_buf[c, rows, :]
                m = jnp.max(s, axis=-1, keepdims=True)
                e = jnp.exp2(s - m)
                es.append(e)
                dens.append(jnp.sum(e, axis=-1, keepdims=True))
            a_buf[rows, :] = (es[0] - es[1] * (dens[0] * lam / dens[1])).astype(jnp.bfloat16)
            inv_den0.append(1.0 / dens[0])
        o = jnp.dot(a_buf[...], v_ref[0], preferred_element_type=jnp.float32)
        o = o * jnp.concatenate(inv_den0, axis=0)
        ms = jnp.mean(o * o, axis=-1, keepdims=True)
        y = o * lax.rsqrt(ms + RMS_EPS) * w
        y = y * g_ref[0, tile_rows(t), :].astype(jnp.float32)
        o_ref[0, tile_rows(t), cols] = y.astype(o_ref.dtype)

    for i in range(len(stages) + 1):
        if i < len(stages):
            scores(i)
        if i >= 1:
            softmax_pv(i - 1)


def _mixer_b(lam_init, lam, q3, k3, v3, g3, subln_w):
    qoff = A_WIDTH // LANES
    koff = (A_KV_HEADS * HEAD_DIM) // LANES
    voff = (C_WIDTH + A_KV_HEADS * HEAD_DIM) // LANES
    goff = A_WIDTH // LANES
    hps = B_HEADS_PER_STEP
    head_blocks = lambda off: [pl.BlockSpec((1, LP, LANES), lambda b, hp, hh=hh: (b, 0, off + hp * hps + hh))
                               for hh in range(hps)]
    return pl.pallas_call(
        functools.partial(_mixer_b_kernel, lam_init),
        grid=(BATCH, B_HEADS // hps),
        in_specs=([pl.BlockSpec(memory_space=pltpu.SMEM)]
                  + head_blocks(qoff) + head_blocks(koff) + head_blocks(voff) + head_blocks(goff)
                  + [pl.BlockSpec((1, hps * B_V_DIM), lambda b, hp: (0, hp))]),
        out_specs=pl.BlockSpec((1, LP, hps * LANES), lambda b, hp: (b, 0, hp)),
        out_shape=jax.ShapeDtypeStruct((BATCH, LP, B_WIDTH), jnp.bfloat16),
        scratch_shapes=[pltpu.VMEM((2, TQ_B, LP), jnp.float32), pltpu.VMEM((2, TQ_B, LP), jnp.float32),
                        pltpu.VMEM((TQ_B, LP), jnp.bfloat16), pltpu.VMEM((TQ_B, LP), jnp.bfloat16)],
        compiler_params=_cparams(("arbitrary", "arbitrary")),
        name="mixer_b_diff_attn",
    )(lam, *([q3] * hps), *([k3] * hps), *([v3] * hps), *([g3] * hps), subln_w.reshape(1, B_WIDTH))


def _split2(x):
    hi = x.astype(jnp.bfloat16)
    lo = (x - hi.astype(jnp.float32)).astype(jnp.bfloat16)
    return hi, lo


def _hgrn_anchor(b, m, reverse):
    a_off = m if reverse else m - 1
    if 2 * m >= SUBLANES:
        b3 = b.reshape(BLOCK // (2 * m), 2 * m, C_WIDTH)
        return jnp.broadcast_to(b3[:, a_off:a_off + 1, :], b3.shape).reshape(BLOCK, C_WIDTH)
    b3 = b.reshape(BLOCK // SUBLANES, SUBLANES, C_WIDTH)
    sub = lax.broadcasted_iota(jnp.int32, b3.shape, 1)
    anchor = None
    for blk in range(SUBLANES // (2 * m)):
        r = blk * 2 * m + a_off
        cand = jnp.broadcast_to(b3[:, r:r + 1, :], b3.shape)
        anchor = cand if anchor is None else jnp.where(sub >= blk * 2 * m, cand, anchor)
    return anchor.reshape(BLOCK, C_WIDTH)


def _hgrn_level_exponent(b, m, reverse):
    if m < SUBLANES:
        return -jnp.abs(b - _hgrn_anchor(b, m, reverse))
    a_off = m if reverse else m - 1
    b3 = b.reshape(BLOCK // (2 * m), 2 * m, C_WIDTH)
    anchor = b3[:, a_off:a_off + 1, :]
    left, right = b3[:, 0:m], b3[:, m:2 * m]
    halves = [left - anchor, anchor - right] if reverse else [anchor - left, right - anchor]
    return jnp.concatenate(halves, axis=1).reshape(BLOCK, C_WIDTH)


def _hgrn_chunks(chains, head16, half16, pair_key, same_head):
    nt_dims = (((1,), (1,)), ((), ()))
    f32 = jnp.float32
    for c in chains:
        hi, lo = _split2(c["lf"])
        c["b"] = (jnp.dot(c["tri16"], hi, preferred_element_type=f32)
                  + jnp.dot(c["tri16"], lo, preferred_element_type=f32))
    for c in chains:
        state = c["state"] if c["prev"] is None else chains[c["prev"]]["new_state"]
        edge = 0 if c["reverse"] else BLOCK - 1
        b, b_all = c["b"], c["b"][edge:edge + 1, :]
        q_in = (c["q"] * jnp.exp2(b)).astype(jnp.bfloat16)
        c["o_inter"] = lax.dot_general(q_in, state.astype(jnp.bfloat16), nt_dims, preferred_element_type=f32)
        k_out = (c["kk"] * jnp.exp2(b_all - b)).astype(jnp.bfloat16)
        u_t = lax.dot_general(c["v16"], k_out, (((0,), (0,)), ((), ())), preferred_element_type=f32)
        c["new_state"] = jnp.exp2(b_all) * state + jnp.where(same_head, u_t, 0.0)
        c["p"] = None
    for m in reversed(HGRN_LEVELS):
        for c in chains:
            if "q16" not in c:
                cols = [slice(col * LANES, (col + 1) * LANES) for col in range(C_WIDTH // LANES)]
                c["q16"] = [c["q"][:, cs].astype(jnp.bfloat16) for cs in cols]
                c["k16"] = [[c["kk"][:, cs].astype(jnp.bfloat16) * half16[hf] for hf in range(2)] for cs in cols]
            parts = []
            if m != 0:
                wgt = jnp.exp2(_hgrn_level_exponent(c["b"], m, c["reverse"]))
            for col in range(C_WIDTH // LANES):
                q_col, k_pair = c["q16"][col], c["k16"][col]
                if m != 0:
                    w16 = wgt[:, col * LANES:(col + 1) * LANES].astype(jnp.bfloat16)
                    q_col, k_pair = q_col * w16, [k * w16 for k in k_pair]
                k_two = jnp.concatenate(k_pair, axis=0)
                parts.append(lax.dot_general(q_col, k_two, nt_dims, preferred_element_type=f32))
            pm = jnp.concatenate(parts, axis=1)
            if c["p"] is None:
                c["p"] = pm
            elif m >= SUBLANES:
                shape3 = (BLOCK // (2 * m), 2 * m, C_HEADS * BLOCK)
                p3, pm3, key3 = c["p"].reshape(shape3), pm.reshape(shape3), pair_key.reshape(shape3)
                q_rows = slice(0, m) if c["reverse"] else slice(m, 2 * m)
                new = jnp.where(key3[:, q_rows] < 2 * m, pm3[:, q_rows], p3[:, q_rows])
                halves = [p3[:, 0:m], new] if not c["reverse"] else [new, p3[:, m:2 * m]]
                c["p"] = jnp.concatenate(halves, axis=1).reshape(BLOCK, C_HEADS * BLOCK)
            else:
                c["p"] = jnp.where(pair_key < max(2 * m, 1), pm, c["p"])
    outs = []
    for c in chains:
        p = jnp.where(c["tri_ok"], c["p"], 0.0).astype(jnp.bfloat16)
        v_heads = jnp.concatenate([c["v16"] * head16[h] for h in range(C_HEADS)], axis=0)
        outs.append((jnp.dot(p, v_heads, preferred_element_type=f32) + c["o_inter"], c["new_state"]))
    return outs


def _mixer_c_kernel(q_ref, v_ref, lff_ref, lfb_ref, kkf_ref, kkb_ref, g_ref, w_ref, o_ref, accf_ref, accr_ref):
    lane_head = lax.broadcasted_iota(jnp.int32, (BLOCK, C_WIDTH), 1) // HEAD_DIM
    head16 = [(lane_head == h).astype(jnp.bfloat16) for h in range(C_HEADS)]
    lane_half = lax.broadcasted_iota(jnp.int32, (BLOCK, LANES), 1) // HEAD_DIM
    half16 = [(lane_half == hf).astype(jnp.bfloat16) for hf in range(2)]
    sr = lax.broadcasted_iota(jnp.int32, (C_WIDTH, C_WIDTH), 0) // HEAD_DIM
    sc = lax.broadcasted_iota(jnp.int32, (C_WIDTH, C_WIDTH), 1) // HEAD_DIM
    same_head = sr == sc
    head_ones16 = same_head.astype(jnp.bfloat16)
    t_i = lax.broadcasted_iota(jnp.int32, (BLOCK, C_HEADS * BLOCK), 0)
    s_i = lax.broadcasted_iota(jnp.int32, (BLOCK, C_HEADS * BLOCK), 1) % BLOCK
    pair_key = t_i ^ s_i
    r_i = lax.broadcasted_iota(jnp.int32, (BLOCK, BLOCK), 0)
    c_i = lax.broadcasted_iota(jnp.int32, (BLOCK, BLOCK), 1)
    tri_f, tri_r = (c_i <= r_i).astype(jnp.bfloat16), (c_i >= r_i).astype(jnp.bfloat16)
    ok_f, ok_r = s_i <= t_i, s_i >= t_i

    def run(steps, states):
        chains = []
        for si, step in enumerate(steps):
            for reverse in (False, True):
                n = (NBLK - 1 - step) if reverse else step
                r0 = n * BLOCK if isinstance(n, int) else pl.multiple_of(n * BLOCK, BLOCK)
                rows = pl.ds(r0, BLOCK)
                lf_ref, kk_ref = (lfb_ref, kkb_ref) if reverse else (lff_ref, kkf_ref)
                chains.append(dict(
                    reverse=reverse, rows=rows, q=q_ref[0, rows, :].astype(jnp.float32), v16=v_ref[0, rows, :],
                    kk=kk_ref[0, rows, :], lf=lf_ref[0, rows, :], tri16=tri_r if reverse else tri_f,
                    tri_ok=ok_r if reverse else ok_f, state=states[reverse], prev=None if si == 0 else len(chains) - 2))
        outs = _hgrn_chunks(chains, head16, half16, pair_key, same_head)
        for c, (o, _) in zip(chains, outs):
            (accr_ref if c["reverse"] else accf_ref)[c["rows"], :] = o
        return outs[-2][1], outs[-1][1]

    zero_state = jnp.zeros((C_WIDTH, C_WIDTH), jnp.float32)
    group = HGRN_STEPS_PER_ITER
    states = lax.fori_loop(0, NBLK // group, lambda j, st: run(tuple(group * j + u for u in range(group)), st),
                           (zero_state, zero_state))
    if NBLK % group:
        run(tuple(range(NBLK - NBLK % group, NBLK)), states)

    w = w_ref[...]

    def finish(n, carry):
        r0 = pl.multiple_of(n * BLOCK, BLOCK)
        y = accf_ref[pl.ds(r0, BLOCK), :] + accr_ref[pl.ds(r0, BLOCK), :]
        sq = y * y
        sq_hi = sq.astype(jnp.bfloat16)
        sq_lo = (sq - sq_hi.astype(jnp.float32)).astype(jnp.bfloat16)
        ms = (jnp.dot(sq_hi, head_ones16, preferred_element_type=jnp.float32)
              + jnp.dot(sq_lo, head_ones16, preferred_element_type=jnp.float32)) * (1.0 / HEAD_DIM)
        y = y * lax.rsqrt(ms + RMS_EPS) * w
        y = y * g_ref[0, pl.ds(r0, BLOCK), :].astype(jnp.float32)
        o_ref[0, pl.ds(r0, BLOCK), :] = y.astype(o_ref.dtype)
        return carry

    lax.fori_loop(0, NBLK, finish, 0, unroll=True)


def _mixer_c(q3, v3, lf3, kk3, g3, norm_w):
    blk = lambda j: pl.BlockSpec((1, LP, C_WIDTH), lambda b: (b, 0, j))
    return pl.pallas_call(
        _mixer_c_kernel,
        grid=(BATCH,),
        in_specs=[
            blk(4),
            blk(0),
            blk(0), blk(1),
            blk(0), blk(1),
            blk(3),
            pl.BlockSpec((1, C_WIDTH), lambda b: (0, 0)),
        ],
        out_specs=pl.BlockSpec((1, LP, C_WIDTH), lambda b: (b, 0, 0)),
        out_shape=jax.ShapeDtypeStruct((BATCH, LP, C_WIDTH), jnp.bfloat16),
        scratch_shapes=[pltpu.VMEM((LP, C_WIDTH), jnp.float32), pltpu.VMEM((LP, C_WIDTH), jnp.float32)],
        compiler_params=_cparams(("arbitrary",)),
        name="mixer_c_hgrn2",
    )(q3, v3, lf3, lf3, kk3, kk3, g3, norm_w.reshape(1, C_WIDTH))


def _outproj_last_kernel(ya_ref, yb_ref, yc_ref, wa_ref, wb_ref, wc_ref, h_ref, g_ref, b_ref, o_ref):
    n_rows = o_ref.shape[1]
    halves = (slice(0, n_rows // 2), slice(n_rows // 2, n_rows))
    accs = []
    for rows in halves:
        acc = jnp.dot(ya_ref[rows, :], wa_ref[...], preferred_element_type=jnp.float32)
        acc = acc + jnp.dot(yb_ref[rows, :], wb_ref[...], preferred_element_type=jnp.float32)
        accs.append(acc + jnp.dot(yc_ref[rows, :], wc_ref[...], preferred_element_type=jnp.float32))
    for rows, acc in zip(halves, accs):
        o_ref[0, rows, :] = _layer_norm(ALPHA * h_ref[rows, :] + acc, g_ref[...], b_ref[...])


def _outproj_last(ya, yb, yc, wa, wb, wc, h32, g, b):
    tl = SEQ // 2
    row = lambda n: pl.BlockSpec((pl.Element(tl), pl.Element(n)),
                                 lambda bi, j: (pl.multiple_of(bi * LP + BLOCK + j * tl, BLOCK), 0))
    full = lambda n: pl.BlockSpec((n, D_MODEL), lambda bi, j: (0, 0))
    vec = pl.BlockSpec((1, D_MODEL), lambda bi, j: (0, 0))
    return pl.pallas_call(
        _outproj_last_kernel,
        grid=(BATCH, SEQ // tl),
        in_specs=[row(A_WIDTH), row(B_WIDTH), row(C_WIDTH), full(A_WIDTH), full(B_WIDTH), full(C_WIDTH),
                  row(D_MODEL), vec, vec],
        out_specs=pl.BlockSpec((1, tl, D_MODEL), lambda bi, j: (bi, j, 0)),
        out_shape=jax.ShapeDtypeStruct((BATCH, SEQ, D_MODEL), jnp.float32),
        compiler_params=_cparams(("arbitrary", "arbitrary")),
        name="outproj_last_residual_ln",
    )(ya.reshape(M_ROWS, A_WIDTH), yb.reshape(M_ROWS, B_WIDTH), yc.reshape(M_ROWS, C_WIDTH), wa, wb, wc, h32,
      g.reshape(1, D_MODEL), b.reshape(1, D_MODEL))


def _regroup_w_in(w):
    def heads(lo, order):
        return [w[:, lo + h * HEAD_DIM: lo + (h + 1) * HEAD_DIM] for h in order]

    c = lambda lo, n: w[:, lo:lo + n]
    o_qa, o_ka, o_va, o_ga = 0, 256, 384, 512
    o_qb, o_kb, o_vb, o_gb = 768, 1280, 1792, 2304
    o_qc, o_ic, o_zf, o_zb, o_gc = 2816, 3072, 3328, 3584, 3840
    cat = lambda xs: jnp.concatenate(xs, axis=1).astype(jnp.bfloat16)
    w_q = cat(heads(o_qa, A_HEAD_ORDER) + [c(o_qb, 512)])
    w_k = cat([c(o_ka, 128), c(o_kb, 512)])
    w_p = cat([c(o_ic, 256), c(o_va, 128), c(o_vb, 512)])
    w_s = cat(heads(o_ga, A_HEAD_ORDER) + [c(o_gb, 512), c(o_gc, 256), c(o_qc, 256)])
    w_g = cat([c(o_zf, 256), c(o_zb, 256)])
    return jnp.concatenate([w_q, w_k, w_p, w_s, w_g], axis=1)


def _rope_tables():
    pos = np.arange(LP, dtype=np.float64) - PAD
    inv = ROPE_THETA ** (-np.arange(0, HEAD_DIM, 2, dtype=np.float64) / HEAD_DIM)
    ang = pos[:, None] * inv[None, :]
    cos, sin = np.cos(ang), np.sin(ang)
    cos_t = np.concatenate([cos, cos, cos, cos], axis=1)
    sin_t = np.concatenate([-sin, sin, -sin, sin], axis=1)
    return jnp.asarray(cos_t, jnp.float32), jnp.asarray(sin_t, jnp.float32)


def kernel(x, meta, emb_ln_g, emb_ln_b, w_in, w_out, a_sink, b_lam, b_subln_w, c_lb_logits, c_norm_w, ln_g, ln_b):
    cos_t, sin_t = _rope_tables()
    p_lb = jax.nn.softmax(c_lb_logits.astype(jnp.float32), axis=0)
    lb_all = jnp.cumsum(p_lb, axis=0) - p_lb[0:1]

    front_args = (x, meta, emb_ln_g, emb_ln_b)
    for l in range(DEPTH):
        lam_init = 0.8 - 0.6 * math.exp(-0.3 * l)
        lp32 = b_lam[l].astype(jnp.float32)
        lam = jnp.exp(jnp.sum(lp32[0] * lp32[1])) - jnp.exp(jnp.sum(lp32[2] * lp32[3])) + lam_init
        wo = w_out[l]
        wa = jnp.concatenate([wo[h * HEAD_DIM:(h + 1) * HEAD_DIM] for h in A_HEAD_ORDER], axis=0).astype(jnp.bfloat16)
        wb = wo[A_WIDTH:A_WIDTH + B_WIDTH].astype(jnp.bfloat16)
        wc = wo[A_WIDTH + B_WIDTH:].astype(jnp.bfloat16)
        sink = jnp.stack([a_sink[l, h] for h in A_HEAD_ORDER]).astype(jnp.float32)

        h32, *proj = _front(front_args, _regroup_w_in(w_in[l]), cos_t, sin_t, lb_all[l].reshape(1, C_WIDTH))
        q_all, k_all, p_all, s_all, lf_all, kk_all = [t.reshape(BATCH, LP, t.shape[-1]) for t in proj]

        ya = _mixer_a(sink, q_all, k_all, p_all, s_all)
        yb = _mixer_b(lam_init, lam.reshape(1), q_all, k_all, p_all, s_all, b_subln_w[l])
        yc = _mixer_c(s_all, p_all, lf_all, kk_all, s_all, c_norm_w[l])

        if l == DEPTH - 1:
            return _outproj_last(ya, yb, yc, wa, wb, wc, h32, ln_g[l], ln_b[l])
        front_args = (ya.reshape(M_ROWS, A_WIDTH), yb.reshape(M_ROWS, B_WIDTH), yc.reshape(M_ROWS, C_WIDTH),
                      wa, wb, wc, h32, ln_g[l], ln_b[l])
```
